```python
import math
import jax, jax.numpy as jnp
from jax import lax
import numpy as np

D_MODEL = 1024
BATCH = 32
SEQ = 2048
DEPTH = 1

D_MIX = D_MODEL
D_A = D_MIX // 2
D_B = D_MIX - D_A
A_GROUPS = 8
A_GROUP_DIM = D_A // A_GROUPS
CHUNK = 128
B_HEADS = 8
HEAD_DIM = D_B // B_HEADS
DILATED_BRANCHES = ((128, 1), (512, 4), (2048, 16))
ROPE_THETA = 10000.0
D_IN = 2 * D_A + 3 * D_B
N_KEYS = 128
N_EXPERTS = N_KEYS * N_KEYS
PEER_HEADS = 8
PEER_TOPK = 16
D_KEY = 256
PEER_TOKEN_BLOCK = 128
EPS = 1e-6

kernel_name = "hybrid_gmlp_dilated_attn_peer_block"


def rms_norm(x, g):
    xf = x.astype(jnp.float32)
    y = xf * lax.rsqrt(jnp.mean(xf * xf, axis=-1, keepdims=True) + EPS)
    return (y * g.astype(jnp.float32)).astype(x.dtype)


def layer_norm(x, g, b):
    xf = x.astype(jnp.float32)
    mu = jnp.mean(xf, axis=-1, keepdims=True)
    var = jnp.mean(jnp.square(xf - mu), axis=-1, keepdims=True)
    y = (xf - mu) * lax.rsqrt(var + EPS)
    return (y * g.astype(jnp.float32) + b.astype(jnp.float32)).astype(x.dtype)


def rope_tables(seq):
    pos = jnp.arange(seq, dtype=jnp.float32)
    inv = 1.0 / (ROPE_THETA ** (jnp.arange(0, HEAD_DIM, 2, dtype=jnp.float32) / HEAD_DIM))
    ang = pos[:, None] * inv[None, :]
    return jnp.cos(ang)[:, None, :], jnp.sin(ang)[:, None, :]


def apply_rope(t, cos, sin):
    tf = t.astype(jnp.float32)
    t1, t2 = tf[..., :HEAD_DIM // 2], tf[..., HEAD_DIM // 2:]
    return jnp.concatenate([t1 * cos - t2 * sin, t2 * cos + t1 * sin], axis=-1).astype(t.dtype)


def spatial_gating(u, v, ln_g, ln_b, w_s, b_s):
    B, S, _ = u.shape
    v = layer_norm(v, ln_g, ln_b)
    vc = v.reshape(B, S // CHUNK, CHUNK, A_GROUPS, A_GROUP_DIM)
    mixed = jnp.einsum('gpq,bnqgc->bnpgc', w_s, vc) + b_s.T[None, None, :, :, None]
    return u * mixed.reshape(B, S, D_A)


def to_residue(t, d):
    B, S = t.shape[:2]
    t = t.reshape((B, S // d, d) + t.shape[2:])
    t = jnp.swapaxes(t, 1, 2)
    return t.reshape((B * d, S // d) + t.shape[3:])


def from_residue(t, d, B):
    L = t.shape[1]
    t = t.reshape((B, d, L) + t.shape[2:])
    t = jnp.swapaxes(t, 1, 2)
    return t.reshape((B, L * d) + t.shape[3:])


def banded_attention(q, k, v, half_window):
    N, L, H, hd = q.shape
    blk = half_window
    nb = -(-L // blk)
    pad = nb * blk - L
    qb = jnp.pad(q, ((0, 0), (0, pad), (0, 0), (0, 0))).reshape(N, nb, blk, H, hd)
    kp = jnp.pad(k, ((0, 0), (blk, pad + blk), (0, 0), (0, 0))).reshape(N, nb + 2, blk, H, hd)
    vp = jnp.pad(v, ((0, 0), (blk, pad + blk), (0, 0), (0, 0))).reshape(N, nb + 2, blk, H, hd)
    kw = jnp.concatenate([kp[:, :-2], kp[:, 1:-1], kp[:, 2:]], axis=2)
    vw = jnp.concatenate([vp[:, :-2], vp[:, 1:-1], vp[:, 2:]], axis=2)
    s = jnp.einsum('nbqhd,nbkhd->nbhqk', qb, kw).astype(jnp.float32) * (HEAD_DIM ** -0.5)
    blocks = jnp.arange(nb)[:, None] * blk
    qpos = blocks + jnp.arange(blk)[None, :]
    kpos = blocks - blk + jnp.arange(3 * blk)[None, :]
    rel = kpos[:, None, :] - qpos[:, :, None]
    valid = (jnp.abs(rel) <= half_window) & (kpos >= 0)[:, None, :] & (kpos < L)[:, None, :]
    s = jnp.where(valid[None, :, None], s, -jnp.inf)
    m = jnp.max(s, axis=-1, keepdims=True)
    p = jnp.exp(s - m)
    l = jnp.sum(p, axis=-1)
    o = jnp.einsum('nbhqk,nbkhd->nbqhd', p, vw.astype(jnp.float32))
    l_t = jnp.swapaxes(l, 2, 3)
    o = o / l_t[..., None]
    lse = jnp.swapaxes(m[..., 0], 2, 3) + jnp.log(l_t)
    o = o.reshape(N, nb * blk, H, hd)[:, :L]
    lse = lse.reshape(N, nb * blk, H)[:, :L]
    return o, lse


def dilated_attention(q, k, v):
    B = q.shape[0]
    outs, lses = [], []
    for window, dil in DILATED_BRANCHES:
        half_window = window // (2 * dil)
        o, lse = banded_attention(to_residue(q, dil), to_residue(k, dil), to_residue(v, dil), half_window)
        outs.append(from_residue(o, dil, B))
        lses.append(from_residue(lse, dil, B))
    w = jax.nn.softmax(jnp.stack(lses, axis=0), axis=0)
    return jnp.sum(w[..., None] * jnp.stack(outs, axis=0), axis=0)


def peer(xn, w_query, sub_keys, expert_u, expert_v):
    B, S, D = xn.shape
    xt = xn.reshape((B * S) // PEER_TOKEN_BLOCK, PEER_TOKEN_BLOCK, D)

    def block(xb):
        q = (xb @ w_query).reshape(PEER_TOKEN_BLOCK, PEER_HEADS, 2, D_KEY // 2)
        s = jnp.einsum('thpc,pkc->thpk', q, sub_keys).astype(jnp.float32)
        s1, i1 = lax.top_k(s[:, :, 0], PEER_TOPK)
        s2, i2 = lax.top_k(s[:, :, 1], PEER_TOPK)
        cand = (s1[..., :, None] + s2[..., None, :]).reshape(PEER_TOKEN_BLOCK, PEER_HEADS, PEER_TOPK * PEER_TOPK)
        sc, ci = lax.top_k(cand, PEER_TOPK)
        e = (jnp.take_along_axis(i1, ci // PEER_TOPK, axis=-1) * N_KEYS
             + jnp.take_along_axis(i2, ci % PEER_TOPK, axis=-1))
        g = jax.nn.softmax(sc, axis=-1)
        u = expert_u[e]
        act = jax.nn.gelu(jnp.einsum('td,thkd->thk', xb, u).astype(jnp.float32))
        return jnp.einsum('thk,thkd->td', (g * act).astype(xb.dtype), expert_v[e])

    return lax.map(block, xt).reshape(B, S, D)


def setup_inputs(seed: int = 0) -> dict:
    key = jax.random.key(seed)
    ks = jax.random.split(key, 17)
    f32 = jnp.float32
    n = lambda k, shape: jax.random.normal(k, shape, dtype=f32)
    return {
        "x": n(ks[0], (BATCH, SEQ, D_MODEL)),
        "norm1_g": 1.0 + 0.02 * n(ks[1], (DEPTH, D_MODEL)),
        "w_in": n(ks[2], (DEPTH, D_MODEL, D_IN)) * D_MODEL ** -0.5,
        "ln_v_g": 1.0 + 0.02 * n(ks[3], (DEPTH, D_A)),
        "ln_v_b": 0.02 * n(ks[4], (DEPTH, D_A)),
        "w_spatial": n(ks[5], (DEPTH, A_GROUPS, CHUNK, CHUNK)) * CHUNK ** -0.5,
        "b_spatial": 0.02 * n(ks[6], (DEPTH, A_GROUPS, CHUNK)),
        "out_norm_a_g": 1.0 + 0.02 * n(ks[7], (DEPTH, D_A)),
        "out_norm_b_g": 1.0 + 0.02 * n(ks[8], (DEPTH, D_B)),
        "w_out": n(ks[9], (DEPTH, D_MIX, D_MODEL)) * D_MIX ** -0.5,
        "norm2_g": 1.0 + 0.02 * n(ks[10], (DEPTH, D_MODEL)),
        "w_query": n(ks[11], (DEPTH, D_MODEL, PEER_HEADS * D_KEY)) * D_MODEL ** -0.5,
        "sub_keys": n(ks[12], (DEPTH, 2, N_KEYS, D_KEY // 2)) * (D_KEY // 2) ** -0.5,
        "expert_u": n(ks[13], (DEPTH, N_EXPERTS, D_MODEL)) * D_MODEL ** -0.5,
        "expert_v": n(ks[14], (DEPTH, N_EXPERTS, D_MODEL)) * PEER_HEADS ** -0.5,
        "final_norm_g": 1.0 + 0.02 * n(ks[15], (D_MODEL,)),
    }


def reference(x, norm1_g, w_in, ln_v_g, ln_v_b, w_spatial, b_spatial, out_norm_a_g, out_norm_b_g,
              w_out, norm2_g, w_query, sub_keys, expert_u, expert_v, final_norm_g):
    B, S, _ = x.shape
    cos, sin = rope_tables(S)
    for layer in range(DEPTH):
        h = rms_norm(x, norm1_g[layer])
        proj = h @ w_in[layer]
        u_a, v_a, q, k, v = jnp.split(
            proj, [D_A, 2 * D_A, 2 * D_A + D_B, 2 * D_A + 2 * D_B], axis=-1)
        a_out = spatial_gating(jax.nn.gelu(u_a), jax.nn.gelu(v_a), ln_v_g[layer], ln_v_b[layer],
                               w_spatial[layer], b_spatial[layer])
        q = apply_rope(q.reshape(B, S, B_HEADS, HEAD_DIM), cos, sin)
        k = apply_rope(k.reshape(B, S, B_HEADS, HEAD_DIM), cos, sin)
        v = v.reshape(B, S, B_HEADS, HEAD_DIM)
        b_out = dilated_attention(q, k, v).reshape(B, S, D_B).astype(x.dtype)
        mix = jnp.concatenate([rms_norm(a_out, out_norm_a_g[layer]),
                               rms_norm(b_out, out_norm_b_g[layer])], axis=-1)
        x = x + mix @ w_out[layer]
        x = x + peer(rms_norm(x, norm2_g[layer]), w_query[layer], sub_keys[layer],
                     expert_u[layer], expert_v[layer])
    return rms_norm(x, final_norm_g)
```

```python
import functools
import math

import jax
import jax.numpy as jnp
from jax import lax
from jax.experimental import pallas as pl
from jax.experimental.pallas import tpu as pltpu

F32 = jnp.float32
BF16 = jnp.bfloat16
I32 = jnp.int32

D_MODEL = 1024
D_A = 512
D_B = 512
A_GROUPS = 8
A_GROUP_DIM = 64
CHUNK = 128
B_HEADS = 8
HEAD_DIM = 64
DILATIONS = (1, 4, 16)
HALF_WINDOW = 64
ROPE_THETA = 10000.0
D_IN = 2 * D_A + 3 * D_B
N_KEYS = 128
PEER_HEADS = 8
PEER_TOPK = 16
D_KEY = 256
N_SLOTS = PEER_HEADS * PEER_TOPK
EPS = 1e-6
NEG_BIG = -1e30

LANES = 128
QBLK = 128
IN_BLOCK = 512
MID_BLOCK = 256
PEER_BLOCK = 128
PEER_RING = 4
VMEM_LIMIT = 48 * 1024 * 1024


def _gelu(x):
    c = math.sqrt(2.0 / math.pi)
    return 0.5 * x * (1.0 + jnp.tanh(c * (x + 0.044715 * (x * x * x))))


def _rms(x, g):
    return x * lax.rsqrt(jnp.mean(x * x, axis=-1, keepdims=True) + EPS) * g


def _in_proj_kernel(x_ref, g1_ref, win_ref, lng_ref, lnb_ref, ws_ref, bs_ref, ga_ref,
                    cos_ref, sin_ref,
                    an_ref, q1_ref, k1_ref, v1_ref, q4_ref, k4_ref, v4_ref,
                    q16_ref, k16_ref, v16_ref, slab_ref):
    nt = x_ref.shape[0]
    h = _rms(x_ref[...], g1_ref[...]).astype(BF16)
    proj = jnp.dot(h, win_ref[...], preferred_element_type=F32)

    u = _gelu(proj[:, :D_A])
    v = _gelu(proj[:, D_A:2 * D_A])
    mu = jnp.mean(v, axis=-1, keepdims=True)
    vc = v - mu
    var = jnp.mean(vc * vc, axis=-1, keepdims=True)
    vln = (vc * lax.rsqrt(var + EPS) * lng_ref[...] + lnb_ref[...]).astype(BF16)
    lane = lax.broadcasted_iota(I32, (CHUNK, LANES), 1)
    lo = lane < A_GROUP_DIM
    zero = jnp.zeros((CHUNK, LANES), BF16)
    chunks = []
    for c in range(nt // CHUNK):
        cols = []
        for j in range(A_GROUPS // 2):
            vv = vln[c * CHUNK:(c + 1) * CHUNK, j * LANES:(j + 1) * LANES]
            rhs = jnp.concatenate([jnp.where(lo, vv, zero), jnp.where(lo, zero, vv)], axis=0)
            cols.append(jnp.dot(ws_ref[j], rhs, preferred_element_type=F32))
        chunks.append(jnp.concatenate(cols, axis=1) + bs_ref[...])
    mixed = jnp.concatenate(chunks, axis=0)
    an_ref[...] = _rms(u * mixed, ga_ref[...]).astype(BF16)

    cosf = cos_ref[...]
    sins = sin_ref[...]
    lane_b = lax.broadcasted_iota(I32, (nt, D_B), 1)
    first_half = (lane_b % HEAD_DIM) < (HEAD_DIM // 2)

    def rope(t):
        partner = jnp.where(first_half, pltpu.roll(t, D_B - HEAD_DIM // 2, 1),
                            pltpu.roll(t, HEAD_DIM // 2, 1))
        return t * cosf + partner * sins

    q = rope(proj[:, 2 * D_A:2 * D_A + D_B]) * (HEAD_DIM ** -0.5)
    k = rope(proj[:, 2 * D_A + D_B:2 * D_A + 2 * D_B])
    vv = proj[:, 2 * D_A + 2 * D_B:]
    q1_ref[...] = q.astype(BF16)
    k1_ref[...] = k.astype(BF16)
    v1_ref[...] = vv.astype(BF16)

    nslab = D_B // LANES
    for a, t in enumerate((q, k, vv)):
        for s in range(nslab):
            slab_ref[a * nslab + s] = t[:, s * LANES:(s + 1) * LANES]
    for d, outs in ((4, (q4_ref, k4_ref, v4_ref)), (16, (q16_ref, k16_ref, v16_ref))):
        rows = nt // d
        for a, o_ref in enumerate(outs):
            for r in range(d):
                for s in range(nslab):
                    o_ref[r, :, s * LANES:(s + 1) * LANES] = (
                        slab_ref[a * nslab + s, pl.ds(r, rows, stride=d), :].astype(BF16))


def _in_proj(x2, g1, win, lng, lnb, ws_cat, bs_full, ga, cosf, sins, batch, seq):
    t_total = x2.shape[0]
    nt = IN_BLOCK
    nb = seq // nt
    grid = (t_total // nt,)
    row = lambda i: (i, 0)
    const2 = lambda i: (0, 0)
    tok_bf = jax.ShapeDtypeStruct((t_total, D_B), BF16)
    out_shape = (
        jax.ShapeDtypeStruct((t_total, D_A), BF16),
        tok_bf, tok_bf, tok_bf,
        *(jax.ShapeDtypeStruct((batch, 4, seq // 4, D_B), BF16),) * 3,
        *(jax.ShapeDtypeStruct((batch, 16, seq // 16, D_B), BF16),) * 3,
    )
    res4 = pl.BlockSpec((None, 4, nt // 4, D_B), lambda i: (i // nb, 0, i % nb, 0))
    res16 = pl.BlockSpec((None, 16, nt // 16, D_B), lambda i: (i // nb, 0, i % nb, 0))
    tok_spec = pl.BlockSpec((nt, D_B), row)
    return pl.pallas_call(
        _in_proj_kernel,
        grid=grid,
        in_specs=[
            pl.BlockSpec((nt, D_MODEL), row),
            pl.BlockSpec((1, D_MODEL), const2),
            pl.BlockSpec((D_MODEL, D_IN), const2),
            pl.BlockSpec((1, D_A), const2),
            pl.BlockSpec((1, D_A), const2),
            pl.BlockSpec((A_GROUPS // 2, CHUNK, 2 * CHUNK), lambda i: (0, 0, 0)),
            pl.BlockSpec((CHUNK, D_A), const2),
            pl.BlockSpec((1, D_A), const2),
            pl.BlockSpec((nt, D_B), lambda i: (i % nb, 0)),
            pl.BlockSpec((nt, D_B), lambda i: (i % nb, 0)),
        ],
        out_specs=(pl.BlockSpec((nt, D_A), row), tok_spec, tok_spec, tok_spec,
                   res4, res4, res4, res16, res16, res16),
        out_shape=out_shape,
        scratch_shapes=[pltpu.VMEM((3 * D_B // LANES, nt, LANES), F32)],
        compiler_params=pltpu.CompilerParams(
            dimension_semantics=("arbitrary",), vmem_limit_bytes=VMEM_LIMIT),
        name="in_proj",
    )(x2, g1, win, lng, lnb, ws_cat, bs_full, ga, cosf, sins)


def _attn_kernel(q1_ref, k1_ref, v1_ref, q4_ref, k4_ref, v4_ref, q16_ref, k16_ref, v16_ref,
                 o_ref, acc_ref, m_ref, l_ref):
    seq = o_ref.shape[0]
    lane = lax.broadcasted_iota(I32, (QBLK, LANES), 1)
    head0 = lane < HEAD_DIM
    branches = ((1, q1_ref, k1_ref, v1_ref), (4, q4_ref, k4_ref, v4_ref),
                (16, q16_ref, k16_ref, v16_ref))
    for bi, (d, q_ref, k_ref, v_ref) in enumerate(branches):
        length = seq // d
        nblk = length // QBLK
        win = min(2 * QBLK, length)
        diff = (lax.broadcasted_iota(I32, (QBLK, win), 1)
                - lax.broadcasted_iota(I32, (QBLK, win), 0))

        def block(blk, carry, d=d, bi=bi, q_ref=q_ref, k_ref=k_ref, v_ref=v_ref,
                  length=length, nblk=nblk, win=win, diff=diff):
            r = blk // nblk
            i0 = pl.multiple_of((blk % nblk) * QBLK, QBLK)
            w0 = pl.multiple_of(jnp.clip(i0 - HALF_WINDOW, 0, length - win), HALF_WINDOW)
            qb = q_ref[r, pl.ds(i0, QBLK), :]
            kw = k_ref[r, pl.ds(w0, win), :]
            vw = v_ref[r, pl.ds(w0, win), :]
            rel = diff + (w0 - i0)
            valid = (rel >= -HALF_WINDOW) & (rel <= HALF_WINDOW)
            accs, ms, ls = [], [], []
            for hsel in (head0, ~head0):
                qh = jnp.where(hsel, qb, jnp.zeros_like(qb))
                s = lax.dot_general(qh, kw, (((1,), (1,)), ((), ())),
                                    preferred_element_type=F32)
                s = jnp.where(valid, s, NEG_BIG)
                m = jnp.max(s, axis=1, keepdims=True)
                p = jnp.exp(s - m)
                ls.append(jnp.sum(p, axis=1, keepdims=True))
                ms.append(m)
                accs.append(jnp.dot(p.astype(BF16), vw, preferred_element_type=F32))
            acc = jnp.where(head0, accs[0], accs[1])
            mm = jnp.where(head0, ms[0], ms[1])
            ll = jnp.where(head0, ls[0], ls[1])
            if d == 1:
                rows = pl.ds(i0, QBLK)
            else:
                rows = pl.ds(i0 * d + r, QBLK, stride=d)
            acc_ref[bi, rows, :] = acc
            m_ref[bi, rows, :] = mm
            l_ref[bi, rows, :] = ll
            return carry

        lax.fori_loop(0, d * nblk, block, 0)

    def merge(c, carry):
        rows = pl.ds(pl.multiple_of(c * QBLK, QBLK), QBLK)
        m1, m2, m3 = m_ref[0, rows, :], m_ref[1, rows, :], m_ref[2, rows, :]
        mx = jnp.maximum(jnp.maximum(m1, m2), m3)
        w1, w2, w3 = jnp.exp(m1 - mx), jnp.exp(m2 - mx), jnp.exp(m3 - mx)
        num = w1 * acc_ref[0, rows, :] + w2 * acc_ref[1, rows, :] + w3 * acc_ref[2, rows, :]
        den = w1 * l_ref[0, rows, :] + w2 * l_ref[1, rows, :] + w3 * l_ref[2, rows, :]
        o_ref[rows, :] = num / den
        return carry

    lax.fori_loop(0, seq // QBLK, merge, 0)


def _attention(q1, k1, v1, q4, k4, v4, q16, k16, v16, batch, seq):
    npair = D_B // LANES
    nat = pl.BlockSpec((None, 1, seq, LANES), lambda b, p: (b, 0, 0, p))
    r4 = pl.BlockSpec((None, 4, seq // 4, LANES), lambda b, p: (b, 0, 0, p))
    r16 = pl.BlockSpec((None, 16, seq // 16, LANES), lambda b, p: (b, 0, 0, p))
    q1, k1, v1 = (t.reshape(batch, 1, seq, D_B) for t in (q1, k1, v1))
    return pl.pallas_call(
        _attn_kernel,
        grid=(batch, npair),
        in_specs=[nat, nat, nat, r4, r4, r4, r16, r16, r16],
        out_specs=pl.BlockSpec((None, seq, LANES), lambda b, p: (b, 0, p)),
        out_shape=jax.ShapeDtypeStruct((batch, seq, D_B), F32),
        scratch_shapes=[pltpu.VMEM((3, seq, LANES), F32)] * 3,
        compiler_params=pltpu.CompilerParams(
            dimension_semantics=("arbitrary", "arbitrary"), vmem_limit_bytes=VMEM_LIMIT),
        name="dilated_attn",
    )(q1, k1, v1, q4, k4, v4, q16, k16, v16)


def _topk_rows(s, k):
    n = s.shape[0]
    iota = lax.broadcasted_iota(I32, s.shape, 0)
    vals, idxs = [], []
    for _ in range(k):
        m = jnp.max(s, axis=0, keepdims=True)
        i = jnp.min(jnp.where(s == m, iota, n), axis=0, keepdims=True)
        vals.append(m)
        idxs.append(i)
        s = jnp.where(iota == i, -jnp.inf, s)
    return jnp.concatenate(vals, axis=0), jnp.concatenate(idxs, axis=0)


def _take_rows(table, sel):
    out = jnp.zeros(sel.shape, table.dtype)
    for a in range(table.shape[0]):
        out = jnp.where(sel == a, table[a:a + 1, :], out)
    return out


def _mid_kernel(x_ref, an_ref, bo_ref, gb_ref, wout_ref, g2_ref, wq_ref, keys_ref,
                x1_ref, xn_ref, idx_ref, gate_ref):
    nt = x_ref.shape[0]
    bn = _rms(bo_ref[...], gb_ref[...]).astype(BF16)
    x1 = (x_ref[...]
          + jnp.dot(an_ref[...], wout_ref[:D_A, :], preferred_element_type=F32)
          + jnp.dot(bn, wout_ref[D_A:, :], preferred_element_type=F32))
    x1_ref[...] = x1
    xn = _rms(x1, g2_ref[...])
    xn_ref[...] = xn
    q = jnp.dot(xn.astype(BF16), wq_ref[...], preferred_element_type=F32).astype(BF16)
    keys = (keys_ref[0], keys_ref[1])
    half = D_KEY // 2
    for c in range(nt // LANES):
        qc = q[c * LANES:(c + 1) * LANES, :]
        experts, gates = [], []
        for h in range(PEER_HEADS):
            tops = []
            for p in range(2):
                qhp = qc[:, (2 * h + p) * half:(2 * h + p + 1) * half]
                s = lax.dot_general(keys[p], qhp, (((1,), (1,)), ((), ())),
                                    preferred_element_type=F32)
                tops.append(_topk_rows(s, PEER_TOPK))
            (s1, i1), (s2, i2) = tops
            cand = jnp.concatenate([s1[a:a + 1, :] + s2 for a in range(PEER_TOPK)], axis=0)
            sc, ci = _topk_rows(cand, PEER_TOPK)
            e = (_take_rows(i1, ci >> 4) * N_KEYS + _take_rows(i2, ci & (PEER_TOPK - 1)))
            ex = jnp.exp(sc - sc[0:1, :])
            gates.append(ex / jnp.sum(ex, axis=0, keepdims=True))
            experts.append(e)
        idx_ref[c] = jnp.concatenate(experts, axis=0)
        gate_ref[c * LANES:(c + 1) * LANES, :] = jnp.concatenate(gates, axis=0).T


def _mid(x2, an, bo, gb, wout, g2, wq, keys):
    t_total = x2.shape[0]
    nt = MID_BLOCK
    row = lambda i: (i, 0)
    const2 = lambda i: (0, 0)
    return pl.pallas_call(
        _mid_kernel,
        grid=(t_total // nt,),
        in_specs=[
            pl.BlockSpec((nt, D_MODEL), row),
            pl.BlockSpec((nt, D_A), row),
            pl.BlockSpec((nt, D_B), row),
            pl.BlockSpec((1, D_B), const2),
            pl.BlockSpec((D_MODEL, D_MODEL), const2),
            pl.BlockSpec((1, D_MODEL), const2),
            pl.BlockSpec((D_MODEL, PEER_HEADS * D_KEY), const2),
            pl.BlockSpec((2, N_KEYS, D_KEY // 2), lambda i: (0, 0, 0)),
        ],
        out_specs=(
            pl.BlockSpec((nt, D_MODEL), row),
            pl.BlockSpec((nt, D_MODEL), row),
            pl.BlockSpec((nt // LANES, N_SLOTS, LANES), lambda i: (i, 0, 0)),
            pl.BlockSpec((nt, N_SLOTS), row),
        ),
        out_shape=(
            jax.ShapeDtypeStruct((t_total, D_MODEL), F32),
            jax.ShapeDtypeStruct((t_total, D_MODEL), F32),
            jax.ShapeDtypeStruct((t_total // LANES, N_SLOTS, LANES), I32),
            jax.ShapeDtypeStruct((t_total, N_SLOTS), F32),
        ),
        compiler_params=pltpu.CompilerParams(
            dimension_semantics=("arbitrary",), vmem_limit_bytes=VMEM_LIMIT),
        name="mid",
    )(x2, an, bo, gb, wout, g2, wq, keys)


def _peer_kernel(idx_ref, gate_ref, xn_ref, x1_ref, gf_ref, uv_ref, y_ref, rows_ref, sem_ref):
    nt = xn_ref.shape[0]

    def row_copy(e, slot, s):
        return pltpu.make_async_copy(uv_ref.at[pl.ds(e, 1), :],
                                     rows_ref.at[slot, pl.ds(s, 1), :], sem_ref.at[slot])

    def issue(t):
        slot = t % PEER_RING

        def body(s, carry):
            row_copy(idx_ref[s, t], slot, s).start()
            return carry

        lax.fori_loop(0, N_SLOTS, body, 0, unroll=8)

    for t in range(PEER_RING - 1):
        issue(t)

    def token(t, carry):
        @pl.when(t + PEER_RING - 1 < nt)
        def _():
            issue(t + PEER_RING - 1)

        slot = t % PEER_RING
        pltpu.make_async_copy(uv_ref.at[pl.ds(0, N_SLOTS), :], rows_ref.at[slot],
                              sem_ref.at[slot]).wait()
        rows = rows_ref[slot]
        u = rows[:, :D_MODEL].astype(BF16)
        v = rows[:, D_MODEL:].astype(BF16)
        xr = xn_ref[pl.ds(t, 1), :]
        x8 = jnp.broadcast_to(xr, (8, D_MODEL)).astype(BF16)
        act = lax.dot_general(x8, u, (((1,), (1,)), ((), ())), preferred_element_type=F32)
        w = (_gelu(act) * gate_ref[pl.ds(t, 1), :]).astype(BF16)
        out = jnp.dot(w, v, preferred_element_type=F32)[0:1, :]
        y_ref[pl.ds(t, 1), :] = _rms(x1_ref[pl.ds(t, 1), :] + out, gf_ref[...])
        return carry

    lax.fori_loop(0, nt, token, 0)


def _peer(idx, gates, xn, x1, gf, uv):
    t_total = xn.shape[0]
    nt = PEER_BLOCK
    row = lambda i: (i, 0)
    return pl.pallas_call(
        _peer_kernel,
        grid=(t_total // nt,),
        in_specs=[
            pl.BlockSpec((None, N_SLOTS, nt), lambda i: (i, 0, 0), memory_space=pltpu.SMEM),
            pl.BlockSpec((nt, N_SLOTS), row),
            pl.BlockSpec((nt, D_MODEL), row),
            pl.BlockSpec((nt, D_MODEL), row),
            pl.BlockSpec((1, D_MODEL), lambda i: (0, 0)),
            pl.BlockSpec(memory_space=pl.ANY),
        ],
        out_specs=pl.BlockSpec((nt, D_MODEL), row),
        out_shape=jax.ShapeDtypeStruct((t_total, D_MODEL), F32),
        scratch_shapes=[pltpu.VMEM((PEER_RING, N_SLOTS, 2 * D_MODEL), F32),
                        pltpu.SemaphoreType.DMA((PEER_RING,))],
        compiler_params=pltpu.CompilerParams(
            dimension_semantics=("arbitrary",), vmem_limit_bytes=VMEM_LIMIT),
        name="peer",
    )(idx, gates, xn, x1, gf, uv)


def _rope_tables(seq):
    pos = jnp.arange(seq, dtype=F32)
    inv = 1.0 / (ROPE_THETA ** (jnp.arange(0, HEAD_DIM, 2, dtype=F32) / HEAD_DIM))
    ang = pos[:, None] * inv[None, :]
    cos, sin = jnp.cos(ang), jnp.sin(ang)
    cosf = jnp.tile(jnp.concatenate([cos, cos], axis=1), (1, B_HEADS))
    sins = jnp.tile(jnp.concatenate([-sin, sin], axis=1), (1, B_HEADS))
    return cosf, sins


def kernel(x, norm1_g, w_in, ln_v_g, ln_v_b, w_spatial, b_spatial, out_norm_a_g, out_norm_b_g,
           w_out, norm2_g, w_query, sub_keys, expert_u, expert_v, final_norm_g):
    batch, seq, _ = x.shape
    assert w_in.shape[0] == 1 and seq % (16 * QBLK) == 0 and seq % IN_BLOCK == 0
    x2 = x.reshape(batch * seq, D_MODEL)
    row = lambda g: g.reshape(1, -1).astype(F32)

    ws = w_spatial[0].astype(BF16)
    ws_cat = jnp.concatenate([ws[0::2], ws[1::2]], axis=2)
    bs_full = jnp.repeat(b_spatial[0].T, A_GROUP_DIM, axis=1)
    cosf, sins = _rope_tables(seq)

    an, q1, k1, v1, q4, k4, v4, q16, k16, v16 = _in_proj(
        x2, row(norm1_g[0]), w_in[0].astype(BF16), row(ln_v_g[0]), row(ln_v_b[0]),
        ws_cat, bs_full, row(out_norm_a_g[0]), cosf, sins, batch, seq)

    bo = _attention(q1, k1, v1, q4, k4, v4, q16, k16, v16, batch, seq)
    bo = bo.reshape(batch * seq, D_B)

    x1, xn, idx, gates = _mid(
        x2, an, bo, row(out_norm_b_g[0]), w_out[0].astype(BF16), row(norm2_g[0]),
        w_query[0].astype(BF16), sub_keys[0].astype(BF16))

    uv = jnp.concatenate([expert_u[0], expert_v[0]], axis=1)
    y = _peer(idx, gates, xn, x1, row(final_norm_g), uv)
    return y.reshape(batch, seq, D_MODEL)
```

```python
import functools
import math

import jax
import jax.numpy as jnp
from jax import lax
from jax.experimental import pallas as pl
from jax.experimental.pallas import tpu as pltpu

F32 = jnp.float32
BF16 = jnp.bfloat16
I32 = jnp.int32

D_MODEL = 1024
D_A = 512
D_B = 512
A_GROUPS = 8
A_GROUP_DIM = 64
CHUNK = 128
B_HEADS = 8
HEAD_DIM = 64
DILATIONS = (1, 4, 16)
HALF_WINDOW = 64
ROPE_THETA = 10000.0
D_IN = 2 * D_A + 3 * D_B
N_KEYS = 128
PEER_HEADS = 8
PEER_TOPK = 16
D_KEY = 256
N_SLOTS = PEER_HEADS * PEER_TOPK
EPS = 1e-6
NEG_BIG = -1e30

LANES = 128
SUB = 8
NCH = 2 * D_MODEL // LANES
QBLK = 128
IN_BLOCK = 512
MID_BLOCK = 256
PEER_BLOCK = 128
PEER_RING = 4
VMEM_LIMIT = 48 * 1024 * 1024


def _gelu(x):
    c = math.sqrt(2.0 / math.pi)
    return 0.5 * x * (1.0 + jnp.tanh(c * (x + 0.044715 * (x * x * x))))


def _rms(x, g):
    return x * lax.rsqrt(jnp.mean(x * x, axis=-1, keepdims=True) + EPS) * g


def _in_proj_kernel(x_ref, g1_ref, win_ref, lng_ref, lnb_ref, ws_ref, bs_ref, ga_ref,
                    cos_ref, sin_ref,
                    an_ref, q1_ref, k1_ref, v1_ref, q4_ref, k4_ref, v4_ref,
                    q16_ref, k16_ref, v16_ref, slab_ref):
    nt = x_ref.shape[0]
    h = _rms(x_ref[...], g1_ref[...]).astype(BF16)
    proj = jnp.dot(h, win_ref[...], preferred_element_type=F32)

    u = _gelu(proj[:, :D_A])
    v = _gelu(proj[:, D_A:2 * D_A])
    mu = jnp.mean(v, axis=-1, keepdims=True)
    vc = v - mu
    var = jnp.mean(vc * vc, axis=-1, keepdims=True)
    vln = (vc * lax.rsqrt(var + EPS) * lng_ref[...] + lnb_ref[...]).astype(BF16)
    lane = lax.broadcasted_iota(I32, (CHUNK, LANES), 1)
    lo = lane < A_GROUP_DIM
    zero = jnp.zeros((CHUNK, LANES), BF16)
    chunks = []
    for c in range(nt // CHUNK):
        cols = []
        for j in range(A_GROUPS // 2):
            vv = vln[c * CHUNK:(c + 1) * CHUNK, j * LANES:(j + 1) * LANES]
            rhs = jnp.concatenate([jnp.where(lo, vv, zero), jnp.where(lo, zero, vv)], axis=0)
            cols.append(jnp.dot(ws_ref[j], rhs, preferred_element_type=F32))
        chunks.append(jnp.concatenate(cols, axis=1) + bs_ref[...])
    mixed = jnp.concatenate(chunks, axis=0)
    an_ref[...] = _rms(u * mixed, ga_ref[...]).astype(BF16)

    cosf = cos_ref[...]
    sins = sin_ref[...]
    lane_b = lax.broadcasted_iota(I32, (nt, D_B), 1)
    first_half = (lane_b % HEAD_DIM) < (HEAD_DIM // 2)

    def rope(t):
        partner = jnp.where(first_half, pltpu.roll(t, D_B - HEAD_DIM // 2, 1),
                            pltpu.roll(t, HEAD_DIM // 2, 1))
        return t * cosf + partner * sins

    q = rope(proj[:, 2 * D_A:2 * D_A + D_B]) * (HEAD_DIM ** -0.5)
    k = rope(proj[:, 2 * D_A + D_B:2 * D_A + 2 * D_B])
    vv = proj[:, 2 * D_A + 2 * D_B:]
    q1_ref[...] = q.astype(BF16)
    k1_ref[...] = k.astype(BF16)
    v1_ref[...] = vv.astype(BF16)

    nslab = D_B // LANES
    for a, t in enumerate((q, k, vv)):
        for s in range(nslab):
            slab_ref[a * nslab + s] = t[:, s * LANES:(s + 1) * LANES]
    for d, outs in ((4, (q4_ref, k4_ref, v4_ref)), (16, (q16_ref, k16_ref, v16_ref))):
        rows = nt // d
        for a, o_ref in enumerate(outs):
            for r in range(d):
                for s in range(nslab):
                    o_ref[r, :, s * LANES:(s + 1) * LANES] = (
                        slab_ref[a * nslab + s, pl.ds(r, rows, stride=d), :].astype(BF16))


def _in_proj(x2, g1, win, lng, lnb, ws_cat, bs_full, ga, cosf, sins, batch, seq):
    t_total = x2.shape[0]
    nt = IN_BLOCK
    nb = seq // nt
    grid = (t_total // nt,)
    row = lambda i: (i, 0)
    const2 = lambda i: (0, 0)
    tok_bf = jax.ShapeDtypeStruct((t_total, D_B), BF16)
    out_shape = (
        jax.ShapeDtypeStruct((t_total, D_A), BF16),
        tok_bf, tok_bf, tok_bf,
        *(jax.ShapeDtypeStruct((batch, 4, seq // 4, D_B), BF16),) * 3,
        *(jax.ShapeDtypeStruct((batch, 16, seq // 16, D_B), BF16),) * 3,
    )
    res4 = pl.BlockSpec((None, 4, nt // 4, D_B), lambda i: (i // nb, 0, i % nb, 0))
    res16 = pl.BlockSpec((None, 16, nt // 16, D_B), lambda i: (i // nb, 0, i % nb, 0))
    tok_spec = pl.BlockSpec((nt, D_B), row)
    return pl.pallas_call(
        _in_proj_kernel,
        grid=grid,
        in_specs=[
            pl.BlockSpec((nt, D_MODEL), row),
            pl.BlockSpec((1, D_MODEL), const2),
            pl.BlockSpec((D_MODEL, D_IN), const2),
            pl.BlockSpec((1, D_A), const2),
            pl.BlockSpec((1, D_A), const2),
            pl.BlockSpec((A_GROUPS // 2, CHUNK, 2 * CHUNK), lambda i: (0, 0, 0)),
            pl.BlockSpec((CHUNK, D_A), const2),
            pl.BlockSpec((1, D_A), const2),
            pl.BlockSpec((nt, D_B), lambda i: (i % nb, 0)),
            pl.BlockSpec((nt, D_B), lambda i: (i % nb, 0)),
        ],
        out_specs=(pl.BlockSpec((nt, D_A), row), tok_spec, tok_spec, tok_spec,
                   res4, res4, res4, res16, res16, res16),
        out_shape=out_shape,
        scratch_shapes=[pltpu.VMEM((3 * D_B // LANES, nt, LANES), F32)],
        compiler_params=pltpu.CompilerParams(
            dimension_semantics=("arbitrary",), vmem_limit_bytes=VMEM_LIMIT),
        name="in_proj",
    )(x2, g1, win, lng, lnb, ws_cat, bs_full, ga, cosf, sins)


def _attn_kernel(q1_ref, k1_ref, v1_ref, q4_ref, k4_ref, v4_ref, q16_ref, k16_ref, v16_ref,
                 o_ref, acc_ref, m_ref, l_ref):
    seq = o_ref.shape[0]
    lane = lax.broadcasted_iota(I32, (QBLK, LANES), 1)
    head0 = lane < HEAD_DIM
    branches = ((1, q1_ref, k1_ref, v1_ref), (4, q4_ref, k4_ref, v4_ref),
                (16, q16_ref, k16_ref, v16_ref))
    for bi, (d, q_ref, k_ref, v_ref) in enumerate(branches):
        length = seq // d
        nblk = length // QBLK
        win = min(2 * QBLK, length)
        diff = (lax.broadcasted_iota(I32, (QBLK, win), 1)
                - lax.broadcasted_iota(I32, (QBLK, win), 0))

        def block(blk, carry, d=d, bi=bi, q_ref=q_ref, k_ref=k_ref, v_ref=v_ref,
                  length=length, nblk=nblk, win=win, diff=diff):
            r = blk // nblk
            i0 = pl.multiple_of((blk % nblk) * QBLK, QBLK)
            w0 = pl.multiple_of(jnp.clip(i0 - HALF_WINDOW, 0, length - win), HALF_WINDOW)
            qb = q_ref[r, pl.ds(i0, QBLK), :]
            kw = k_ref[r, pl.ds(w0, win), :]
            vw = v_ref[r, pl.ds(w0, win), :]
            rel = diff + (w0 - i0)
            valid = (rel >= -HALF_WINDOW) & (rel <= HALF_WINDOW)
            accs, ms, ls = [], [], []
            for hsel in (head0, ~head0):
                qh = jnp.where(hsel, qb, jnp.zeros_like(qb))
                s = lax.dot_general(qh, kw, (((1,), (1,)), ((), ())),
                                    preferred_element_type=F32)
                s = jnp.where(valid, s, NEG_BIG)
                m = jnp.max(s, axis=1, keepdims=True)
                p = jnp.exp(s - m)
                ls.append(jnp.sum(p, axis=1, keepdims=True))
                ms.append(m)
                accs.append(jnp.dot(p.astype(BF16), vw, preferred_element_type=F32))
            acc = jnp.where(head0, accs[0], accs[1])
            mm = jnp.where(head0, ms[0], ms[1])
            ll = jnp.where(head0, ls[0], ls[1])
            if d == 1:
                rows = pl.ds(i0, QBLK)
            else:
                rows = pl.ds(i0 * d + r, QBLK, stride=d)
            acc_ref[bi, rows, :] = acc
            m_ref[bi, rows, :] = mm
            l_ref[bi, rows, :] = ll
            return carry

        lax.fori_loop(0, d * nblk, block, 0)

    def merge(c, carry):
        rows = pl.ds(pl.multiple_of(c * QBLK, QBLK), QBLK)
        m1, m2, m3 = m_ref[0, rows, :], m_ref[1, rows, :], m_ref[2, rows, :]
        mx = jnp.maximum(jnp.maximum(m1, m2), m3)
        w1, w2, w3 = jnp.exp(m1 - mx), jnp.exp(m2 - mx), jnp.exp(m3 - mx)
        num = w1 * acc_ref[0, rows, :] + w2 * acc_ref[1, rows, :] + w3 * acc_ref[2, rows, :]
        den = w1 * l_ref[0, rows, :] + w2 * l_ref[1, rows, :] + w3 * l_ref[2, rows, :]
        o_ref[rows, :] = num / den
        return carry

    lax.fori_loop(0, seq // QBLK, merge, 0)


def _attention(q1, k1, v1, q4, k4, v4, q16, k16, v16, batch, seq):
    npair = D_B // LANES
    nat = pl.BlockSpec((None, 1, seq, LANES), lambda b, p: (b, 0, 0, p))
    r4 = pl.BlockSpec((None, 4, seq // 4, LANES), lambda b, p: (b, 0, 0, p))
    r16 = pl.BlockSpec((None, 16, seq // 16, LANES), lambda b, p: (b, 0, 0, p))
    q1, k1, v1 = (t.reshape(batch, 1, seq, D_B) for t in (q1, k1, v1))
    return pl.pallas_call(
        _attn_kernel,
        grid=(batch, npair),
        in_specs=[nat, nat, nat, r4, r4, r4, r16, r16, r16],
        out_specs=pl.BlockSpec((None, seq, LANES), lambda b, p: (b, 0, p)),
        out_shape=jax.ShapeDtypeStruct((batch, seq, D_B), F32),
        scratch_shapes=[pltpu.VMEM((3, seq, LANES), F32)] * 3,
        compiler_params=pltpu.CompilerParams(
            dimension_semantics=("arbitrary", "arbitrary"), vmem_limit_bytes=VMEM_LIMIT),
        name="dilated_attn",
    )(q1, k1, v1, q4, k4, v4, q16, k16, v16)


def _topk_rows(s, k):
    n = s.shape[0]
    iota = lax.broadcasted_iota(I32, s.shape, 0)
    vals, idxs = [], []
    for _ in range(k):
        m = jnp.max(s, axis=0, keepdims=True)
        i = jnp.min(jnp.where(s == m, iota, n), axis=0, keepdims=True)
        vals.append(m)
        idxs.append(i)
        s = jnp.where(iota == i, -jnp.inf, s)
    return jnp.concatenate(vals, axis=0), jnp.concatenate(idxs, axis=0)


def _take_rows(table, sel):
    out = jnp.zeros(sel.shape, table.dtype)
    for a in range(table.shape[0]):
        out = jnp.where(sel == a, table[a:a + 1, :], out)
    return out


def _mid_kernel(x_ref, an_ref, bo_ref, gb_ref, wout_ref, g2_ref, wq_ref, keys_ref,
                x1_ref, xn_ref, idx_ref, gate_ref):
    nt = x_ref.shape[0]
    bn = _rms(bo_ref[...], gb_ref[...]).astype(BF16)
    x1 = (x_ref[...]
          + jnp.dot(an_ref[...], wout_ref[:D_A, :], preferred_element_type=F32)
          + jnp.dot(bn, wout_ref[D_A:, :], preferred_element_type=F32))
    x1_ref[...] = x1
    xn = _rms(x1, g2_ref[...])
    xn_ref[...] = xn
    q = jnp.dot(xn.astype(BF16), wq_ref[...], preferred_element_type=F32).astype(BF16)
    keys = (keys_ref[0], keys_ref[1])
    half = D_KEY // 2
    for c in range(nt // LANES):
        qc = q[c * LANES:(c + 1) * LANES, :]
        experts, gates = [], []
        for h in range(PEER_HEADS):
            tops = []
            for p in range(2):
                qhp = qc[:, (2 * h + p) * half:(2 * h + p + 1) * half]
                s = lax.dot_general(keys[p], qhp, (((1,), (1,)), ((), ())),
                                    preferred_element_type=F32)
                tops.append(_topk_rows(s, PEER_TOPK))
            (s1, i1), (s2, i2) = tops
            cand = jnp.concatenate([s1[a:a + 1, :] + s2 for a in range(PEER_TOPK)], axis=0)
            sc, ci = _topk_rows(cand, PEER_TOPK)
            e = (_take_rows(i1, ci >> 4) * N_KEYS + _take_rows(i2, ci & (PEER_TOPK - 1)))
            ex = jnp.exp(sc - sc[0:1, :])
            gates.append(ex / jnp.sum(ex, axis=0, keepdims=True))
            experts.append(e)
        idx_ref[c] = jnp.concatenate(experts, axis=0).T
        gate_ref[c * LANES:(c + 1) * LANES, :] = jnp.concatenate(gates, axis=0).T


def _mid(x2, an, bo, gb, wout, g2, wq, keys):
    t_total = x2.shape[0]
    nt = MID_BLOCK
    row = lambda i: (i, 0)
    const2 = lambda i: (0, 0)
    return pl.pallas_call(
        _mid_kernel,
        grid=(t_total // nt,),
        in_specs=[
            pl.BlockSpec((nt, D_MODEL), row),
            pl.BlockSpec((nt, D_A), row),
            pl.BlockSpec((nt, D_B), row),
            pl.BlockSpec((1, D_B), const2),
            pl.BlockSpec((D_MODEL, D_MODEL), const2),
            pl.BlockSpec((1, D_MODEL), const2),
            pl.BlockSpec((D_MODEL, PEER_HEADS * D_KEY), const2),
            pl.BlockSpec((2, N_KEYS, D_KEY // 2), lambda i: (0, 0, 0)),
        ],
        out_specs=(
            pl.BlockSpec((nt, D_MODEL), row),
            pl.BlockSpec((nt, D_MODEL), row),
            pl.BlockSpec((nt // LANES, LANES, N_SLOTS), lambda i: (i, 0, 0)),
            pl.BlockSpec((nt, N_SLOTS), row),
        ),
        out_shape=(
            jax.ShapeDtypeStruct((t_total, D_MODEL), F32),
            jax.ShapeDtypeStruct((t_total, D_MODEL), F32),
            jax.ShapeDtypeStruct((t_total // LANES, LANES, N_SLOTS), I32),
            jax.ShapeDtypeStruct((t_total, N_SLOTS), F32),
        ),
        compiler_params=pltpu.CompilerParams(
            dimension_semantics=("arbitrary",), vmem_limit_bytes=VMEM_LIMIT),
        name="mid",
    )(x2, an, bo, gb, wout, g2, wq, keys)


def _peer_kernel(idx_ref, gate_ref, xn_ref, x1_ref, gf_ref, uv_ref, y_ref, *scratch):
    rows_refs = scratch[:PEER_RING]
    bf_ref, sem_ref = scratch[PEER_RING:]
    nt = xn_ref.shape[0]
    wide = 2 * LANES
    nw = 2 * D_MODEL // wide

    def issue(t, slot, lo=0, hi=N_SLOTS):
        for s in range(lo, hi):
            pltpu.make_async_copy(uv_ref.at[idx_ref[t, s]],
                                  rows_refs[slot].at[s // SUB, :, s % SUB, :],
                                  sem_ref.at[slot]).start(priority=s % 2)

    def wait(slot):
        pltpu.make_async_copy(uv_ref.at[pl.ds(0, N_SLOTS)],
                              rows_refs[slot].reshape(N_SLOTS, NCH, LANES), sem_ref.at[slot]).wait()

    def stage(slot, k):
        for j in range(NCH):
            bf_ref[k, :, j * LANES:(j + 1) * LANES] = (
                rows_refs[slot][:, j, :, :].reshape(N_SLOTS, LANES).astype(BF16))

    def pair(t0, slots, prefetch):
        for k in range(2):
            wait(slots[k])
            stage(slots[k], k)
        nbatch = 2 * nw
        per = N_SLOTS // (nbatch // 2)
        batches = [(k, b * per, (b + 1) * per) for b in range(nbatch // 2) for k in range(2)]

        def next_batch():
            if prefetch and batches:
                k, lo, hi = batches.pop(0)
                issue(t0 + PEER_RING + k, slots[k], lo, hi)

        x8 = [jnp.broadcast_to(xn_ref[pl.ds(t0 + k, 1), :], (SUB, D_MODEL)).astype(BF16)
              for k in range(2)]
        act = [jnp.zeros((SUB, N_SLOTS), F32) for _ in range(2)]
        for j in range(nw // 2):
            for k in range(2):
                next_batch()
                act[k] = act[k] + lax.dot_general(
                    x8[k][:, j * wide:(j + 1) * wide], bf_ref[k, :, j * wide:(j + 1) * wide],
                    (((1,), (1,)), ((), ())), preferred_element_type=F32)
        w = [(_gelu(act[k]) * gate_ref[pl.ds(t0 + k, 1), :]).astype(BF16) for k in range(2)]
        outs = [[], []]
        for j in range(nw // 2):
            for k in range(2):
                next_batch()
                outs[k].append(jnp.dot(
                    w[k], bf_ref[k, :, D_MODEL + j * wide:D_MODEL + (j + 1) * wide],
                    preferred_element_type=F32)[0:1, :])
        for k in range(2):
            out = jnp.concatenate(outs[k], axis=1)
            y_ref[pl.ds(t0 + k, 1), :] = _rms(x1_ref[pl.ds(t0 + k, 1), :] + out, gf_ref[...])

    def group(g, prefetch):
        for p in range(PEER_RING // 2):
            pair(g * PEER_RING + 2 * p, (2 * p, 2 * p + 1), prefetch)

    for t in range(PEER_RING):
        issue(t, t)
    ngroup = nt // PEER_RING
    lax.fori_loop(0, ngroup - 1, lambda g, c: (group(g, True), c)[1], 0)
    group(ngroup - 1, False)


def _peer(idx, gates, xn, x1, gf, uv):
    t_total = xn.shape[0]
    nt = PEER_BLOCK
    row = lambda i: (i, 0)
    return pl.pallas_call(
        _peer_kernel,
        grid=(t_total // nt,),
        in_specs=[
            pl.BlockSpec((None, nt, N_SLOTS), lambda i: (i, 0, 0), memory_space=pltpu.SMEM),
            pl.BlockSpec((nt, N_SLOTS), row),
            pl.BlockSpec((nt, D_MODEL), row),
            pl.BlockSpec((nt, D_MODEL), row),
            pl.BlockSpec((1, D_MODEL), lambda i: (0, 0)),
            pl.BlockSpec(memory_space=pl.ANY),
        ],
        out_specs=pl.BlockSpec((nt, D_MODEL), row),
        out_shape=jax.ShapeDtypeStruct((t_total, D_MODEL), F32),
        scratch_shapes=[pltpu.VMEM((N_SLOTS // SUB, NCH, SUB, LANES), F32)] * PEER_RING + [
            pltpu.VMEM((2, N_SLOTS, 2 * D_MODEL), BF16),
            pltpu.SemaphoreType.DMA((PEER_RING,))],
        compiler_params=pltpu.CompilerParams(
            dimension_semantics=("arbitrary",), vmem_limit_bytes=VMEM_LIMIT),
        name="peer",
    )(idx, gates, xn, x1, gf, uv)


def _rope_tables(seq):
    pos = jnp.arange(seq, dtype=F32)
    inv = 1.0 / (ROPE_THETA ** (jnp.arange(0, HEAD_DIM, 2, dtype=F32) / HEAD_DIM))
    ang = pos[:, None] * inv[None, :]
    cos, sin = jnp.cos(ang), jnp.sin(ang)
    cosf = jnp.tile(jnp.concatenate([cos, cos], axis=1), (1, B_HEADS))
    sins = jnp.tile(jnp.concatenate([-sin, sin], axis=1), (1, B_HEADS))
    return cosf, sins


def kernel(x, norm1_g, w_in, ln_v_g, ln_v_b, w_spatial, b_spatial, out_norm_a_g, out_norm_b_g,
           w_out, norm2_g, w_query, sub_keys, expert_u, expert_v, final_norm_g):
    batch, seq, _ = x.shape
    assert w_in.shape[0] == 1 and seq % (16 * QBLK) == 0 and seq % IN_BLOCK == 0
    x2 = x.reshape(batch * seq, D_MODEL)
    row = lambda g: g.reshape(1, -1).astype(F32)

    ws = w_spatial[0].astype(BF16)
    ws_cat = jnp.concatenate([ws[0::2], ws[1::2]], axis=2)
    bs_full = jnp.repeat(b_spatial[0].T, A_GROUP_DIM, axis=1)
    cosf, sins = _rope_tables(seq)

    an, q1, k1, v1, q4, k4, v4, q16, k16, v16 = _in_proj(
        x2, row(norm1_g[0]), w_in[0].astype(BF16), row(ln_v_g[0]), row(ln_v_b[0]),
        ws_cat, bs_full, row(out_norm_a_g[0]), cosf, sins, batch, seq)

    bo = _attention(q1, k1, v1, q4, k4, v4, q16, k16, v16, batch, seq)
    bo = bo.reshape(batch * seq, D_B)

    x1, xn, idx, gates = _mid(
        x2, an, bo, row(out_norm_b_g[0]), w_out[0].astype(BF16), row(norm2_g[0]),
        w_query[0].astype(BF16), sub_keys[0].astype(BF16))

    uv = jnp.concatenate([expert_u[0].reshape(-1, NCH // 2, LANES),
                          expert_v[0].reshape(-1, NCH // 2, LANES)], axis=1)
    y = _peer(idx, gates, xn, x1, row(final_norm_g), uv)
    return y.reshape(batch, seq, D_MODEL)
```

```python
import functools
import math

import jax
import jax.numpy as jnp
from jax import lax
from jax.experimental import pallas as pl
from jax.experimental.pallas import tpu as pltpu
from jax.experimental.pallas import tpu_sc as plsc

F32 = jnp.float32
BF16 = jnp.bfloat16
I32 = jnp.int32

D_MODEL = 1024
D_A = 512
D_B = 512
A_GROUPS = 8
A_GROUP_DIM = 64
CHUNK = 128
B_HEADS = 8
HEAD_DIM = 64
DILATIONS = (1, 4, 16)
HALF_WINDOW = 64
ROPE_THETA = 10000.0
D_IN = 2 * D_A + 3 * D_B
N_KEYS = 128
PEER_HEADS = 8
PEER_TOPK = 16
D_KEY = 256
N_SLOTS = PEER_HEADS * PEER_TOPK
EPS = 1e-6
NEG_BIG = -1e30

LANES = 128
SUB = 8
NCH = 2 * D_MODEL // LANES
QBLK = 128
IN_BLOCK = 512
MID_BLOCK = 256
PEER_BLOCK = 128
PEER_RING = 4
SC_TOKEN_FRACTION = 8
SC_ROWS = 16
SC_PASSES = 4
VMEM_LIMIT = 48 * 1024 * 1024


def _gelu(x):
    c = math.sqrt(2.0 / math.pi)
    return 0.5 * x * (1.0 + jnp.tanh(c * (x + 0.044715 * (x * x * x))))


def _rms(x, g):
    return x * lax.rsqrt(jnp.mean(x * x, axis=-1, keepdims=True) + EPS) * g


def _in_proj_kernel(x_ref, g1_ref, win_ref, lng_ref, lnb_ref, ws_ref, bs_ref, ga_ref,
                    cos_ref, sin_ref,
                    an_ref, q1_ref, k1_ref, v1_ref, q4_ref, k4_ref, v4_ref,
                    q16_ref, k16_ref, v16_ref, slab_ref):
    nt = x_ref.shape[0]
    h = _rms(x_ref[...], g1_ref[...]).astype(BF16)
    proj = jnp.dot(h, win_ref[...], preferred_element_type=F32)

    u = _gelu(proj[:, :D_A])
    v = _gelu(proj[:, D_A:2 * D_A])
    mu = jnp.mean(v, axis=-1, keepdims=True)
    vc = v - mu
    var = jnp.mean(vc * vc, axis=-1, keepdims=True)
    vln = (vc * lax.rsqrt(var + EPS) * lng_ref[...] + lnb_ref[...]).astype(BF16)
    lane = lax.broadcasted_iota(I32, (CHUNK, LANES), 1)
    lo = lane < A_GROUP_DIM
    zero = jnp.zeros((CHUNK, LANES), BF16)
    chunks = []
    for c in range(nt // CHUNK):
        cols = []
        for j in range(A_GROUPS // 2):
            vv = vln[c * CHUNK:(c + 1) * CHUNK, j * LANES:(j + 1) * LANES]
            rhs = jnp.concatenate([jnp.where(lo, vv, zero), jnp.where(lo, zero, vv)], axis=0)
            cols.append(jnp.dot(ws_ref[j], rhs, preferred_element_type=F32))
        chunks.append(jnp.concatenate(cols, axis=1) + bs_ref[...])
    mixed = jnp.concatenate(chunks, axis=0)
    an_ref[...] = _rms(u * mixed, ga_ref[...]).astype(BF16)

    cosf = cos_ref[...]
    sins = sin_ref[...]
    lane_b = lax.broadcasted_iota(I32, (nt, D_B), 1)
    first_half = (lane_b % HEAD_DIM) < (HEAD_DIM // 2)

    def rope(t):
        partner = jnp.where(first_half, pltpu.roll(t, D_B - HEAD_DIM // 2, 1),
                            pltpu.roll(t, HEAD_DIM // 2, 1))
        return t * cosf + partner * sins

    q = rope(proj[:, 2 * D_A:2 * D_A + D_B]) * (HEAD_DIM ** -0.5)
    k = rope(proj[:, 2 * D_A + D_B:2 * D_A + 2 * D_B])
    vv = proj[:, 2 * D_A + 2 * D_B:]
    q1_ref[...] = q.astype(BF16)
    k1_ref[...] = k.astype(BF16)
    v1_ref[...] = vv.astype(BF16)

    nslab = D_B // LANES
    for a, t in enumerate((q, k, vv)):
        for s in range(nslab):
            slab_ref[a * nslab + s] = t[:, s * LANES:(s + 1) * LANES]
    for d, outs in ((4, (q4_ref, k4_ref, v4_ref)), (16, (q16_ref, k16_ref, v16_ref))):
        rows = nt // d
        for a, o_ref in enumerate(outs):
            for r in range(d):
                for s in range(nslab):
                    o_ref[r, :, s * LANES:(s + 1) * LANES] = (
                        slab_ref[a * nslab + s, pl.ds(r, rows, stride=d), :].astype(BF16))


def _in_proj(x2, g1, win, lng, lnb, ws_cat, bs_full, ga, cosf, sins, batch, seq):
    t_total = x2.shape[0]
    nt = IN_BLOCK
    nb = seq // nt
    grid = (t_total // nt,)
    row = lambda i: (i, 0)
    const2 = lambda i: (0, 0)
    tok_bf = jax.ShapeDtypeStruct((t_total, D_B), BF16)
    out_shape = (
        jax.ShapeDtypeStruct((t_total, D_A), BF16),
        tok_bf, tok_bf, tok_bf,
        *(jax.ShapeDtypeStruct((batch, 4, seq // 4, D_B), BF16),) * 3,
        *(jax.ShapeDtypeStruct((batch, 16, seq // 16, D_B), BF16),) * 3,
    )
    res4 = pl.BlockSpec((None, 4, nt // 4, D_B), lambda i: (i // nb, 0, i % nb, 0))
    res16 = pl.BlockSpec((None, 16, nt // 16, D_B), lambda i: (i // nb, 0, i % nb, 0))
    tok_spec = pl.BlockSpec((nt, D_B), row)
    return pl.pallas_call(
        _in_proj_kernel,
        grid=grid,
        in_specs=[
            pl.BlockSpec((nt, D_MODEL), row),
            pl.BlockSpec((1, D_MODEL), const2),
            pl.BlockSpec((D_MODEL, D_IN), const2),
            pl.BlockSpec((1, D_A), const2),
            pl.BlockSpec((1, D_A), const2),
            pl.BlockSpec((A_GROUPS // 2, CHUNK, 2 * CHUNK), lambda i: (0, 0, 0)),
            pl.BlockSpec((CHUNK, D_A), const2),
            pl.BlockSpec((1, D_A), const2),
            pl.BlockSpec((nt, D_B), lambda i: (i % nb, 0)),
            pl.BlockSpec((nt, D_B), lambda i: (i % nb, 0)),
        ],
        out_specs=(pl.BlockSpec((nt, D_A), row), tok_spec, tok_spec, tok_spec,
                   res4, res4, res4, res16, res16, res16),
        out_shape=out_shape,
        scratch_shapes=[pltpu.VMEM((3 * D_B // LANES, nt, LANES), F32)],
        compiler_params=pltpu.CompilerParams(
            dimension_semantics=("arbitrary",), vmem_limit_bytes=VMEM_LIMIT),
        name="in_proj",
    )(x2, g1, win, lng, lnb, ws_cat, bs_full, ga, cosf, sins)


def _attn_kernel(q1_ref, k1_ref, v1_ref, q4_ref, k4_ref, v4_ref, q16_ref, k16_ref, v16_ref,
                 o_ref, acc_ref, m_ref, l_ref):
    seq = o_ref.shape[0]
    lane = lax.broadcasted_iota(I32, (QBLK, LANES), 1)
    head0 = lane < HEAD_DIM
    branches = ((1, q1_ref, k1_ref, v1_ref), (4, q4_ref, k4_ref, v4_ref),
                (16, q16_ref, k16_ref, v16_ref))
    for bi, (d, q_ref, k_ref, v_ref) in enumerate(branches):
        length = seq // d
        nblk = length // QBLK
        win = min(2 * QBLK, length)
        diff = (lax.broadcasted_iota(I32, (QBLK, win), 1)
                - lax.broadcasted_iota(I32, (QBLK, win), 0))

        def block(blk, carry, d=d, bi=bi, q_ref=q_ref, k_ref=k_ref, v_ref=v_ref,
                  length=length, nblk=nblk, win=win, diff=diff):
            r = blk // nblk
            i0 = pl.multiple_of((blk % nblk) * QBLK, QBLK)
            w0 = pl.multiple_of(jnp.clip(i0 - HALF_WINDOW, 0, length - win), HALF_WINDOW)
            qb = q_ref[r, pl.ds(i0, QBLK), :]
            kw = k_ref[r, pl.ds(w0, win), :]
            vw = v_ref[r, pl.ds(w0, win), :]
            rel = diff + (w0 - i0)
            valid = (rel >= -HALF_WINDOW) & (rel <= HALF_WINDOW)
            accs, ms, ls = [], [], []
            for hsel in (head0, ~head0):
                qh = jnp.where(hsel, qb, jnp.zeros_like(qb))
                s = lax.dot_general(qh, kw, (((1,), (1,)), ((), ())),
                                    preferred_element_type=F32)
                s = jnp.where(valid, s, NEG_BIG)
                m = jnp.max(s, axis=1, keepdims=True)
                p = jnp.exp(s - m)
                ls.append(jnp.sum(p, axis=1, keepdims=True))
                ms.append(m)
                accs.append(jnp.dot(p.astype(BF16), vw, preferred_element_type=F32))
            acc = jnp.where(head0, accs[0], accs[1])
            mm = jnp.where(head0, ms[0], ms[1])
            ll = jnp.where(head0, ls[0], ls[1])
            if d == 1:
                rows = pl.ds(i0, QBLK)
            else:
                rows = pl.ds(i0 * d + r, QBLK, stride=d)
            acc_ref[bi, rows, :] = acc
            m_ref[bi, rows, :] = mm
            l_ref[bi, rows, :] = ll
            return carry

        lax.fori_loop(0, d * nblk, block, 0)

    def merge(c, carry):
        rows = pl.ds(pl.multiple_of(c * QBLK, QBLK), QBLK)
        m1, m2, m3 = m_ref[0, rows, :], m_ref[1, rows, :], m_ref[2, rows, :]
        mx = jnp.maximum(jnp.maximum(m1, m2), m3)
        w1, w2, w3 = jnp.exp(m1 - mx), jnp.exp(m2 - mx), jnp.exp(m3 - mx)
        num = w1 * acc_ref[0, rows, :] + w2 * acc_ref[1, rows, :] + w3 * acc_ref[2, rows, :]
        den = w1 * l_ref[0, rows, :] + w2 * l_ref[1, rows, :] + w3 * l_ref[2, rows, :]
        o_ref[rows, :] = num / den
        return carry

    lax.fori_loop(0, seq // QBLK, merge, 0)


def _attention(q1, k1, v1, q4, k4, v4, q16, k16, v16, batch, seq):
    npair = D_B // LANES
    nat = pl.BlockSpec((None, 1, seq, LANES), lambda b, p: (b, 0, 0, p))
    r4 = pl.BlockSpec((None, 4, seq // 4, LANES), lambda b, p: (b, 0, 0, p))
    r16 = pl.BlockSpec((None, 16, seq // 16, LANES), lambda b, p: (b, 0, 0, p))
    q1, k1, v1 = (t.reshape(batch, 1, seq, D_B) for t in (q1, k1, v1))
    return pl.pallas_call(
        _attn_kernel,
        grid=(batch, npair),
        in_specs=[nat, nat, nat, r4, r4, r4, r16, r16, r16],
        out_specs=pl.BlockSpec((None, seq, LANES), lambda b, p: (b, 0, p)),
        out_shape=jax.ShapeDtypeStruct((batch, seq, D_B), F32),
        scratch_shapes=[pltpu.VMEM((3, seq, LANES), F32)] * 3,
        compiler_params=pltpu.CompilerParams(
            dimension_semantics=("arbitrary", "arbitrary"), vmem_limit_bytes=VMEM_LIMIT),
        name="dilated_attn",
    )(q1, k1, v1, q4, k4, v4, q16, k16, v16)


def _topk_rows(s, k):
    n = s.shape[0]
    iota = lax.broadcasted_iota(I32, s.shape, 0)
    vals, idxs = [], []
    for _ in range(k):
        m = jnp.max(s, axis=0, keepdims=True)
        i = jnp.min(jnp.where(s == m, iota, n), axis=0, keepdims=True)
        vals.append(m)
        idxs.append(i)
        s = jnp.where(iota == i, -jnp.inf, s)
    return jnp.concatenate(vals, axis=0), jnp.concatenate(idxs, axis=0)


def _take_rows(table, sel):
    out = jnp.zeros(sel.shape, table.dtype)
    for a in range(table.shape[0]):
        out = jnp.where(sel == a, table[a:a + 1, :], out)
    return out


def _mid_kernel(x_ref, an_ref, bo_ref, gb_ref, wout_ref, g2_ref, wq_ref, keys_ref,
                x1_ref, xn_ref, idx_ref, gate_ref):
    nt = x_ref.shape[0]
    bn = _rms(bo_ref[...], gb_ref[...]).astype(BF16)
    x1 = (x_ref[...]
          + jnp.dot(an_ref[...], wout_ref[:D_A, :], preferred_element_type=F32)
          + jnp.dot(bn, wout_ref[D_A:, :], preferred_element_type=F32))
    x1_ref[...] = x1
    xn = _rms(x1, g2_ref[...])
    xn_ref[...] = xn
    q = jnp.dot(xn.astype(BF16), wq_ref[...], preferred_element_type=F32).astype(BF16)
    keys = (keys_ref[0], keys_ref[1])
    half = D_KEY // 2
    for c in range(nt // LANES):
        qc = q[c * LANES:(c + 1) * LANES, :]
        experts, gates = [], []
        for h in range(PEER_HEADS):
            tops = []
            for p in range(2):
                qhp = qc[:, (2 * h + p) * half:(2 * h + p + 1) * half]
                s = lax.dot_general(keys[p], qhp, (((1,), (1,)), ((), ())),
                                    preferred_element_type=F32)
                tops.append(_topk_rows(s, PEER_TOPK))
            (s1, i1), (s2, i2) = tops
            cand = jnp.concatenate([s1[a:a + 1, :] + s2 for a in range(PEER_TOPK)], axis=0)
            sc, ci = _topk_rows(cand, PEER_TOPK)
            e = (_take_rows(i1, ci >> 4) * N_KEYS + _take_rows(i2, ci & (PEER_TOPK - 1)))
            ex = jnp.exp(sc - sc[0:1, :])
            gates.append(ex / jnp.sum(ex, axis=0, keepdims=True))
            experts.append(e)
        idx_ref[c] = jnp.concatenate(experts, axis=0).T
        gate_ref[c * LANES:(c + 1) * LANES, :] = jnp.concatenate(gates, axis=0).T


def _mid(x2, an, bo, gb, wout, g2, wq, keys):
    t_total = x2.shape[0]
    nt = MID_BLOCK
    row = lambda i: (i, 0)
    const2 = lambda i: (0, 0)
    return pl.pallas_call(
        _mid_kernel,
        grid=(t_total // nt,),
        in_specs=[
            pl.BlockSpec((nt, D_MODEL), row),
            pl.BlockSpec((nt, D_A), row),
            pl.BlockSpec((nt, D_B), row),
            pl.BlockSpec((1, D_B), const2),
            pl.BlockSpec((D_MODEL, D_MODEL), const2),
            pl.BlockSpec((1, D_MODEL), const2),
            pl.BlockSpec((D_MODEL, PEER_HEADS * D_KEY), const2),
            pl.BlockSpec((2, N_KEYS, D_KEY // 2), lambda i: (0, 0, 0)),
        ],
        out_specs=(
            pl.BlockSpec((nt, D_MODEL), row),
            pl.BlockSpec((nt, D_MODEL), row),
            pl.BlockSpec((nt // LANES, LANES, N_SLOTS), lambda i: (i, 0, 0)),
            pl.BlockSpec((nt, N_SLOTS), row),
        ),
        out_shape=(
            jax.ShapeDtypeStruct((t_total, D_MODEL), F32),
            jax.ShapeDtypeStruct((t_total, D_MODEL), F32),
            jax.ShapeDtypeStruct((t_total // LANES, LANES, N_SLOTS), I32),
            jax.ShapeDtypeStruct((t_total, N_SLOTS), F32),
        ),
        compiler_params=pltpu.CompilerParams(
            dimension_semantics=("arbitrary",), vmem_limit_bytes=VMEM_LIMIT),
        name="mid",
    )(x2, an, bo, gb, wout, g2, wq, keys)


def _peer_kernel(idx_ref, gate_ref, xn_ref, x1_ref, gf_ref, uv_ref, y_ref, *scratch):
    rows_refs = scratch[:PEER_RING]
    bf_ref, sem_ref = scratch[PEER_RING:]
    nt = xn_ref.shape[0]
    wide = 2 * LANES
    nw = 2 * D_MODEL // wide

    def issue(t, slot, lo=0, hi=N_SLOTS):
        for s in range(lo, hi):
            pltpu.make_async_copy(uv_ref.at[idx_ref[t, s]],
                                  rows_refs[slot].at[s // SUB, :, s % SUB, :],
                                  sem_ref.at[slot]).start(priority=s % 2)

    def wait(slot):
        pltpu.make_async_copy(uv_ref.at[pl.ds(0, N_SLOTS)],
                              rows_refs[slot].reshape(N_SLOTS, NCH, LANES), sem_ref.at[slot]).wait()

    def stage(slot, k):
        for j in range(NCH):
            bf_ref[k, :, j * LANES:(j + 1) * LANES] = (
                rows_refs[slot][:, j, :, :].reshape(N_SLOTS, LANES).astype(BF16))

    def pair(t0, slots, prefetch):
        for k in range(2):
            wait(slots[k])
            stage(slots[k], k)
        nbatch = 2 * nw
        per = N_SLOTS // (nbatch // 2)
        batches = [(k, b * per, (b + 1) * per) for b in range(nbatch // 2) for k in range(2)]

        def next_batch():
            if prefetch and batches:
                k, lo, hi = batches.pop(0)
                issue(t0 + PEER_RING + k, slots[k], lo, hi)

        x8 = [jnp.broadcast_to(xn_ref[pl.ds(t0 + k, 1), :], (SUB, D_MODEL)).astype(BF16)
              for k in range(2)]
        act = [jnp.zeros((SUB, N_SLOTS), F32) for _ in range(2)]
        for j in range(nw // 2):
            for k in range(2):
                next_batch()
                act[k] = act[k] + lax.dot_general(
                    x8[k][:, j * wide:(j + 1) * wide], bf_ref[k, :, j * wide:(j + 1) * wide],
                    (((1,), (1,)), ((), ())), preferred_element_type=F32)
        w = [(_gelu(act[k]) * gate_ref[pl.ds(t0 + k, 1), :]).astype(BF16) for k in range(2)]
        outs = [[], []]
        for j in range(nw // 2):
            for k in range(2):
                next_batch()
                outs[k].append(jnp.dot(
                    w[k], bf_ref[k, :, D_MODEL + j * wide:D_MODEL + (j + 1) * wide],
                    preferred_element_type=F32)[0:1, :])
        for k in range(2):
            out = jnp.concatenate(outs[k], axis=1)
            y_ref[pl.ds(t0 + k, 1), :] = _rms(x1_ref[pl.ds(t0 + k, 1), :] + out, gf_ref[...])

    def group(g, prefetch):
        for p in range(PEER_RING // 2):
            pair(g * PEER_RING + 2 * p, (2 * p, 2 * p + 1), prefetch)

    for t in range(PEER_RING):
        issue(t, t)
    ngroup = nt // PEER_RING
    lax.fori_loop(0, ngroup - 1, lambda g, c: (group(g, True), c)[1], 0)
    group(ngroup - 1, False)


def _peer(idx, gates, xn, x1, gf, uv, first_token):
    t_total = xn.shape[0] - first_token
    nt = PEER_BLOCK
    b0 = first_token // nt
    row = lambda i: (i + b0, 0)
    return pl.pallas_call(
        _peer_kernel,
        grid=(t_total // nt,),
        in_specs=[
            pl.BlockSpec((None, nt, N_SLOTS), lambda i: (i + b0, 0, 0), memory_space=pltpu.SMEM),
            pl.BlockSpec((nt, N_SLOTS), row),
            pl.BlockSpec((nt, D_MODEL), row),
            pl.BlockSpec((nt, D_MODEL), row),
            pl.BlockSpec((1, D_MODEL), lambda i: (0, 0)),
            pl.BlockSpec(memory_space=pl.ANY),
        ],
        out_specs=pl.BlockSpec((nt, D_MODEL), lambda i: (i, 0)),
        out_shape=jax.ShapeDtypeStruct((t_total, D_MODEL), F32),
        scratch_shapes=[pltpu.VMEM((N_SLOTS // SUB, NCH, SUB, LANES), F32)] * PEER_RING + [
            pltpu.VMEM((2, N_SLOTS, 2 * D_MODEL), BF16),
            pltpu.SemaphoreType.DMA((PEER_RING,))],
        compiler_params=pltpu.CompilerParams(
            dimension_semantics=("arbitrary",), vmem_limit_bytes=VMEM_LIMIT),
        name="peer",
    )(idx, gates, xn, x1, gf, uv)


def _sc_peer(idx, xn, gates, uv2, n_tokens):
    info = plsc.get_sparse_core_info()
    nc, lanes_n = info.num_cores, info.num_lanes
    nw = nc * info.num_subcores
    per = n_tokens // nw
    assert n_tokens % nw == 0
    nchunk = N_SLOTS // SC_ROWS
    qv = D_MODEL // (SC_PASSES * lanes_n)
    c0 = math.sqrt(2.0 / math.pi)
    mesh = plsc.VectorSubcoreMesh(core_axis_name="c", subcore_axis_name="s")

    @functools.partial(
        pl.kernel, mesh=mesh,
        out_type=jax.ShapeDtypeStruct((n_tokens, D_MODEL), F32),
        scratch_types=[
            pltpu.VMEM((nchunk, SC_ROWS), I32),
            pltpu.VMEM((D_MODEL,), F32),
            pltpu.VMEM((N_SLOTS,), F32),
            pltpu.VMEM((SC_ROWS, 2 * D_MODEL), F32),
            pltpu.VMEM((SC_ROWS, 2 * D_MODEL), F32),
            pltpu.VMEM((D_MODEL,), F32),
            pltpu.VMEM((SC_ROWS, lanes_n), F32),
            pltpu.VMEM((SC_ROWS,), F32),
            pltpu.SemaphoreType.DMA,
            pltpu.SemaphoreType.DMA,
        ],
        compiler_params=pltpu.CompilerParams(needs_layout_passes=False),
        name="peer_sc",
    )
    def sc_kernel(idx_hbm, xn_hbm, gate_hbm, uv_hbm, out_hbm, idx_v, x_v, g_v, rows0, rows1,
                  out_v, part_v, w_v, sem0, sem1):
        wid = lax.axis_index("s") * nc + lax.axis_index("c")
        lane_ids = lax.iota(I32, lanes_n)
        zero = jnp.zeros((lanes_n,), F32)
        bufs = ((rows0, sem0), (rows1, sem1))

        def vec(q, j):
            return pl.ds((q * qv + j) * lanes_n, lanes_n)

        def gather(c, b):
            rows, sem = bufs[b]
            return pltpu.make_async_copy(uv_hbm.at[idx_v.at[c]], rows, sem)

        def chunk(c, b):
            rows, _ = bufs[b]
            for q in range(SC_PASSES):
                xq = [x_v[vec(q, j)] for j in range(qv)]

                def urow(r, carry, q=q, xq=xq):
                    a = [rows[r, vec(q, j)] * xq[j] for j in range(4)]
                    for j in range(4, qv):
                        a[j % 4] = a[j % 4] + rows[r, vec(q, j)] * xq[j]
                    tot = (a[0] + a[1]) + (a[2] + a[3])
                    part_v[r, :] = tot if q == 0 else part_v[r, :] + tot
                    return carry

                lax.fori_loop(0, SC_ROWS, urow, 0)
            act = zero
            for l in range(lanes_n):
                act = act + plsc.load_gather(part_v, [lane_ids, jnp.full((lanes_n,), l, I32)])
            z = c0 * (act + 0.044715 * (act * act * act))
            tanh_z = 1.0 - 2.0 / (jnp.exp(2.0 * z) + 1.0)
            w_v[...] = 0.5 * act * (1.0 + tanh_z) * g_v[pl.ds(c * SC_ROWS, SC_ROWS)]
            for q in range(SC_PASSES):
                o = tuple(out_v[vec(q, j)] for j in range(qv))

                def vrow(r, o, q=q):
                    wr = plsc.load_gather(w_v, [jnp.full((lanes_n,), r, I32)])
                    return tuple(o[j] + wr * rows[r, pl.ds(D_MODEL + (q * qv + j) * lanes_n, lanes_n)]
                                 for j in range(qv))

                o = lax.fori_loop(0, SC_ROWS, vrow, o)
                for j in range(qv):
                    out_v[vec(q, j)] = o[j]

        @pl.loop(0, per)
        def _(i):
            t = wid * per + i
            pltpu.sync_copy(idx_hbm.at[t], idx_v)
            pltpu.sync_copy(xn_hbm.at[t], x_v)
            pltpu.sync_copy(gate_hbm.at[t], g_v)
            for j in range(D_MODEL // lanes_n):
                out_v[pl.ds(j * lanes_n, lanes_n)] = zero
            gather(0, 0).start()

            @pl.loop(0, nchunk, step=2)
            def _(c):
                gather(c + 1, 1).start()
                gather(c, 0).wait()
                chunk(c, 0)

                @pl.when(c + 2 < nchunk)
                def _():
                    gather(c + 2, 0).start()

                gather(c + 1, 1).wait()
                chunk(c + 1, 1)

            pltpu.sync_copy(out_v, out_hbm.at[t])

    return sc_kernel(idx.reshape(-1, nchunk, SC_ROWS), xn, gates, uv2)


def _residual_norm_kernel(x1_ref, o_ref, gf_ref, y_ref):
    y_ref[...] = _rms(x1_ref[...] + o_ref[...], gf_ref[...])


def _residual_norm(x1, out, gf):
    n = out.shape[0]
    nt = MID_BLOCK
    row = lambda i: (i, 0)
    return pl.pallas_call(
        _residual_norm_kernel,
        grid=(n // nt,),
        in_specs=[pl.BlockSpec((nt, D_MODEL), row), pl.BlockSpec((nt, D_MODEL), row),
                  pl.BlockSpec((1, D_MODEL), lambda i: (0, 0))],
        out_specs=pl.BlockSpec((nt, D_MODEL), row),
        out_shape=jax.ShapeDtypeStruct((n, D_MODEL), F32),
        compiler_params=pltpu.CompilerParams(dimension_semantics=("arbitrary",)),
        name="residual_norm",
    )(x1, out, gf)


def _rope_tables(seq):
    pos = jnp.arange(seq, dtype=F32)
    inv = 1.0 / (ROPE_THETA ** (jnp.arange(0, HEAD_DIM, 2, dtype=F32) / HEAD_DIM))
    ang = pos[:, None] * inv[None, :]
    cos, sin = jnp.cos(ang), jnp.sin(ang)
    cosf = jnp.tile(jnp.concatenate([cos, cos], axis=1), (1, B_HEADS))
    sins = jnp.tile(jnp.concatenate([-sin, sin], axis=1), (1, B_HEADS))
    return cosf, sins


def kernel(x, norm1_g, w_in, ln_v_g, ln_v_b, w_spatial, b_spatial, out_norm_a_g, out_norm_b_g,
           w_out, norm2_g, w_query, sub_keys, expert_u, expert_v, final_norm_g):
    batch, seq, _ = x.shape
    assert w_in.shape[0] == 1 and seq % (16 * QBLK) == 0 and seq % IN_BLOCK == 0
    x2 = x.reshape(batch * seq, D_MODEL)
    row = lambda g: g.reshape(1, -1).astype(F32)

    ws = w_spatial[0].astype(BF16)
    ws_cat = jnp.concatenate([ws[0::2], ws[1::2]], axis=2)
    bs_full = jnp.repeat(b_spatial[0].T, A_GROUP_DIM, axis=1)
    cosf, sins = _rope_tables(seq)

    an, q1, k1, v1, q4, k4, v4, q16, k16, v16 = _in_proj(
        x2, row(norm1_g[0]), w_in[0].astype(BF16), row(ln_v_g[0]), row(ln_v_b[0]),
        ws_cat, bs_full, row(out_norm_a_g[0]), cosf, sins, batch, seq)

    bo = _attention(q1, k1, v1, q4, k4, v4, q16, k16, v16, batch, seq)
    bo = bo.reshape(batch * seq, D_B)

    x1, xn, idx, gates = _mid(
        x2, an, bo, row(out_norm_b_g[0]), w_out[0].astype(BF16), row(norm2_g[0]),
        w_query[0].astype(BF16), sub_keys[0].astype(BF16))

    uv = jnp.concatenate([expert_u[0].reshape(-1, NCH // 2, LANES),
                          expert_v[0].reshape(-1, NCH // 2, LANES)], axis=1)
    n_sc = (batch * seq) // SC_TOKEN_FRACTION
    gf = row(final_norm_g)
    uv2 = jnp.concatenate([expert_u[0], expert_v[0]], axis=1)
    out_sc = _sc_peer(idx.reshape(batch * seq, N_SLOTS), xn, gates, uv2, n_sc)
    y_tc = _peer(idx, gates, xn, x1, gf, uv, n_sc)
    y_sc = _residual_norm(x1, out_sc, gf)
    return jnp.concatenate([y_sc, y_tc], axis=0).reshape(batch, seq, D_MODEL)
```

```python
import functools
import math

import jax
import jax.numpy as jnp
from jax import lax
from jax.experimental import pallas as pl
from jax.experimental.pallas import tpu as pltpu
from jax.experimental.pallas import tpu_sc as plsc

F32 = jnp.float32
BF16 = jnp.bfloat16
I32 = jnp.int32

D_MODEL = 1024
D_A = 512
D_B = 512
A_GROUPS = 8
A_GROUP_DIM = 64
CHUNK = 128
B_HEADS = 8
HEAD_DIM = 64
DILATIONS = (1, 4, 16)
HALF_WINDOW = 64
ROPE_THETA = 10000.0
D_IN = 2 * D_A + 3 * D_B
N_KEYS = 128
PEER_HEADS = 8
PEER_TOPK = 16
D_KEY = 256
N_SLOTS = PEER_HEADS * PEER_TOPK
EPS = 1e-6
NEG_BIG = -1e30

LANES = 128
SUB = 8
NCH = 2 * D_MODEL // LANES
QBLK = 128
IN_BLOCK = 512
MID_BLOCK = 256
PEER_BLOCK = 128
PEER_RING = 4
SC_SHARE = (7, 16)
SC_ROWS = 16
SC_PASSES = 4
VMEM_LIMIT = 48 * 1024 * 1024


def _gelu(x):
    c = math.sqrt(2.0 / math.pi)
    return 0.5 * x * (1.0 + jnp.tanh(c * (x + 0.044715 * (x * x * x))))


def _rms(x, g):
    return x * lax.rsqrt(jnp.mean(x * x, axis=-1, keepdims=True) + EPS) * g


def _in_proj_kernel(x_ref, g1_ref, win_ref, lng_ref, lnb_ref, ws_ref, bs_ref, ga_ref,
                    cos_ref, sin_ref,
                    an_ref, q1_ref, k1_ref, v1_ref, q4_ref, k4_ref, v4_ref,
                    q16_ref, k16_ref, v16_ref, slab_ref):
    nt = x_ref.shape[0]
    h = _rms(x_ref[...], g1_ref[...]).astype(BF16)
    proj = jnp.dot(h, win_ref[...], preferred_element_type=F32)

    u = _gelu(proj[:, :D_A])
    v = _gelu(proj[:, D_A:2 * D_A])
    mu = jnp.mean(v, axis=-1, keepdims=True)
    vc = v - mu
    var = jnp.mean(vc * vc, axis=-1, keepdims=True)
    vln = (vc * lax.rsqrt(var + EPS) * lng_ref[...] + lnb_ref[...]).astype(BF16)
    lane = lax.broadcasted_iota(I32, (CHUNK, LANES), 1)
    lo = lane < A_GROUP_DIM
    zero = jnp.zeros((CHUNK, LANES), BF16)
    chunks = []
    for c in range(nt // CHUNK):
        cols = []
        for j in range(A_GROUPS // 2):
            vv = vln[c * CHUNK:(c + 1) * CHUNK, j * LANES:(j + 1) * LANES]
            rhs = jnp.concatenate([jnp.where(lo, vv, zero), jnp.where(lo, zero, vv)], axis=0)
            cols.append(jnp.dot(ws_ref[j], rhs, preferred_element_type=F32))
        chunks.append(jnp.concatenate(cols, axis=1) + bs_ref[...])
    mixed = jnp.concatenate(chunks, axis=0)
    an_ref[...] = _rms(u * mixed, ga_ref[...]).astype(BF16)

    cosf = cos_ref[...]
    sins = sin_ref[...]
    lane_b = lax.broadcasted_iota(I32, (nt, D_B), 1)
    first_half = (lane_b % HEAD_DIM) < (HEAD_DIM // 2)

    def rope(t):
        partner = jnp.where(first_half, pltpu.roll(t, D_B - HEAD_DIM // 2, 1),
                            pltpu.roll(t, HEAD_DIM // 2, 1))
        return t * cosf + partner * sins

    q = rope(proj[:, 2 * D_A:2 * D_A + D_B]) * (HEAD_DIM ** -0.5)
    k = rope(proj[:, 2 * D_A + D_B:2 * D_A + 2 * D_B])
    vv = proj[:, 2 * D_A + 2 * D_B:]
    q1_ref[...] = q.astype(BF16)
    k1_ref[...] = k.astype(BF16)
    v1_ref[...] = vv.astype(BF16)

    nslab = D_B // LANES
    for a, t in enumerate((q, k, vv)):
        for s in range(nslab):
            slab_ref[a * nslab + s] = t[:, s * LANES:(s + 1) * LANES]
    for d, outs in ((4, (q4_ref, k4_ref, v4_ref)), (16, (q16_ref, k16_ref, v16_ref))):
        rows = nt // d
        for a, o_ref in enumerate(outs):
            for r in range(d):
                for s in range(nslab):
                    o_ref[r, :, s * LANES:(s + 1) * LANES] = (
                        slab_ref[a * nslab + s, pl.ds(r, rows, stride=d), :].astype(BF16))


def _in_proj(x2, g1, win, lng, lnb, ws_cat, bs_full, ga, cosf, sins, batch, seq):
    t_total = x2.shape[0]
    nt = IN_BLOCK
    nb = seq // nt
    grid = (t_total // nt,)
    row = lambda i: (i, 0)
    const2 = lambda i: (0, 0)
    tok_bf = jax.ShapeDtypeStruct((t_total, D_B), BF16)
    out_shape = (
        jax.ShapeDtypeStruct((t_total, D_A), BF16),
        tok_bf, tok_bf, tok_bf,
        *(jax.ShapeDtypeStruct((batch, 4, seq // 4, D_B), BF16),) * 3,
        *(jax.ShapeDtypeStruct((batch, 16, seq // 16, D_B), BF16),) * 3,
    )
    res4 = pl.BlockSpec((None, 4, nt // 4, D_B), lambda i: (i // nb, 0, i % nb, 0))
    res16 = pl.BlockSpec((None, 16, nt // 16, D_B), lambda i: (i // nb, 0, i % nb, 0))
    tok_spec = pl.BlockSpec((nt, D_B), row)
    return pl.pallas_call(
        _in_proj_kernel,
        grid=grid,
        in_specs=[
            pl.BlockSpec((nt, D_MODEL), row),
            pl.BlockSpec((1, D_MODEL), const2),
            pl.BlockSpec((D_MODEL, D_IN), const2),
            pl.BlockSpec((1, D_A), const2),
            pl.BlockSpec((1, D_A), const2),
            pl.BlockSpec((A_GROUPS // 2, CHUNK, 2 * CHUNK), lambda i: (0, 0, 0)),
            pl.BlockSpec((CHUNK, D_A), const2),
            pl.BlockSpec((1, D_A), const2),
            pl.BlockSpec((nt, D_B), lambda i: (i % nb, 0)),
            pl.BlockSpec((nt, D_B), lambda i: (i % nb, 0)),
        ],
        out_specs=(pl.BlockSpec((nt, D_A), row), tok_spec, tok_spec, tok_spec,
                   res4, res4, res4, res16, res16, res16),
        out_shape=out_shape,
        scratch_shapes=[pltpu.VMEM((3 * D_B // LANES, nt, LANES), F32)],
        compiler_params=pltpu.CompilerParams(
            dimension_semantics=("arbitrary",), vmem_limit_bytes=VMEM_LIMIT),
        name="in_proj",
    )(x2, g1, win, lng, lnb, ws_cat, bs_full, ga, cosf, sins)


def _attn_kernel(q1_ref, k1_ref, v1_ref, q4_ref, k4_ref, v4_ref, q16_ref, k16_ref, v16_ref,
                 o_ref, acc_ref, m_ref, l_ref):
    seq = o_ref.shape[0]
    lane = lax.broadcasted_iota(I32, (QBLK, LANES), 1)
    head0 = lane < HEAD_DIM
    branches = ((1, q1_ref, k1_ref, v1_ref), (4, q4_ref, k4_ref, v4_ref),
                (16, q16_ref, k16_ref, v16_ref))
    for bi, (d, q_ref, k_ref, v_ref) in enumerate(branches):
        length = seq // d
        nblk = length // QBLK
        win = min(2 * QBLK, length)
        diff = (lax.broadcasted_iota(I32, (QBLK, win), 1)
                - lax.broadcasted_iota(I32, (QBLK, win), 0))

        def block(blk, carry, d=d, bi=bi, q_ref=q_ref, k_ref=k_ref, v_ref=v_ref,
                  length=length, nblk=nblk, win=win, diff=diff):
            r = blk // nblk
            i0 = pl.multiple_of((blk % nblk) * QBLK, QBLK)
            w0 = pl.multiple_of(jnp.clip(i0 - HALF_WINDOW, 0, length - win), HALF_WINDOW)
            qb = q_ref[r, pl.ds(i0, QBLK), :]
            kw = k_ref[r, pl.ds(w0, win), :]
            vw = v_ref[r, pl.ds(w0, win), :]
            rel = diff + (w0 - i0)
            valid = (rel >= -HALF_WINDOW) & (rel <= HALF_WINDOW)
            accs, ms, ls = [], [], []
            for hsel in (head0, ~head0):
                qh = jnp.where(hsel, qb, jnp.zeros_like(qb))
                s = lax.dot_general(qh, kw, (((1,), (1,)), ((), ())),
                                    preferred_element_type=F32)
                s = jnp.where(valid, s, NEG_BIG)
                m = jnp.max(s, axis=1, keepdims=True)
                p = jnp.exp(s - m)
                ls.append(jnp.sum(p, axis=1, keepdims=True))
                ms.append(m)
                accs.append(jnp.dot(p.astype(BF16), vw, preferred_element_type=F32))
            acc = jnp.where(head0, accs[0], accs[1])
            mm = jnp.where(head0, ms[0], ms[1])
            ll = jnp.where(head0, ls[0], ls[1])
            if d == 1:
                rows = pl.ds(i0, QBLK)
            else:
                rows = pl.ds(i0 * d + r, QBLK, stride=d)
            acc_ref[bi, rows, :] = acc
            m_ref[bi, rows, :] = mm
            l_ref[bi, rows, :] = ll
            return carry

        lax.fori_loop(0, d * nblk, block, 0)

    def merge(c, carry):
        rows = pl.ds(pl.multiple_of(c * QBLK, QBLK), QBLK)
        m1, m2, m3 = m_ref[0, rows, :], m_ref[1, rows, :], m_ref[2, rows, :]
        mx = jnp.maximum(jnp.maximum(m1, m2), m3)
        w1, w2, w3 = jnp.exp(m1 - mx), jnp.exp(m2 - mx), jnp.exp(m3 - mx)
        num = w1 * acc_ref[0, rows, :] + w2 * acc_ref[1, rows, :] + w3 * acc_ref[2, rows, :]
        den = w1 * l_ref[0, rows, :] + w2 * l_ref[1, rows, :] + w3 * l_ref[2, rows, :]
        o_ref[rows, :] = num / den
        return carry

    lax.fori_loop(0, seq // QBLK, merge, 0)


def _attention(q1, k1, v1, q4, k4, v4, q16, k16, v16, batch, seq):
    npair = D_B // LANES
    nat = pl.BlockSpec((None, 1, seq, LANES), lambda b, p: (b, 0, 0, p))
    r4 = pl.BlockSpec((None, 4, seq // 4, LANES), lambda b, p: (b, 0, 0, p))
    r16 = pl.BlockSpec((None, 16, seq // 16, LANES), lambda b, p: (b, 0, 0, p))
    q1, k1, v1 = (t.reshape(batch, 1, seq, D_B) for t in (q1, k1, v1))
    return pl.pallas_call(
        _attn_kernel,
        grid=(batch, npair),
        in_specs=[nat, nat, nat, r4, r4, r4, r16, r16, r16],
        out_specs=pl.BlockSpec((None, seq, LANES), lambda b, p: (b, 0, p)),
        out_shape=jax.ShapeDtypeStruct((batch, seq, D_B), F32),
        scratch_shapes=[pltpu.VMEM((3, seq, LANES), F32)] * 3,
        compiler_params=pltpu.CompilerParams(
            dimension_semantics=("arbitrary", "arbitrary"), vmem_limit_bytes=VMEM_LIMIT),
        name="dilated_attn",
    )(q1, k1, v1, q4, k4, v4, q16, k16, v16)


def _topk_rows(s, k):
    n = s.shape[0]
    iota = lax.broadcasted_iota(I32, s.shape, 0)
    vals, idxs = [], []
    for _ in range(k):
        m = jnp.max(s, axis=0, keepdims=True)
        i = jnp.min(jnp.where(s == m, iota, n), axis=0, keepdims=True)
        vals.append(m)
        idxs.append(i)
        s = jnp.where(iota == i, -jnp.inf, s)
    return jnp.concatenate(vals, axis=0), jnp.concatenate(idxs, axis=0)


def _take_rows(table, sel):
    out = jnp.zeros(sel.shape, table.dtype)
    for a in range(table.shape[0]):
        out = jnp.where(sel == a, table[a:a + 1, :], out)
    return out


def _mid_kernel(x_ref, an_ref, bo_ref, gb_ref, wout_ref, g2_ref, wq_ref, keys_ref,
                x1_ref, xn_ref, idx_ref, gate_ref):
    nt = x_ref.shape[0]
    bn = _rms(bo_ref[...], gb_ref[...]).astype(BF16)
    x1 = (x_ref[...]
          + jnp.dot(an_ref[...], wout_ref[:D_A, :], preferred_element_type=F32)
          + jnp.dot(bn, wout_ref[D_A:, :], preferred_element_type=F32))
    x1_ref[...] = x1
    xn = _rms(x1, g2_ref[...])
    xn_ref[...] = xn
    q = jnp.dot(xn.astype(BF16), wq_ref[...], preferred_element_type=F32).astype(BF16)
    keys = (keys_ref[0], keys_ref[1])
    half = D_KEY // 2
    for c in range(nt // LANES):
        qc = q[c * LANES:(c + 1) * LANES, :]
        experts, gates = [], []
        for h in range(PEER_HEADS):
            tops = []
            for p in range(2):
                qhp = qc[:, (2 * h + p) * half:(2 * h + p + 1) * half]
                s = lax.dot_general(keys[p], qhp, (((1,), (1,)), ((), ())),
                                    preferred_element_type=F32)
                tops.append(_topk_rows(s, PEER_TOPK))
            (s1, i1), (s2, i2) = tops
            cand = jnp.concatenate([s1[a:a + 1, :] + s2 for a in range(PEER_TOPK)], axis=0)
            sc, ci = _topk_rows(cand, PEER_TOPK)
            e = (_take_rows(i1, ci >> 4) * N_KEYS + _take_rows(i2, ci & (PEER_TOPK - 1)))
            ex = jnp.exp(sc - sc[0:1, :])
            gates.append(ex / jnp.sum(ex, axis=0, keepdims=True))
            experts.append(e)
        idx_ref[c] = jnp.concatenate(experts, axis=0).T
        gate_ref[c * LANES:(c + 1) * LANES, :] = jnp.concatenate(gates, axis=0).T


def _mid(x2, an, bo, gb, wout, g2, wq, keys):
    t_total = x2.shape[0]
    nt = MID_BLOCK
    row = lambda i: (i, 0)
    const2 = lambda i: (0, 0)
    return pl.pallas_call(
        _mid_kernel,
        grid=(t_total // nt,),
        in_specs=[
            pl.BlockSpec((nt, D_MODEL), row),
            pl.BlockSpec((nt, D_A), row),
            pl.BlockSpec((nt, D_B), row),
            pl.BlockSpec((1, D_B), const2),
            pl.BlockSpec((D_MODEL, D_MODEL), const2),
            pl.BlockSpec((1, D_MODEL), const2),
            pl.BlockSpec((D_MODEL, PEER_HEADS * D_KEY), const2),
            pl.BlockSpec((2, N_KEYS, D_KEY // 2), lambda i: (0, 0, 0)),
        ],
        out_specs=(
            pl.BlockSpec((nt, D_MODEL), row),
            pl.BlockSpec((nt, D_MODEL), row),
            pl.BlockSpec((nt // LANES, LANES, N_SLOTS), lambda i: (i, 0, 0)),
            pl.BlockSpec((nt, N_SLOTS), row),
        ),
        out_shape=(
            jax.ShapeDtypeStruct((t_total, D_MODEL), F32),
            jax.ShapeDtypeStruct((t_total, D_MODEL), F32),
            jax.ShapeDtypeStruct((t_total // LANES, LANES, N_SLOTS), I32),
            jax.ShapeDtypeStruct((t_total, N_SLOTS), F32),
        ),
        compiler_params=pltpu.CompilerParams(
            dimension_semantics=("arbitrary",), vmem_limit_bytes=VMEM_LIMIT),
        name="mid",
    )(x2, an, bo, gb, wout, g2, wq, keys)


def _peer_kernel(idx_ref, gate_ref, xn_ref, x1_ref, gf_ref, uv_ref, y_ref, *scratch):
    rows_refs = scratch[:PEER_RING]
    bf_ref, sem_ref = scratch[PEER_RING:]
    nt = xn_ref.shape[0]
    wide = 2 * LANES
    nw = 2 * D_MODEL // wide

    def issue(t, slot, lo=0, hi=N_SLOTS):
        for s in range(lo, hi):
            pltpu.make_async_copy(uv_ref.at[idx_ref[t, s]],
                                  rows_refs[slot].at[s // SUB, :, s % SUB, :],
                                  sem_ref.at[slot]).start(priority=s % 2)

    def wait(slot):
        pltpu.make_async_copy(uv_ref.at[pl.ds(0, N_SLOTS)],
                              rows_refs[slot].reshape(N_SLOTS, NCH, LANES), sem_ref.at[slot]).wait()

    def stage(slot, k):
        for j in range(NCH):
            bf_ref[k, :, j * LANES:(j + 1) * LANES] = (
                rows_refs[slot][:, j, :, :].reshape(N_SLOTS, LANES).astype(BF16))

    def pair(t0, slots, prefetch):
        for k in range(2):
            wait(slots[k])
            stage(slots[k], k)
        nbatch = 2 * nw
        per = N_SLOTS // (nbatch // 2)
        batches = [(k, b * per, (b + 1) * per) for b in range(nbatch // 2) for k in range(2)]

        def next_batch():
            if prefetch and batches:
                k, lo, hi = batches.pop(0)
                issue(t0 + PEER_RING + k, slots[k], lo, hi)

        x8 = [jnp.broadcast_to(xn_ref[pl.ds(t0 + k, 1), :], (SUB, D_MODEL)).astype(BF16)
              for k in range(2)]
        act = [jnp.zeros((SUB, N_SLOTS), F32) for _ in range(2)]
        for j in range(nw // 2):
            for k in range(2):
                next_batch()
                act[k] = act[k] + lax.dot_general(
                    x8[k][:, j * wide:(j + 1) * wide], bf_ref[k, :, j * wide:(j + 1) * wide],
                    (((1,), (1,)), ((), ())), preferred_element_type=F32)
        w = [(_gelu(act[k]) * gate_ref[pl.ds(t0 + k, 1), :]).astype(BF16) for k in range(2)]
        outs = [[], []]
        for j in range(nw // 2):
            for k in range(2):
                next_batch()
                outs[k].append(jnp.dot(
                    w[k], bf_ref[k, :, D_MODEL + j * wide:D_MODEL + (j + 1) * wide],
                    preferred_element_type=F32)[0:1, :])
        for k in range(2):
            out = jnp.concatenate(outs[k], axis=1)
            y_ref[pl.ds(t0 + k, 1), :] = _rms(x1_ref[pl.ds(t0 + k, 1), :] + out, gf_ref[...])

    def group(g, prefetch):
        for p in range(PEER_RING // 2):
            pair(g * PEER_RING + 2 * p, (2 * p, 2 * p + 1), prefetch)

    for t in range(PEER_RING):
        issue(t, t)
    ngroup = nt // PEER_RING
    lax.fori_loop(0, ngroup - 1, lambda g, c: (group(g, True), c)[1], 0)
    group(ngroup - 1, False)


def _peer(idx, gates, xn, x1, gf, uv, first_token):
    t_total = xn.shape[0] - first_token
    nt = PEER_BLOCK
    b0 = first_token // nt
    row = lambda i: (i + b0, 0)
    return pl.pallas_call(
        _peer_kernel,
        grid=(t_total // nt,),
        in_specs=[
            pl.BlockSpec((None, nt, N_SLOTS), lambda i: (i + b0, 0, 0), memory_space=pltpu.SMEM),
            pl.BlockSpec((nt, N_SLOTS), row),
            pl.BlockSpec((nt, D_MODEL), row),
            pl.BlockSpec((nt, D_MODEL), row),
            pl.BlockSpec((1, D_MODEL), lambda i: (0, 0)),
            pl.BlockSpec(memory_space=pl.ANY),
        ],
        out_specs=pl.BlockSpec((nt, D_MODEL), lambda i: (i, 0)),
        out_shape=jax.ShapeDtypeStruct((t_total, D_MODEL), F32),
        scratch_shapes=[pltpu.VMEM((N_SLOTS // SUB, NCH, SUB, LANES), F32)] * PEER_RING + [
            pltpu.VMEM((2, N_SLOTS, 2 * D_MODEL), BF16),
            pltpu.SemaphoreType.DMA((PEER_RING,))],
        compiler_params=pltpu.CompilerParams(
            dimension_semantics=("arbitrary",), vmem_limit_bytes=VMEM_LIMIT),
        name="peer",
    )(idx, gates, xn, x1, gf, uv)


def _sc_peer(idx, xn, gates, uv2, n_tokens):
    info = plsc.get_sparse_core_info()
    nc, lanes_n = info.num_cores, info.num_lanes
    nw = nc * info.num_subcores
    per = n_tokens // nw
    assert n_tokens % nw == 0
    nchunk = N_SLOTS // SC_ROWS
    qv = D_MODEL // (SC_PASSES * lanes_n)
    c0 = math.sqrt(2.0 / math.pi)
    mesh = plsc.VectorSubcoreMesh(core_axis_name="c", subcore_axis_name="s")

    @functools.partial(
        pl.kernel, mesh=mesh,
        out_type=jax.ShapeDtypeStruct((n_tokens, D_MODEL), F32),
        scratch_types=[
            pltpu.VMEM((nchunk, SC_ROWS), I32),
            pltpu.VMEM((D_MODEL,), F32),
            pltpu.VMEM((N_SLOTS,), F32),
            pltpu.VMEM((SC_ROWS, 2 * D_MODEL), F32),
            pltpu.VMEM((SC_ROWS, 2 * D_MODEL), F32),
            pltpu.VMEM((D_MODEL,), F32),
            pltpu.VMEM((SC_ROWS, lanes_n), F32),
            pltpu.VMEM((SC_ROWS,), F32),
            pltpu.SemaphoreType.DMA,
            pltpu.SemaphoreType.DMA,
        ],
        compiler_params=pltpu.CompilerParams(needs_layout_passes=False),
        name="peer_sc",
    )
    def sc_kernel(idx_hbm, xn_hbm, gate_hbm, uv_hbm, out_hbm, idx_v, x_v, g_v, rows0, rows1,
                  out_v, part_v, w_v, sem0, sem1):
        wid = lax.axis_index("s") * nc + lax.axis_index("c")
        lane_ids = lax.iota(I32, lanes_n)
        zero = jnp.zeros((lanes_n,), F32)
        bufs = ((rows0, sem0), (rows1, sem1))

        def vec(q, j):
            return pl.ds((q * qv + j) * lanes_n, lanes_n)

        def gather(c, b):
            rows, sem = bufs[b]
            return pltpu.make_async_copy(uv_hbm.at[idx_v.at[c]], rows, sem)

        def chunk(c, b):
            rows, _ = bufs[b]
            def ustep(j, acc):
                xj = x_v[pl.ds(j * lanes_n, lanes_n)]
                return tuple(acc[r] + rows[r, pl.ds(j * lanes_n, lanes_n)] * xj
                             for r in range(SC_ROWS))

            acc = lax.fori_loop(0, D_MODEL // lanes_n, ustep, (zero,) * SC_ROWS)
            for r in range(SC_ROWS):
                part_v[r, :] = acc[r]
            act = zero
            for l in range(lanes_n):
                act = act + plsc.load_gather(part_v, [lane_ids, jnp.full((lanes_n,), l, I32)])
            z = c0 * (act + 0.044715 * (act * act * act))
            tanh_z = 1.0 - 2.0 / (jnp.exp(2.0 * z) + 1.0)
            w_v[...] = 0.5 * act * (1.0 + tanh_z) * g_v[pl.ds(c * SC_ROWS, SC_ROWS)]
            for q in range(SC_PASSES):
                o = tuple(out_v[vec(q, j)] for j in range(qv))

                def vrow(r, o, q=q):
                    wr = plsc.load_gather(w_v, [jnp.full((lanes_n,), r, I32)])
                    return tuple(o[j] + wr * rows[r, pl.ds(D_MODEL + (q * qv + j) * lanes_n, lanes_n)]
                                 for j in range(qv))

                o = lax.fori_loop(0, SC_ROWS, vrow, o)
                for j in range(qv):
                    out_v[vec(q, j)] = o[j]

        @pl.loop(0, per)
        def _(i):
            t = wid * per + i
            pltpu.sync_copy(idx_hbm.at[t], idx_v)
            pltpu.sync_copy(xn_hbm.at[t], x_v)
            pltpu.sync_copy(gate_hbm.at[t], g_v)
            for j in range(D_MODEL // lanes_n):
                out_v[pl.ds(j * lanes_n, lanes_n)] = zero
            gather(0, 0).start()

            @pl.loop(0, nchunk, step=2)
            def _(c):
                gather(c + 1, 1).start()
                gather(c, 0).wait()
                chunk(c, 0)

                @pl.when(c + 2 < nchunk)
                def _():
                    gather(c + 2, 0).start()

                gather(c + 1, 1).wait()
                chunk(c + 1, 1)

            pltpu.sync_copy(out_v, out_hbm.at[t])

    return sc_kernel(idx.reshape(-1, nchunk, SC_ROWS), xn, gates, uv2)


def _residual_norm_kernel(x1_ref, o_ref, gf_ref, y_ref):
    y_ref[...] = _rms(x1_ref[...] + o_ref[...], gf_ref[...])


def _residual_norm(x1, out, gf):
    n = out.shape[0]
    nt = PEER_BLOCK
    row = lambda i: (i, 0)
    return pl.pallas_call(
        _residual_norm_kernel,
        grid=(n // nt,),
        in_specs=[pl.BlockSpec((nt, D_MODEL), row), pl.BlockSpec((nt, D_MODEL), row),
                  pl.BlockSpec((1, D_MODEL), lambda i: (0, 0))],
        out_specs=pl.BlockSpec((nt, D_MODEL), row),
        out_shape=jax.ShapeDtypeStruct((n, D_MODEL), F32),
        compiler_params=pltpu.CompilerParams(dimension_semantics=("arbitrary",)),
        name="residual_norm",
    )(x1, out, gf)


def _rope_tables(seq):
    pos = jnp.arange(seq, dtype=F32)
    inv = 1.0 / (ROPE_THETA ** (jnp.arange(0, HEAD_DIM, 2, dtype=F32) / HEAD_DIM))
    ang = pos[:, None] * inv[None, :]
    cos, sin = jnp.cos(ang), jnp.sin(ang)
    cosf = jnp.tile(jnp.concatenate([cos, cos], axis=1), (1, B_HEADS))
    sins = jnp.tile(jnp.concatenate([-sin, sin], axis=1), (1, B_HEADS))
    return cosf, sins


def kernel(x, norm1_g, w_in, ln_v_g, ln_v_b, w_spatial, b_spatial, out_norm_a_g, out_norm_b_g,
           w_out, norm2_g, w_query, sub_keys, expert_u, expert_v, final_norm_g):
    batch, seq, _ = x.shape
    assert w_in.shape[0] == 1 and seq % (16 * QBLK) == 0 and seq % IN_BLOCK == 0
    x2 = x.reshape(batch * seq, D_MODEL)
    row = lambda g: g.reshape(1, -1).astype(F32)

    ws = w_spatial[0].astype(BF16)
    ws_cat = jnp.concatenate([ws[0::2], ws[1::2]], axis=2)
    bs_full = jnp.repeat(b_spatial[0].T, A_GROUP_DIM, axis=1)
    cosf, sins = _rope_tables(seq)

    an, q1, k1, v1, q4, k4, v4, q16, k16, v16 = _in_proj(
        x2, row(norm1_g[0]), w_in[0].astype(BF16), row(ln_v_g[0]), row(ln_v_b[0]),
        ws_cat, bs_full, row(out_norm_a_g[0]), cosf, sins, batch, seq)

    bo = _attention(q1, k1, v1, q4, k4, v4, q16, k16, v16, batch, seq)
    bo = bo.reshape(batch * seq, D_B)

    x1, xn, idx, gates = _mid(
        x2, an, bo, row(out_norm_b_g[0]), w_out[0].astype(BF16), row(norm2_g[0]),
        w_query[0].astype(BF16), sub_keys[0].astype(BF16))

    uv = jnp.concatenate([expert_u[0].reshape(-1, NCH // 2, LANES),
                          expert_v[0].reshape(-1, NCH // 2, LANES)], axis=1)
    n_sc = (batch * seq) * SC_SHARE[0] // SC_SHARE[1]
    assert n_sc % PEER_BLOCK == 0
    gf = row(final_norm_g)
    uv2 = jnp.concatenate([expert_u[0], expert_v[0]], axis=1)
    out_sc = _sc_peer(idx.reshape(batch * seq, N_SLOTS), xn, gates, uv2, n_sc)
    y_tc = _peer(idx, gates, xn, x1, gf, uv, n_sc)
    y_sc = _residual_norm(x1, out_sc, gf)
    return jnp.concatenate([y_sc, y_tc], axis=0).reshape(batch, seq, D_MODEL)
```

```python
import functools
import math

import jax
import jax.numpy as jnp
from jax import lax
from jax.experimental import pallas as pl
from jax.experimental.pallas import tpu as pltpu
from jax.experimental.pallas import tpu_sc as plsc

F32 = jnp.float32
BF16 = jnp.bfloat16
I32 = jnp.int32

D_MODEL = 1024
D_A = 512
D_B = 512
A_GROUPS = 8
A_GROUP_DIM = 64
CHUNK = 128
B_HEADS = 8
HEAD_DIM = 64
DILATIONS = (1, 4, 16)
HALF_WINDOW = 64
ROPE_THETA = 10000.0
D_IN = 2 * D_A + 3 * D_B
N_KEYS = 128
PEER_HEADS = 8
PEER_TOPK = 16
D_KEY = 256
N_SLOTS = PEER_HEADS * PEER_TOPK
EPS = 1e-6
NEG_BIG = -1e30

LANES = 128
SUB = 8
NCH = 2 * D_MODEL // LANES
QBLK = 128
IN_BLOCK = 512
MID_BLOCK = 256
PEER_BLOCK = 128
PEER_RING = 4
SC_SHARE = (9, 16)
PIPE_CHUNKS = 2
SC_ROWS = 16
SC_PASSES = 4
VMEM_LIMIT = 48 * 1024 * 1024


def _gelu(x):
    c = math.sqrt(2.0 / math.pi)
    return 0.5 * x * (1.0 + jnp.tanh(c * (x + 0.044715 * (x * x * x))))


def _rms(x, g):
    return x * lax.rsqrt(jnp.mean(x * x, axis=-1, keepdims=True) + EPS) * g


def _in_proj_kernel(x_ref, g1_ref, win_ref, lng_ref, lnb_ref, ws_ref, bs_ref, ga_ref,
                    cos_ref, sin_ref,
                    an_ref, q1_ref, k1_ref, v1_ref, q4_ref, k4_ref, v4_ref,
                    q16_ref, k16_ref, v16_ref, slab_ref):
    nt = x_ref.shape[0]
    h = _rms(x_ref[...], g1_ref[...]).astype(BF16)
    proj = jnp.dot(h, win_ref[...], preferred_element_type=F32)

    u = _gelu(proj[:, :D_A])
    v = _gelu(proj[:, D_A:2 * D_A])
    mu = jnp.mean(v, axis=-1, keepdims=True)
    vc = v - mu
    var = jnp.mean(vc * vc, axis=-1, keepdims=True)
    vln = (vc * lax.rsqrt(var + EPS) * lng_ref[...] + lnb_ref[...]).astype(BF16)
    lane = lax.broadcasted_iota(I32, (CHUNK, LANES), 1)
    lo = lane < A_GROUP_DIM
    zero = jnp.zeros((CHUNK, LANES), BF16)
    chunks = []
    for c in range(nt // CHUNK):
        cols = []
        for j in range(A_GROUPS // 2):
            vv = vln[c * CHUNK:(c + 1) * CHUNK, j * LANES:(j + 1) * LANES]
            rhs = jnp.concatenate([jnp.where(lo, vv, zero), jnp.where(lo, zero, vv)], axis=0)
            cols.append(jnp.dot(ws_ref[j], rhs, preferred_element_type=F32))
        chunks.append(jnp.concatenate(cols, axis=1) + bs_ref[...])
    mixed = jnp.concatenate(chunks, axis=0)
    an_ref[...] = _rms(u * mixed, ga_ref[...]).astype(BF16)

    cosf = cos_ref[...]
    sins = sin_ref[...]
    lane_b = lax.broadcasted_iota(I32, (nt, D_B), 1)
    first_half = (lane_b % HEAD_DIM) < (HEAD_DIM // 2)

    def rope(t):
        partner = jnp.where(first_half, pltpu.roll(t, D_B - HEAD_DIM // 2, 1),
                            pltpu.roll(t, HEAD_DIM // 2, 1))
        return t * cosf + partner * sins

    q = rope(proj[:, 2 * D_A:2 * D_A + D_B]) * (HEAD_DIM ** -0.5)
    k = rope(proj[:, 2 * D_A + D_B:2 * D_A + 2 * D_B])
    vv = proj[:, 2 * D_A + 2 * D_B:]
    q1_ref[...] = q.astype(BF16)
    k1_ref[...] = k.astype(BF16)
    v1_ref[...] = vv.astype(BF16)

    nslab = D_B // LANES
    for a, t in enumerate((q, k, vv)):
        for s in range(nslab):
            slab_ref[a * nslab + s] = t[:, s * LANES:(s + 1) * LANES]
    for d, outs in ((4, (q4_ref, k4_ref, v4_ref)), (16, (q16_ref, k16_ref, v16_ref))):
        rows = nt // d
        for a, o_ref in enumerate(outs):
            for r in range(d):
                for s in range(nslab):
                    o_ref[r, :, s * LANES:(s + 1) * LANES] = (
                        slab_ref[a * nslab + s, pl.ds(r, rows, stride=d), :].astype(BF16))


def _in_proj(x2, g1, win, lng, lnb, ws_cat, bs_full, ga, cosf, sins, batch, seq):
    t_total = x2.shape[0]
    nt = IN_BLOCK
    nb = seq // nt
    grid = (t_total // nt,)
    row = lambda i: (i, 0)
    const2 = lambda i: (0, 0)
    tok_bf = jax.ShapeDtypeStruct((t_total, D_B), BF16)
    out_shape = (
        jax.ShapeDtypeStruct((t_total, D_A), BF16),
        tok_bf, tok_bf, tok_bf,
        *(jax.ShapeDtypeStruct((batch, 4, seq // 4, D_B), BF16),) * 3,
        *(jax.ShapeDtypeStruct((batch, 16, seq // 16, D_B), BF16),) * 3,
    )
    res4 = pl.BlockSpec((None, 4, nt // 4, D_B), lambda i: (i // nb, 0, i % nb, 0))
    res16 = pl.BlockSpec((None, 16, nt // 16, D_B), lambda i: (i // nb, 0, i % nb, 0))
    tok_spec = pl.BlockSpec((nt, D_B), row)
    return pl.pallas_call(
        _in_proj_kernel,
        grid=grid,
        in_specs=[
            pl.BlockSpec((nt, D_MODEL), row),
            pl.BlockSpec((1, D_MODEL), const2),
            pl.BlockSpec((D_MODEL, D_IN), const2),
            pl.BlockSpec((1, D_A), const2),
            pl.BlockSpec((1, D_A), const2),
            pl.BlockSpec((A_GROUPS // 2, CHUNK, 2 * CHUNK), lambda i: (0, 0, 0)),
            pl.BlockSpec((CHUNK, D_A), const2),
            pl.BlockSpec((1, D_A), const2),
            pl.BlockSpec((nt, D_B), lambda i: (i % nb, 0)),
            pl.BlockSpec((nt, D_B), lambda i: (i % nb, 0)),
        ],
        out_specs=(pl.BlockSpec((nt, D_A), row), tok_spec, tok_spec, tok_spec,
                   res4, res4, res4, res16, res16, res16),
        out_shape=out_shape,
        scratch_shapes=[pltpu.VMEM((3 * D_B // LANES, nt, LANES), F32)],
        compiler_params=pltpu.CompilerParams(
            dimension_semantics=("arbitrary",), vmem_limit_bytes=VMEM_LIMIT),
        name="in_proj",
    )(x2, g1, win, lng, lnb, ws_cat, bs_full, ga, cosf, sins)


def _attn_kernel(q1_ref, k1_ref, v1_ref, q4_ref, k4_ref, v4_ref, q16_ref, k16_ref, v16_ref,
                 o_ref, acc_ref, m_ref, l_ref):
    seq = o_ref.shape[0]
    lane = lax.broadcasted_iota(I32, (QBLK, LANES), 1)
    head0 = lane < HEAD_DIM
    branches = ((1, q1_ref, k1_ref, v1_ref), (4, q4_ref, k4_ref, v4_ref),
                (16, q16_ref, k16_ref, v16_ref))
    for bi, (d, q_ref, k_ref, v_ref) in enumerate(branches):
        length = seq // d
        nblk = length // QBLK
        win = min(2 * QBLK, length)
        diff = (lax.broadcasted_iota(I32, (QBLK, win), 1)
                - lax.broadcasted_iota(I32, (QBLK, win), 0))

        def block(blk, carry, d=d, bi=bi, q_ref=q_ref, k_ref=k_ref, v_ref=v_ref,
                  length=length, nblk=nblk, win=win, diff=diff):
            r = blk // nblk
            i0 = pl.multiple_of((blk % nblk) * QBLK, QBLK)
            w0 = pl.multiple_of(jnp.clip(i0 - HALF_WINDOW, 0, length - win), HALF_WINDOW)
            qb = q_ref[r, pl.ds(i0, QBLK), :]
            kw = k_ref[r, pl.ds(w0, win), :]
            vw = v_ref[r, pl.ds(w0, win), :]
            rel = diff + (w0 - i0)
            valid = (rel >= -HALF_WINDOW) & (rel <= HALF_WINDOW)
            accs, ms, ls = [], [], []
            for hsel in (head0, ~head0):
                qh = jnp.where(hsel, qb, jnp.zeros_like(qb))
                s = lax.dot_general(qh, kw, (((1,), (1,)), ((), ())),
                                    preferred_element_type=F32)
                s = jnp.where(valid, s, NEG_BIG)
                m = jnp.max(s, axis=1, keepdims=True)
                p = jnp.exp(s - m)
                ls.append(jnp.sum(p, axis=1, keepdims=True))
                ms.append(m)
                accs.append(jnp.dot(p.astype(BF16), vw, preferred_element_type=F32))
            acc = jnp.where(head0, accs[0], accs[1])
            mm = jnp.where(head0, ms[0], ms[1])
            ll = jnp.where(head0, ls[0], ls[1])
            if d == 1:
                rows = pl.ds(i0, QBLK)
            else:
                rows = pl.ds(i0 * d + r, QBLK, stride=d)
            acc_ref[bi, rows, :] = acc
            m_ref[bi, rows, :] = mm
            l_ref[bi, rows, :] = ll
            return carry

        lax.fori_loop(0, d * nblk, block, 0)

    def merge(c, carry):
        rows = pl.ds(pl.multiple_of(c * QBLK, QBLK), QBLK)
        m1, m2, m3 = m_ref[0, rows, :], m_ref[1, rows, :], m_ref[2, rows, :]
        mx = jnp.maximum(jnp.maximum(m1, m2), m3)
        w1, w2, w3 = jnp.exp(m1 - mx), jnp.exp(m2 - mx), jnp.exp(m3 - mx)
        num = w1 * acc_ref[0, rows, :] + w2 * acc_ref[1, rows, :] + w3 * acc_ref[2, rows, :]
        den = w1 * l_ref[0, rows, :] + w2 * l_ref[1, rows, :] + w3 * l_ref[2, rows, :]
        o_ref[rows, :] = num / den
        return carry

    lax.fori_loop(0, seq // QBLK, merge, 0)


def _attention(q1, k1, v1, q4, k4, v4, q16, k16, v16, batch, seq):
    npair = D_B // LANES
    nat = pl.BlockSpec((None, 1, seq, LANES), lambda b, p: (b, 0, 0, p))
    r4 = pl.BlockSpec((None, 4, seq // 4, LANES), lambda b, p: (b, 0, 0, p))
    r16 = pl.BlockSpec((None, 16, seq // 16, LANES), lambda b, p: (b, 0, 0, p))
    q1, k1, v1 = (t.reshape(batch, 1, seq, D_B) for t in (q1, k1, v1))
    return pl.pallas_call(
        _attn_kernel,
        grid=(batch, npair),
        in_specs=[nat, nat, nat, r4, r4, r4, r16, r16, r16],
        out_specs=pl.BlockSpec((None, seq, LANES), lambda b, p: (b, 0, p)),
        out_shape=jax.ShapeDtypeStruct((batch, seq, D_B), F32),
        scratch_shapes=[pltpu.VMEM((3, seq, LANES), F32)] * 3,
        compiler_params=pltpu.CompilerParams(
            dimension_semantics=("arbitrary", "arbitrary"), vmem_limit_bytes=VMEM_LIMIT),
        name="dilated_attn",
    )(q1, k1, v1, q4, k4, v4, q16, k16, v16)


def _topk_rows(s, k):
    n = s.shape[0]
    iota = lax.broadcasted_iota(I32, s.shape, 0)
    vals, idxs = [], []
    for _ in range(k):
        m = jnp.max(s, axis=0, keepdims=True)
        i = jnp.min(jnp.where(s == m, iota, n), axis=0, keepdims=True)
        vals.append(m)
        idxs.append(i)
        s = jnp.where(iota == i, -jnp.inf, s)
    return jnp.concatenate(vals, axis=0), jnp.concatenate(idxs, axis=0)


def _take_rows(table, sel):
    out = jnp.zeros(sel.shape, table.dtype)
    for a in range(table.shape[0]):
        out = jnp.where(sel == a, table[a:a + 1, :], out)
    return out


def _mid_kernel(x_ref, an_ref, bo_ref, gb_ref, wout_ref, g2_ref, wq_ref, keys_ref,
                x1_ref, xn_ref, idx_ref, gate_ref):
    nt = x_ref.shape[0]
    bn = _rms(bo_ref[...], gb_ref[...]).astype(BF16)
    x1 = (x_ref[...]
          + jnp.dot(an_ref[...], wout_ref[:D_A, :], preferred_element_type=F32)
          + jnp.dot(bn, wout_ref[D_A:, :], preferred_element_type=F32))
    x1_ref[...] = x1
    xn = _rms(x1, g2_ref[...])
    xn_ref[...] = xn
    q = jnp.dot(xn.astype(BF16), wq_ref[...], preferred_element_type=F32).astype(BF16)
    keys = (keys_ref[0], keys_ref[1])
    half = D_KEY // 2
    for c in range(nt // LANES):
        qc = q[c * LANES:(c + 1) * LANES, :]
        experts, gates = [], []
        for h in range(PEER_HEADS):
            tops = []
            for p in range(2):
                qhp = qc[:, (2 * h + p) * half:(2 * h + p + 1) * half]
                s = lax.dot_general(keys[p], qhp, (((1,), (1,)), ((), ())),
                                    preferred_element_type=F32)
                tops.append(_topk_rows(s, PEER_TOPK))
            (s1, i1), (s2, i2) = tops
            cand = jnp.concatenate([s1[a:a + 1, :] + s2 for a in range(PEER_TOPK)], axis=0)
            sc, ci = _topk_rows(cand, PEER_TOPK)
            e = (_take_rows(i1, ci >> 4) * N_KEYS + _take_rows(i2, ci & (PEER_TOPK - 1)))
            ex = jnp.exp(sc - sc[0:1, :])
            gates.append(ex / jnp.sum(ex, axis=0, keepdims=True))
            experts.append(e)
        idx_ref[c] = jnp.concatenate(experts, axis=0).T
        gate_ref[c * LANES:(c + 1) * LANES, :] = jnp.concatenate(gates, axis=0).T


def _mid(x2, an, bo, gb, wout, g2, wq, keys):
    t_total = x2.shape[0]
    nt = MID_BLOCK
    row = lambda i: (i, 0)
    const2 = lambda i: (0, 0)
    return pl.pallas_call(
        _mid_kernel,
        grid=(t_total // nt,),
        in_specs=[
            pl.BlockSpec((nt, D_MODEL), row),
            pl.BlockSpec((nt, D_A), row),
            pl.BlockSpec((nt, D_B), row),
            pl.BlockSpec((1, D_B), const2),
            pl.BlockSpec((D_MODEL, D_MODEL), const2),
            pl.BlockSpec((1, D_MODEL), const2),
            pl.BlockSpec((D_MODEL, PEER_HEADS * D_KEY), const2),
            pl.BlockSpec((2, N_KEYS, D_KEY // 2), lambda i: (0, 0, 0)),
        ],
        out_specs=(
            pl.BlockSpec((nt, D_MODEL), row),
            pl.BlockSpec((nt, D_MODEL), row),
            pl.BlockSpec((nt // LANES, LANES, N_SLOTS), lambda i: (i, 0, 0)),
            pl.BlockSpec((nt, N_SLOTS), row),
        ),
        out_shape=(
            jax.ShapeDtypeStruct((t_total, D_MODEL), F32),
            jax.ShapeDtypeStruct((t_total, D_MODEL), F32),
            jax.ShapeDtypeStruct((t_total // LANES, LANES, N_SLOTS), I32),
            jax.ShapeDtypeStruct((t_total, N_SLOTS), F32),
        ),
        compiler_params=pltpu.CompilerParams(
            dimension_semantics=("arbitrary",), vmem_limit_bytes=VMEM_LIMIT),
        name="mid",
    )(x2, an, bo, gb, wout, g2, wq, keys)


def _peer_kernel(idx_ref, gate_ref, xn_ref, x1_ref, gf_ref, uv_ref, y_ref, *scratch):
    rows_refs = scratch[:PEER_RING]
    bf_ref, sem_ref = scratch[PEER_RING:]
    nt = xn_ref.shape[0]
    wide = 2 * LANES
    nw = 2 * D_MODEL // wide

    def issue(t, slot, lo=0, hi=N_SLOTS):
        for s in range(lo, hi):
            pltpu.make_async_copy(uv_ref.at[idx_ref[t, s]],
                                  rows_refs[slot].at[s // SUB, :, s % SUB, :],
                                  sem_ref.at[slot]).start(priority=s % 2)

    def wait(slot):
        pltpu.make_async_copy(uv_ref.at[pl.ds(0, N_SLOTS)],
                              rows_refs[slot].reshape(N_SLOTS, NCH, LANES), sem_ref.at[slot]).wait()

    def stage(slot, k):
        for j in range(NCH):
            bf_ref[k, :, j * LANES:(j + 1) * LANES] = (
                rows_refs[slot][:, j, :, :].reshape(N_SLOTS, LANES).astype(BF16))

    def pair(t0, slots, prefetch):
        for k in range(2):
            wait(slots[k])
            stage(slots[k], k)
        nbatch = 2 * nw
        per = N_SLOTS // (nbatch // 2)
        batches = [(k, b * per, (b + 1) * per) for b in range(nbatch // 2) for k in range(2)]

        def next_batch():
            if prefetch and batches:
                k, lo, hi = batches.pop(0)
                issue(t0 + PEER_RING + k, slots[k], lo, hi)

        x8 = [jnp.broadcast_to(xn_ref[pl.ds(t0 + k, 1), :], (SUB, D_MODEL)).astype(BF16)
              for k in range(2)]
        act = [jnp.zeros((SUB, N_SLOTS), F32) for _ in range(2)]
        for j in range(nw // 2):
            for k in range(2):
                next_batch()
                act[k] = act[k] + lax.dot_general(
                    x8[k][:, j * wide:(j + 1) * wide], bf_ref[k, :, j * wide:(j + 1) * wide],
                    (((1,), (1,)), ((), ())), preferred_element_type=F32)
        w = [(_gelu(act[k]) * gate_ref[pl.ds(t0 + k, 1), :]).astype(BF16) for k in range(2)]
        outs = [[], []]
        for j in range(nw // 2):
            for k in range(2):
                next_batch()
                outs[k].append(jnp.dot(
                    w[k], bf_ref[k, :, D_MODEL + j * wide:D_MODEL + (j + 1) * wide],
                    preferred_element_type=F32)[0:1, :])
        for k in range(2):
            out = jnp.concatenate(outs[k], axis=1)
            y_ref[pl.ds(t0 + k, 1), :] = _rms(x1_ref[pl.ds(t0 + k, 1), :] + out, gf_ref[...])

    def group(g, prefetch):
        for p in range(PEER_RING // 2):
            pair(g * PEER_RING + 2 * p, (2 * p, 2 * p + 1), prefetch)

    for t in range(PEER_RING):
        issue(t, t)
    ngroup = nt // PEER_RING
    lax.fori_loop(0, ngroup - 1, lambda g, c: (group(g, True), c)[1], 0)
    group(ngroup - 1, False)


def _peer(idx, gates, xn, x1, gf, uv, first_token):
    t_total = xn.shape[0] - first_token
    nt = PEER_BLOCK
    b0 = first_token // nt
    row = lambda i: (i + b0, 0)
    return pl.pallas_call(
        _peer_kernel,
        grid=(t_total // nt,),
        in_specs=[
            pl.BlockSpec((None, nt, N_SLOTS), lambda i: (i + b0, 0, 0), memory_space=pltpu.SMEM),
            pl.BlockSpec((nt, N_SLOTS), row),
            pl.BlockSpec((nt, D_MODEL), row),
            pl.BlockSpec((nt, D_MODEL), row),
            pl.BlockSpec((1, D_MODEL), lambda i: (0, 0)),
            pl.BlockSpec(memory_space=pl.ANY),
        ],
        out_specs=pl.BlockSpec((nt, D_MODEL), lambda i: (i, 0)),
        out_shape=jax.ShapeDtypeStruct((t_total, D_MODEL), F32),
        scratch_shapes=[pltpu.VMEM((N_SLOTS // SUB, NCH, SUB, LANES), F32)] * PEER_RING + [
            pltpu.VMEM((2, N_SLOTS, 2 * D_MODEL), BF16),
            pltpu.SemaphoreType.DMA((PEER_RING,))],
        compiler_params=pltpu.CompilerParams(
            dimension_semantics=("arbitrary",), vmem_limit_bytes=VMEM_LIMIT),
        name="peer",
    )(idx, gates, xn, x1, gf, uv)


def _sc_peer(idx, xn, gates, uv2, n_tokens):
    info = plsc.get_sparse_core_info()
    nc, lanes_n = info.num_cores, info.num_lanes
    nw = nc * info.num_subcores
    per = n_tokens // nw
    assert n_tokens % nw == 0
    nchunk = N_SLOTS // SC_ROWS
    qv = D_MODEL // (SC_PASSES * lanes_n)
    c0 = math.sqrt(2.0 / math.pi)
    mesh = plsc.VectorSubcoreMesh(core_axis_name="c", subcore_axis_name="s")

    @functools.partial(
        pl.kernel, mesh=mesh,
        out_type=jax.ShapeDtypeStruct((n_tokens, D_MODEL), F32),
        scratch_types=[
            pltpu.VMEM((nchunk, SC_ROWS), I32),
            pltpu.VMEM((D_MODEL,), F32),
            pltpu.VMEM((N_SLOTS,), F32),
            pltpu.VMEM((SC_ROWS, 2 * D_MODEL), F32),
            pltpu.VMEM((SC_ROWS, 2 * D_MODEL), F32),
            pltpu.VMEM((D_MODEL,), F32),
            pltpu.VMEM((SC_ROWS, lanes_n), F32),
            pltpu.VMEM((SC_ROWS,), F32),
            pltpu.SemaphoreType.DMA,
            pltpu.SemaphoreType.DMA,
        ],
        compiler_params=pltpu.CompilerParams(needs_layout_passes=False),
        name="peer_sc",
    )
    def sc_kernel(idx_hbm, xn_hbm, gate_hbm, uv_hbm, out_hbm, idx_v, x_v, g_v, rows0, rows1,
                  out_v, part_v, w_v, sem0, sem1):
        wid = lax.axis_index("s") * nc + lax.axis_index("c")
        lane_ids = lax.iota(I32, lanes_n)
        zero = jnp.zeros((lanes_n,), F32)
        bufs = ((rows0, sem0), (rows1, sem1))

        def vec(q, j):
            return pl.ds((q * qv + j) * lanes_n, lanes_n)

        def gather(c, b):
            rows, sem = bufs[b]
            return pltpu.make_async_copy(uv_hbm.at[idx_v.at[c]], rows, sem)

        def chunk(c, b):
            rows, _ = bufs[b]
            def ustep(j, acc):
                xj = x_v[pl.ds(j * lanes_n, lanes_n)]
                return tuple(acc[r] + rows[r, pl.ds(j * lanes_n, lanes_n)] * xj
                             for r in range(SC_ROWS))

            acc = lax.fori_loop(0, D_MODEL // lanes_n, ustep, (zero,) * SC_ROWS)
            for r in range(SC_ROWS):
                part_v[r, :] = acc[r]
            act = zero
            for l in range(lanes_n):
                act = act + plsc.load_gather(part_v, [lane_ids, jnp.full((lanes_n,), l, I32)])
            z = c0 * (act + 0.044715 * (act * act * act))
            tanh_z = 1.0 - 2.0 / (jnp.exp(2.0 * z) + 1.0)
            w_v[...] = 0.5 * act * (1.0 + tanh_z) * g_v[pl.ds(c * SC_ROWS, SC_ROWS)]
            for q in range(SC_PASSES):
                o = tuple(out_v[vec(q, j)] for j in range(qv))

                def vrow(r, o, q=q):
                    wr = plsc.load_gather(w_v, [jnp.full((lanes_n,), r, I32)])
                    return tuple(o[j] + wr * rows[r, pl.ds(D_MODEL + (q * qv + j) * lanes_n, lanes_n)]
                                 for j in range(qv))

                o = lax.fori_loop(0, SC_ROWS, vrow, o)
                for j in range(qv):
                    out_v[vec(q, j)] = o[j]

        @pl.loop(0, per)
        def _(i):
            t = wid * per + i
            pltpu.sync_copy(idx_hbm.at[t], idx_v)
            pltpu.sync_copy(xn_hbm.at[t], x_v)
            pltpu.sync_copy(gate_hbm.at[t], g_v)
            for j in range(D_MODEL // lanes_n):
                out_v[pl.ds(j * lanes_n, lanes_n)] = zero
            gather(0, 0).start()

            @pl.loop(0, nchunk, step=2)
            def _(c):
                gather(c + 1, 1).start()
                gather(c, 0).wait()
                chunk(c, 0)

                @pl.when(c + 2 < nchunk)
                def _():
                    gather(c + 2, 0).start()

                gather(c + 1, 1).wait()
                chunk(c + 1, 1)

            pltpu.sync_copy(out_v, out_hbm.at[t])

    return sc_kernel(idx.reshape(-1, nchunk, SC_ROWS), xn, gates, uv2)


def _residual_norm_kernel(x1_ref, o_ref, gf_ref, y_ref):
    y_ref[...] = _rms(x1_ref[...] + o_ref[...], gf_ref[...])


def _residual_norm(x1, out, gf):
    n = out.shape[0]
    nt = PEER_BLOCK
    row = lambda i: (i, 0)
    return pl.pallas_call(
        _residual_norm_kernel,
        grid=(n // nt,),
        in_specs=[pl.BlockSpec((nt, D_MODEL), row), pl.BlockSpec((nt, D_MODEL), row),
                  pl.BlockSpec((1, D_MODEL), lambda i: (0, 0))],
        out_specs=pl.BlockSpec((nt, D_MODEL), row),
        out_shape=jax.ShapeDtypeStruct((n, D_MODEL), F32),
        compiler_params=pltpu.CompilerParams(dimension_semantics=("arbitrary",)),
        name="residual_norm",
    )(x1, out, gf)


def _rope_tables(seq):
    pos = jnp.arange(seq, dtype=F32)
    inv = 1.0 / (ROPE_THETA ** (jnp.arange(0, HEAD_DIM, 2, dtype=F32) / HEAD_DIM))
    ang = pos[:, None] * inv[None, :]
    cos, sin = jnp.cos(ang), jnp.sin(ang)
    cosf = jnp.tile(jnp.concatenate([cos, cos], axis=1), (1, B_HEADS))
    sins = jnp.tile(jnp.concatenate([-sin, sin], axis=1), (1, B_HEADS))
    return cosf, sins


def kernel(x, norm1_g, w_in, ln_v_g, ln_v_b, w_spatial, b_spatial, out_norm_a_g, out_norm_b_g,
           w_out, norm2_g, w_query, sub_keys, expert_u, expert_v, final_norm_g):
    batch, seq, _ = x.shape
    assert w_in.shape[0] == 1 and seq % (16 * QBLK) == 0 and seq % IN_BLOCK == 0
    row = lambda g: g.reshape(1, -1).astype(F32)

    ws = w_spatial[0].astype(BF16)
    ws_cat = jnp.concatenate([ws[0::2], ws[1::2]], axis=2)
    bs_full = jnp.repeat(b_spatial[0].T, A_GROUP_DIM, axis=1)
    cosf, sins = _rope_tables(seq)
    win, wout, wq = w_in[0].astype(BF16), w_out[0].astype(BF16), w_query[0].astype(BF16)
    keys = sub_keys[0].astype(BF16)
    gf = row(final_norm_g)
    uv = jnp.concatenate([expert_u[0].reshape(-1, NCH // 2, LANES),
                          expert_v[0].reshape(-1, NCH // 2, LANES)], axis=1)
    uv2 = jnp.concatenate([expert_u[0], expert_v[0]], axis=1)

    nchunk = PIPE_CHUNKS if batch % PIPE_CHUNKS == 0 else 1
    cb = batch // nchunk
    pieces = []
    for ci in range(nchunk):
        x2 = x[ci * cb:(ci + 1) * cb].reshape(cb * seq, D_MODEL)
        an, q1, k1, v1, q4, k4, v4, q16, k16, v16 = _in_proj(
            x2, row(norm1_g[0]), win, row(ln_v_g[0]), row(ln_v_b[0]),
            ws_cat, bs_full, row(out_norm_a_g[0]), cosf, sins, cb, seq)
        bo = _attention(q1, k1, v1, q4, k4, v4, q16, k16, v16, cb, seq).reshape(cb * seq, D_B)
        x1, xn, idx, gates = _mid(x2, an, bo, row(out_norm_b_g[0]), wout, row(norm2_g[0]), wq, keys)
        n_sc = (cb * seq) * SC_SHARE[0] // SC_SHARE[1]
        assert n_sc % PEER_BLOCK == 0
        out_sc = _sc_peer(idx.reshape(cb * seq, N_SLOTS), xn, gates, uv2, n_sc)
        y_tc = _peer(idx, gates, xn, x1, gf, uv, n_sc)
        pieces += [_residual_norm(x1, out_sc, gf), y_tc]
    return jnp.concatenate(pieces, axis=0).reshape(batch, seq, D_MODEL)
```

```python
import functools
import math

import jax
import jax.numpy as jnp
from jax import lax
from jax.experimental import pallas as pl
from jax.experimental.pallas import tpu as pltpu
from jax.experimental.pallas import tpu_sc as plsc

F32 = jnp.float32
BF16 = jnp.bfloat16
I32 = jnp.int32

D_MODEL = 1024
D_A = 512
D_B = 512
A_GROUPS = 8
A_GROUP_DIM = 64
CHUNK = 128
B_HEADS = 8
HEAD_DIM = 64
DILATIONS = (1, 4, 16)
HALF_WINDOW = 64
ROPE_THETA = 10000.0
D_IN = 2 * D_A + 3 * D_B
N_KEYS = 128
PEER_HEADS = 8
PEER_TOPK = 16
D_KEY = 256
N_SLOTS = PEER_HEADS * PEER_TOPK
EPS = 1e-6
NEG_BIG = -1e30

LANES = 128
SUB = 8
NCH = D_MODEL // LANES
U_HALF = -65536
QBLK = 128
IN_BLOCK = 512
MID_BLOCK = 256
PEER_BLOCK = 128
PEER_RING = 4
SC_SHARE = (9, 16)
PIPE_CHUNKS = 2
SC_ROWS = 16
SC_PASSES = 4
VMEM_LIMIT = 48 * 1024 * 1024


def _gelu(x):
    c = math.sqrt(2.0 / math.pi)
    return 0.5 * x * (1.0 + jnp.tanh(c * (x + 0.044715 * (x * x * x))))


def _rms(x, g):
    return x * lax.rsqrt(jnp.mean(x * x, axis=-1, keepdims=True) + EPS) * g


def _in_proj_kernel(x_ref, g1_ref, win_ref, lng_ref, lnb_ref, ws_ref, bs_ref, ga_ref,
                    cos_ref, sin_ref,
                    an_ref, q1_ref, k1_ref, v1_ref, q4_ref, k4_ref, v4_ref,
                    q16_ref, k16_ref, v16_ref, slab_ref):
    nt = x_ref.shape[0]
    h = _rms(x_ref[...], g1_ref[...]).astype(BF16)
    proj = jnp.dot(h, win_ref[...], preferred_element_type=F32)

    u = _gelu(proj[:, :D_A])
    v = _gelu(proj[:, D_A:2 * D_A])
    mu = jnp.mean(v, axis=-1, keepdims=True)
    vc = v - mu
    var = jnp.mean(vc * vc, axis=-1, keepdims=True)
    vln = (vc * lax.rsqrt(var + EPS) * lng_ref[...] + lnb_ref[...]).astype(BF16)
    lane = lax.broadcasted_iota(I32, (CHUNK, LANES), 1)
    lo = lane < A_GROUP_DIM
    zero = jnp.zeros((CHUNK, LANES), BF16)
    chunks = []
    for c in range(nt // CHUNK):
        cols = []
        for j in range(A_GROUPS // 2):
            vv = vln[c * CHUNK:(c + 1) * CHUNK, j * LANES:(j + 1) * LANES]
            rhs = jnp.concatenate([jnp.where(lo, vv, zero), jnp.where(lo, zero, vv)], axis=0)
            cols.append(jnp.dot(ws_ref[j], rhs, preferred_element_type=F32))
        chunks.append(jnp.concatenate(cols, axis=1) + bs_ref[...])
    mixed = jnp.concatenate(chunks, axis=0)
    an_ref[...] = _rms(u * mixed, ga_ref[...]).astype(BF16)

    cosf = cos_ref[...]
    sins = sin_ref[...]
    lane_b = lax.broadcasted_iota(I32, (nt, D_B), 1)
    first_half = (lane_b % HEAD_DIM) < (HEAD_DIM // 2)

    def rope(t):
        partner = jnp.where(first_half, pltpu.roll(t, D_B - HEAD_DIM // 2, 1),
                            pltpu.roll(t, HEAD_DIM // 2, 1))
        return t * cosf + partner * sins

    q = rope(proj[:, 2 * D_A:2 * D_A + D_B]) * (HEAD_DIM ** -0.5)
    k = rope(proj[:, 2 * D_A + D_B:2 * D_A + 2 * D_B])
    vv = proj[:, 2 * D_A + 2 * D_B:]
    q1_ref[...] = q.astype(BF16)
    k1_ref[...] = k.astype(BF16)
    v1_ref[...] = vv.astype(BF16)

    nslab = D_B // LANES
    for a, t in enumerate((q, k, vv)):
        for s in range(nslab):
            slab_ref[a * nslab + s] = t[:, s * LANES:(s + 1) * LANES]
    for d, outs in ((4, (q4_ref, k4_ref, v4_ref)), (16, (q16_ref, k16_ref, v16_ref))):
        rows = nt // d
        for a, o_ref in enumerate(outs):
            for r in range(d):
                for s in range(nslab):
                    o_ref[r, :, s * LANES:(s + 1) * LANES] = (
                        slab_ref[a * nslab + s, pl.ds(r, rows, stride=d), :].astype(BF16))


def _in_proj(x2, g1, win, lng, lnb, ws_cat, bs_full, ga, cosf, sins, batch, seq):
    t_total = x2.shape[0]
    nt = IN_BLOCK
    nb = seq // nt
    grid = (t_total // nt,)
    row = lambda i: (i, 0)
    const2 = lambda i: (0, 0)
    tok_bf = jax.ShapeDtypeStruct((t_total, D_B), BF16)
    out_shape = (
        jax.ShapeDtypeStruct((t_total, D_A), BF16),
        tok_bf, tok_bf, tok_bf,
        *(jax.ShapeDtypeStruct((batch, 4, seq // 4, D_B), BF16),) * 3,
        *(jax.ShapeDtypeStruct((batch, 16, seq // 16, D_B), BF16),) * 3,
    )
    res4 = pl.BlockSpec((None, 4, nt // 4, D_B), lambda i: (i // nb, 0, i % nb, 0))
    res16 = pl.BlockSpec((None, 16, nt // 16, D_B), lambda i: (i // nb, 0, i % nb, 0))
    tok_spec = pl.BlockSpec((nt, D_B), row)
    return pl.pallas_call(
        _in_proj_kernel,
        grid=grid,
        in_specs=[
            pl.BlockSpec((nt, D_MODEL), row),
            pl.BlockSpec((1, D_MODEL), const2),
            pl.BlockSpec((D_MODEL, D_IN), const2),
            pl.BlockSpec((1, D_A), const2),
            pl.BlockSpec((1, D_A), const2),
            pl.BlockSpec((A_GROUPS // 2, CHUNK, 2 * CHUNK), lambda i: (0, 0, 0)),
            pl.BlockSpec((CHUNK, D_A), const2),
            pl.BlockSpec((1, D_A), const2),
            pl.BlockSpec((nt, D_B), lambda i: (i % nb, 0)),
            pl.BlockSpec((nt, D_B), lambda i: (i % nb, 0)),
        ],
        out_specs=(pl.BlockSpec((nt, D_A), row), tok_spec, tok_spec, tok_spec,
                   res4, res4, res4, res16, res16, res16),
        out_shape=out_shape,
        scratch_shapes=[pltpu.VMEM((3 * D_B // LANES, nt, LANES), F32)],
        compiler_params=pltpu.CompilerParams(
            dimension_semantics=("arbitrary",), vmem_limit_bytes=VMEM_LIMIT),
        name="in_proj",
    )(x2, g1, win, lng, lnb, ws_cat, bs_full, ga, cosf, sins)


def _attn_kernel(q1_ref, k1_ref, v1_ref, q4_ref, k4_ref, v4_ref, q16_ref, k16_ref, v16_ref,
                 o_ref, acc_ref, m_ref, l_ref):
    seq = o_ref.shape[0]
    lane = lax.broadcasted_iota(I32, (QBLK, LANES), 1)
    head0 = lane < HEAD_DIM
    branches = ((1, q1_ref, k1_ref, v1_ref), (4, q4_ref, k4_ref, v4_ref),
                (16, q16_ref, k16_ref, v16_ref))
    for bi, (d, q_ref, k_ref, v_ref) in enumerate(branches):
        length = seq // d
        nblk = length // QBLK
        win = min(2 * QBLK, length)
        diff = (lax.broadcasted_iota(I32, (QBLK, win), 1)
                - lax.broadcasted_iota(I32, (QBLK, win), 0))

        def block(blk, carry, d=d, bi=bi, q_ref=q_ref, k_ref=k_ref, v_ref=v_ref,
                  length=length, nblk=nblk, win=win, diff=diff):
            r = blk // nblk
            i0 = pl.multiple_of((blk % nblk) * QBLK, QBLK)
            w0 = pl.multiple_of(jnp.clip(i0 - HALF_WINDOW, 0, length - win), HALF_WINDOW)
            qb = q_ref[r, pl.ds(i0, QBLK), :]
            kw = k_ref[r, pl.ds(w0, win), :]
            vw = v_ref[r, pl.ds(w0, win), :]
            rel = diff + (w0 - i0)
            valid = (rel >= -HALF_WINDOW) & (rel <= HALF_WINDOW)
            accs, ms, ls = [], [], []
            for hsel in (head0, ~head0):
                qh = jnp.where(hsel, qb, jnp.zeros_like(qb))
                s = lax.dot_general(qh, kw, (((1,), (1,)), ((), ())),
                                    preferred_element_type=F32)
                s = jnp.where(valid, s, NEG_BIG)
                m = jnp.max(s, axis=1, keepdims=True)
                p = jnp.exp(s - m)
                ls.append(jnp.sum(p, axis=1, keepdims=True))
                ms.append(m)
                accs.append(jnp.dot(p.astype(BF16), vw, preferred_element_type=F32))
            acc = jnp.where(head0, accs[0], accs[1])
            mm = jnp.where(head0, ms[0], ms[1])
            ll = jnp.where(head0, ls[0], ls[1])
            if d == 1:
                rows = pl.ds(i0, QBLK)
            else:
                rows = pl.ds(i0 * d + r, QBLK, stride=d)
            acc_ref[bi, rows, :] = acc
            m_ref[bi, rows, :] = mm
            l_ref[bi, rows, :] = ll
            return carry

        lax.fori_loop(0, d * nblk, block, 0)

    def merge(c, carry):
        rows = pl.ds(pl.multiple_of(c * QBLK, QBLK), QBLK)
        m1, m2, m3 = m_ref[0, rows, :], m_ref[1, rows, :], m_ref[2, rows, :]
        mx = jnp.maximum(jnp.maximum(m1, m2), m3)
        w1, w2, w3 = jnp.exp(m1 - mx), jnp.exp(m2 - mx), jnp.exp(m3 - mx)
        num = w1 * acc_ref[0, rows, :] + w2 * acc_ref[1, rows, :] + w3 * acc_ref[2, rows, :]
        den = w1 * l_ref[0, rows, :] + w2 * l_ref[1, rows, :] + w3 * l_ref[2, rows, :]
        o_ref[rows, :] = num / den
        return carry

    lax.fori_loop(0, seq // QBLK, merge, 0)


def _attention(q1, k1, v1, q4, k4, v4, q16, k16, v16, batch, seq):
    npair = D_B // LANES
    nat = pl.BlockSpec((None, 1, seq, LANES), lambda b, p: (b, 0, 0, p))
    r4 = pl.BlockSpec((None, 4, seq // 4, LANES), lambda b, p: (b, 0, 0, p))
    r16 = pl.BlockSpec((None, 16, seq // 16, LANES), lambda b, p: (b, 0, 0, p))
    q1, k1, v1 = (t.reshape(batch, 1, seq, D_B) for t in (q1, k1, v1))
    return pl.pallas_call(
        _attn_kernel,
        grid=(batch, npair),
        in_specs=[nat, nat, nat, r4, r4, r4, r16, r16, r16],
        out_specs=pl.BlockSpec((None, seq, LANES), lambda b, p: (b, 0, p)),
        out_shape=jax.ShapeDtypeStruct((batch, seq, D_B), F32),
        scratch_shapes=[pltpu.VMEM((3, seq, LANES), F32)] * 3,
        compiler_params=pltpu.CompilerParams(
            dimension_semantics=("arbitrary", "arbitrary"), vmem_limit_bytes=VMEM_LIMIT),
        name="dilated_attn",
    )(q1, k1, v1, q4, k4, v4, q16, k16, v16)


def _topk_rows(s, k):
    n = s.shape[0]
    iota = lax.broadcasted_iota(I32, s.shape, 0)
    vals, idxs = [], []
    for _ in range(k):
        m = jnp.max(s, axis=0, keepdims=True)
        i = jnp.min(jnp.where(s == m, iota, n), axis=0, keepdims=True)
        vals.append(m)
        idxs.append(i)
        s = jnp.where(iota == i, -jnp.inf, s)
    return jnp.concatenate(vals, axis=0), jnp.concatenate(idxs, axis=0)


def _take_rows(table, sel):
    out = jnp.zeros(sel.shape, table.dtype)
    for a in range(table.shape[0]):
        out = jnp.where(sel == a, table[a:a + 1, :], out)
    return out


def _mid_kernel(x_ref, an_ref, bo_ref, gb_ref, wout_ref, g2_ref, wq_ref, keys_ref,
                x1_ref, xn_ref, idx_ref, gate_ref):
    nt = x_ref.shape[0]
    bn = _rms(bo_ref[...], gb_ref[...]).astype(BF16)
    x1 = (x_ref[...]
          + jnp.dot(an_ref[...], wout_ref[:D_A, :], preferred_element_type=F32)
          + jnp.dot(bn, wout_ref[D_A:, :], preferred_element_type=F32))
    x1_ref[...] = x1
    xn = _rms(x1, g2_ref[...])
    xn_ref[...] = xn
    q = jnp.dot(xn.astype(BF16), wq_ref[...], preferred_element_type=F32).astype(BF16)
    keys = (keys_ref[0], keys_ref[1])
    half = D_KEY // 2
    for c in range(nt // LANES):
        qc = q[c * LANES:(c + 1) * LANES, :]
        experts, gates = [], []
        for h in range(PEER_HEADS):
            tops = []
            for p in range(2):
                qhp = qc[:, (2 * h + p) * half:(2 * h + p + 1) * half]
                s = lax.dot_general(keys[p], qhp, (((1,), (1,)), ((), ())),
                                    preferred_element_type=F32)
                tops.append(_topk_rows(s, PEER_TOPK))
            (s1, i1), (s2, i2) = tops
            cand = jnp.concatenate([s1[a:a + 1, :] + s2 for a in range(PEER_TOPK)], axis=0)
            sc, ci = _topk_rows(cand, PEER_TOPK)
            e = (_take_rows(i1, ci >> 4) * N_KEYS + _take_rows(i2, ci & (PEER_TOPK - 1)))
            ex = jnp.exp(sc - sc[0:1, :])
            gates.append(ex / jnp.sum(ex, axis=0, keepdims=True))
            experts.append(e)
        idx_ref[c] = jnp.concatenate(experts, axis=0).T
        gate_ref[c * LANES:(c + 1) * LANES, :] = jnp.concatenate(gates, axis=0).T


def _mid(x2, an, bo, gb, wout, g2, wq, keys):
    t_total = x2.shape[0]
    nt = MID_BLOCK
    row = lambda i: (i, 0)
    const2 = lambda i: (0, 0)
    return pl.pallas_call(
        _mid_kernel,
        grid=(t_total // nt,),
        in_specs=[
            pl.BlockSpec((nt, D_MODEL), row),
            pl.BlockSpec((nt, D_A), row),
            pl.BlockSpec((nt, D_B), row),
            pl.BlockSpec((1, D_B), const2),
            pl.BlockSpec((D_MODEL, D_MODEL), const2),
            pl.BlockSpec((1, D_MODEL), const2),
            pl.BlockSpec((D_MODEL, PEER_HEADS * D_KEY), const2),
            pl.BlockSpec((2, N_KEYS, D_KEY // 2), lambda i: (0, 0, 0)),
        ],
        out_specs=(
            pl.BlockSpec((nt, D_MODEL), row),
            pl.BlockSpec((nt, D_MODEL), row),
            pl.BlockSpec((nt // LANES, LANES, N_SLOTS), lambda i: (i, 0, 0)),
            pl.BlockSpec((nt, N_SLOTS), row),
        ),
        out_shape=(
            jax.ShapeDtypeStruct((t_total, D_MODEL), F32),
            jax.ShapeDtypeStruct((t_total, D_MODEL), F32),
            jax.ShapeDtypeStruct((t_total // LANES, LANES, N_SLOTS), I32),
            jax.ShapeDtypeStruct((t_total, N_SLOTS), F32),
        ),
        compiler_params=pltpu.CompilerParams(
            dimension_semantics=("arbitrary",), vmem_limit_bytes=VMEM_LIMIT),
        name="mid",
    )(x2, an, bo, gb, wout, g2, wq, keys)


def _peer_kernel(idx_ref, gate_ref, xn_ref, x1_ref, gf_ref, uv_ref, y_ref, *scratch):
    rows_refs = scratch[:PEER_RING]
    bf_ref, sem_ref = scratch[PEER_RING:]
    nt = xn_ref.shape[0]
    wide = 2 * LANES
    nw = 2 * D_MODEL // wide

    def issue(t, slot, lo=0, hi=N_SLOTS):
        for s in range(lo, hi):
            pltpu.make_async_copy(uv_ref.at[idx_ref[t, s]],
                                  rows_refs[slot].at[s // SUB, :, s % SUB, :],
                                  sem_ref.at[slot]).start(priority=s % 2)

    def wait(slot):
        pltpu.make_async_copy(uv_ref.at[pl.ds(0, N_SLOTS)],
                              rows_refs[slot].reshape(N_SLOTS, NCH, LANES), sem_ref.at[slot]).wait()

    def stage(slot, k):
        for j in range(NCH):
            w = rows_refs[slot][:, j, :, :].reshape(N_SLOTS, LANES)
            bf_ref[k, :, j * LANES:(j + 1) * LANES] = (
                pltpu.bitcast(w & U_HALF, F32).astype(BF16))
            bf_ref[k, :, D_MODEL + j * LANES:D_MODEL + (j + 1) * LANES] = (
                pltpu.bitcast(w << 16, F32).astype(BF16))

    def pair(t0, slots, prefetch):
        for k in range(2):
            wait(slots[k])
            stage(slots[k], k)
        nbatch = 2 * nw
        per = N_SLOTS // (nbatch // 2)
        batches = [(k, b * per, (b + 1) * per) for b in range(nbatch // 2) for k in range(2)]

        def next_batch():
            if prefetch and batches:
                k, lo, hi = batches.pop(0)
                issue(t0 + PEER_RING + k, slots[k], lo, hi)

        x8 = [jnp.broadcast_to(xn_ref[pl.ds(t0 + k, 1), :], (SUB, D_MODEL)).astype(BF16)
              for k in range(2)]
        act = [jnp.zeros((SUB, N_SLOTS), F32) for _ in range(2)]
        for j in range(nw // 2):
            for k in range(2):
                next_batch()
                act[k] = act[k] + lax.dot_general(
                    x8[k][:, j * wide:(j + 1) * wide], bf_ref[k, :, j * wide:(j + 1) * wide],
                    (((1,), (1,)), ((), ())), preferred_element_type=F32)
        w = [(_gelu(act[k]) * gate_ref[pl.ds(t0 + k, 1), :]).astype(BF16) for k in range(2)]
        outs = [[], []]
        for j in range(nw // 2):
            for k in range(2):
                next_batch()
                outs[k].append(jnp.dot(
                    w[k], bf_ref[k, :, D_MODEL + j * wide:D_MODEL + (j + 1) * wide],
                    preferred_element_type=F32)[0:1, :])
        for k in range(2):
            out = jnp.concatenate(outs[k], axis=1)
            y_ref[pl.ds(t0 + k, 1), :] = _rms(x1_ref[pl.ds(t0 + k, 1), :] + out, gf_ref[...])

    def group(g, prefetch):
        for p in range(PEER_RING // 2):
            pair(g * PEER_RING + 2 * p, (2 * p, 2 * p + 1), prefetch)

    for t in range(PEER_RING):
        issue(t, t)
    ngroup = nt // PEER_RING
    lax.fori_loop(0, ngroup - 1, lambda g, c: (group(g, True), c)[1], 0)
    group(ngroup - 1, False)


def _peer(idx, gates, xn, x1, gf, uv, first_token):
    t_total = xn.shape[0] - first_token
    nt = PEER_BLOCK
    b0 = first_token // nt
    row = lambda i: (i + b0, 0)
    return pl.pallas_call(
        _peer_kernel,
        grid=(t_total // nt,),
        in_specs=[
            pl.BlockSpec((None, nt, N_SLOTS), lambda i: (i + b0, 0, 0), memory_space=pltpu.SMEM),
            pl.BlockSpec((nt, N_SLOTS), row),
            pl.BlockSpec((nt, D_MODEL), row),
            pl.BlockSpec((nt, D_MODEL), row),
            pl.BlockSpec((1, D_MODEL), lambda i: (0, 0)),
            pl.BlockSpec(memory_space=pl.ANY),
        ],
        out_specs=pl.BlockSpec((nt, D_MODEL), lambda i: (i, 0)),
        out_shape=jax.ShapeDtypeStruct((t_total, D_MODEL), F32),
        scratch_shapes=[pltpu.VMEM((N_SLOTS // SUB, NCH, SUB, LANES), I32)] * PEER_RING + [
            pltpu.VMEM((2, N_SLOTS, 2 * D_MODEL), BF16),
            pltpu.SemaphoreType.DMA((PEER_RING,))],
        compiler_params=pltpu.CompilerParams(
            dimension_semantics=("arbitrary",), vmem_limit_bytes=VMEM_LIMIT),
        name="peer",
    )(idx, gates, xn, x1, gf, uv)


def _sc_peer(idx, xn, gates, uv2, n_tokens):
    info = plsc.get_sparse_core_info()
    nc, lanes_n = info.num_cores, info.num_lanes
    nw = nc * info.num_subcores
    per = n_tokens // nw
    assert n_tokens % nw == 0
    nchunk = N_SLOTS // SC_ROWS
    qv = D_MODEL // (SC_PASSES * lanes_n)
    c0 = math.sqrt(2.0 / math.pi)
    mesh = plsc.VectorSubcoreMesh(core_axis_name="c", subcore_axis_name="s")

    @functools.partial(
        pl.kernel, mesh=mesh,
        out_type=jax.ShapeDtypeStruct((n_tokens, D_MODEL), F32),
        scratch_types=[
            pltpu.VMEM((nchunk, SC_ROWS), I32),
            pltpu.VMEM((D_MODEL,), F32),
            pltpu.VMEM((N_SLOTS,), F32),
            pltpu.VMEM((SC_ROWS, D_MODEL), I32),
            pltpu.VMEM((SC_ROWS, D_MODEL), I32),
            pltpu.VMEM((D_MODEL,), F32),
            pltpu.VMEM((SC_ROWS, lanes_n), F32),
            pltpu.VMEM((SC_ROWS,), F32),
            pltpu.SemaphoreType.DMA,
            pltpu.SemaphoreType.DMA,
        ],
        compiler_params=pltpu.CompilerParams(needs_layout_passes=False),
        name="peer_sc",
    )
    def sc_kernel(idx_hbm, xn_hbm, gate_hbm, uv_hbm, out_hbm, idx_v, x_v, g_v, rows0, rows1,
                  out_v, part_v, w_v, sem0, sem1):
        wid = lax.axis_index("s") * nc + lax.axis_index("c")
        lane_ids = lax.iota(I32, lanes_n)
        zero = jnp.zeros((lanes_n,), F32)
        bufs = ((rows0, sem0), (rows1, sem1))

        def vec(q, j):
            return pl.ds((q * qv + j) * lanes_n, lanes_n)

        def gather(c, b):
            rows, sem = bufs[b]
            return pltpu.make_async_copy(uv_hbm.at[idx_v.at[c]], rows, sem)

        def chunk(c, b):
            rows, _ = bufs[b]
            def ustep(j, acc):
                xj = x_v[pl.ds(j * lanes_n, lanes_n)]
                return tuple(
                    acc[r] + plsc.bitcast(rows[r, pl.ds(j * lanes_n, lanes_n)] & U_HALF, F32) * xj
                    for r in range(SC_ROWS))

            acc = lax.fori_loop(0, D_MODEL // lanes_n, ustep, (zero,) * SC_ROWS)
            for r in range(SC_ROWS):
                part_v[r, :] = acc[r]
            act = zero
            for l in range(lanes_n):
                act = act + plsc.load_gather(part_v, [lane_ids, jnp.full((lanes_n,), l, I32)])
            z = c0 * (act + 0.044715 * (act * act * act))
            tanh_z = 1.0 - 2.0 / (jnp.exp(2.0 * z) + 1.0)
            w_v[...] = 0.5 * act * (1.0 + tanh_z) * g_v[pl.ds(c * SC_ROWS, SC_ROWS)]
            for q in range(SC_PASSES):
                o = tuple(out_v[vec(q, j)] for j in range(qv))

                def vrow(r, o, q=q):
                    wr = plsc.load_gather(w_v, [jnp.full((lanes_n,), r, I32)])
                    return tuple(o[j] + wr * plsc.bitcast(rows[r, vec(q, j)] << 16, F32)
                                 for j in range(qv))

                o = lax.fori_loop(0, SC_ROWS, vrow, o)
                for j in range(qv):
                    out_v[vec(q, j)] = o[j]

        @pl.loop(0, per)
        def _(i):
            t = wid * per + i
            pltpu.sync_copy(idx_hbm.at[t], idx_v)
            pltpu.sync_copy(xn_hbm.at[t], x_v)
            pltpu.sync_copy(gate_hbm.at[t], g_v)
            for j in range(D_MODEL // lanes_n):
                out_v[pl.ds(j * lanes_n, lanes_n)] = zero
            gather(0, 0).start()

            @pl.loop(0, nchunk, step=2)
            def _(c):
                gather(c + 1, 1).start()
                gather(c, 0).wait()
                chunk(c, 0)

                @pl.when(c + 2 < nchunk)
                def _():
                    gather(c + 2, 0).start()

                gather(c + 1, 1).wait()
                chunk(c + 1, 1)

            pltpu.sync_copy(out_v, out_hbm.at[t])

    return sc_kernel(idx.reshape(-1, nchunk, SC_ROWS), xn, gates, uv2)


def _residual_norm_kernel(x1_ref, o_ref, gf_ref, y_ref):
    y_ref[...] = _rms(x1_ref[...] + o_ref[...], gf_ref[...])


def _residual_norm(x1, out, gf):
    n = out.shape[0]
    nt = PEER_BLOCK
    row = lambda i: (i, 0)
    return pl.pallas_call(
        _residual_norm_kernel,
        grid=(n // nt,),
        in_specs=[pl.BlockSpec((nt, D_MODEL), row), pl.BlockSpec((nt, D_MODEL), row),
                  pl.BlockSpec((1, D_MODEL), lambda i: (0, 0))],
        out_specs=pl.BlockSpec((nt, D_MODEL), row),
        out_shape=jax.ShapeDtypeStruct((n, D_MODEL), F32),
        compiler_params=pltpu.CompilerParams(dimension_semantics=("arbitrary",)),
        name="residual_norm",
    )(x1, out, gf)


def _rope_tables(seq):
    pos = jnp.arange(seq, dtype=F32)
    inv = 1.0 / (ROPE_THETA ** (jnp.arange(0, HEAD_DIM, 2, dtype=F32) / HEAD_DIM))
    ang = pos[:, None] * inv[None, :]
    cos, sin = jnp.cos(ang), jnp.sin(ang)
    cosf = jnp.tile(jnp.concatenate([cos, cos], axis=1), (1, B_HEADS))
    sins = jnp.tile(jnp.concatenate([-sin, sin], axis=1), (1, B_HEADS))
    return cosf, sins


def kernel(x, norm1_g, w_in, ln_v_g, ln_v_b, w_spatial, b_spatial, out_norm_a_g, out_norm_b_g,
           w_out, norm2_g, w_query, sub_keys, expert_u, expert_v, final_norm_g):
    batch, seq, _ = x.shape
    assert w_in.shape[0] == 1 and seq % (16 * QBLK) == 0 and seq % IN_BLOCK == 0
    row = lambda g: g.reshape(1, -1).astype(F32)

    ws = w_spatial[0].astype(BF16)
    ws_cat = jnp.concatenate([ws[0::2], ws[1::2]], axis=2)
    bs_full = jnp.repeat(b_spatial[0].T, A_GROUP_DIM, axis=1)
    cosf, sins = _rope_tables(seq)
    win, wout, wq = w_in[0].astype(BF16), w_out[0].astype(BF16), w_query[0].astype(BF16)
    keys = sub_keys[0].astype(BF16)
    gf = row(final_norm_g)
    half = lambda t: lax.bitcast_convert_type(t.astype(BF16), jnp.uint16).astype(jnp.uint32)
    uv2 = lax.bitcast_convert_type((half(expert_u[0]) << 16) | half(expert_v[0]), I32)
    uv = uv2.reshape(-1, NCH, LANES)

    nchunk = PIPE_CHUNKS if batch % PIPE_CHUNKS == 0 else 1
    cb = batch // nchunk
    pieces = []
    for ci in range(nchunk):
        x2 = x[ci * cb:(ci + 1) * cb].reshape(cb * seq, D_MODEL)
        an, q1, k1, v1, q4, k4, v4, q16, k16, v16 = _in_proj(
            x2, row(norm1_g[0]), win, row(ln_v_g[0]), row(ln_v_b[0]),
            ws_cat, bs_full, row(out_norm_a_g[0]), cosf, sins, cb, seq)
        bo = _attention(q1, k1, v1, q4, k4, v4, q16, k16, v16, cb, seq).reshape(cb * seq, D_B)
        x1, xn, idx, gates = _mid(x2, an, bo, row(out_norm_b_g[0]), wout, row(norm2_g[0]), wq, keys)
        n_sc = (cb * seq) * SC_SHARE[0] // SC_SHARE[1]
        assert n_sc % PEER_BLOCK == 0
        out_sc = _sc_peer(idx.reshape(cb * seq, N_SLOTS), xn, gates, uv2, n_sc)
        y_tc = _peer(idx, gates, xn, x1, gf, uv, n_sc)
        pieces += [_residual_norm(x1, out_sc, gf), y_tc]
    return jnp.concatenate(pieces, axis=0).reshape(batch, seq, D_MODEL)
```

```python
import functools
import math

import jax
import jax.numpy as jnp
from jax import lax
from jax.experimental import pallas as pl
from jax.experimental.pallas import tpu as pltpu
from jax.experimental.pallas import tpu_sc as plsc

F32 = jnp.float32
BF16 = jnp.bfloat16
I32 = jnp.int32

D_MODEL = 1024
D_A = 512
D_B = 512
A_GROUPS = 8
A_GROUP_DIM = 64
CHUNK = 128
B_HEADS = 8
HEAD_DIM = 64
DILATIONS = (1, 4, 16)
HALF_WINDOW = 64
ROPE_THETA = 10000.0
D_IN = 2 * D_A + 3 * D_B
N_KEYS = 128
PEER_HEADS = 8
PEER_TOPK = 16
D_KEY = 256
N_SLOTS = PEER_HEADS * PEER_TOPK
EPS = 1e-6
NEG_BIG = -1e30

LANES = 128
SUB = 8
NCH = D_MODEL // LANES
U_HALF = -65536
QBLK = 128
IN_BLOCK = 512
MID_BLOCK = 256
PEER_BLOCK = 128
PEER_RING = 4
SC_SHARE = (5, 8)
PIPE_CHUNKS = 4
SC_ROWS = 16
SC_PASSES = 4
VMEM_LIMIT = 48 * 1024 * 1024


def _gelu(x):
    c = math.sqrt(2.0 / math.pi)
    return 0.5 * x * (1.0 + jnp.tanh(c * (x + 0.044715 * (x * x * x))))


def _rms(x, g):
    return x * lax.rsqrt(jnp.mean(x * x, axis=-1, keepdims=True) + EPS) * g


def _in_proj_kernel(x_ref, g1_ref, win_ref, lng_ref, lnb_ref, ws_ref, bs_ref, ga_ref,
                    cos_ref, sin_ref,
                    an_ref, q1_ref, k1_ref, v1_ref, q4_ref, k4_ref, v4_ref,
                    q16_ref, k16_ref, v16_ref, slab_ref):
    nt = x_ref.shape[0]
    h = _rms(x_ref[...], g1_ref[...]).astype(BF16)
    proj = jnp.dot(h, win_ref[...], preferred_element_type=F32)

    u = _gelu(proj[:, :D_A])
    v = _gelu(proj[:, D_A:2 * D_A])
    mu = jnp.mean(v, axis=-1, keepdims=True)
    vc = v - mu
    var = jnp.mean(vc * vc, axis=-1, keepdims=True)
    vln = (vc * lax.rsqrt(var + EPS) * lng_ref[...] + lnb_ref[...]).astype(BF16)
    lane = lax.broadcasted_iota(I32, (CHUNK, LANES), 1)
    lo = lane < A_GROUP_DIM
    zero = jnp.zeros((CHUNK, LANES), BF16)
    chunks = []
    for c in range(nt // CHUNK):
        cols = []
        for j in range(A_GROUPS // 2):
            vv = vln[c * CHUNK:(c + 1) * CHUNK, j * LANES:(j + 1) * LANES]
            rhs = jnp.concatenate([jnp.where(lo, vv, zero), jnp.where(lo, zero, vv)], axis=0)
            cols.append(jnp.dot(ws_ref[j], rhs, preferred_element_type=F32))
        chunks.append(jnp.concatenate(cols, axis=1) + bs_ref[...])
    mixed = jnp.concatenate(chunks, axis=0)
    an_ref[...] = _rms(u * mixed, ga_ref[...]).astype(BF16)

    cosf = cos_ref[...]
    sins = sin_ref[...]
    lane_b = lax.broadcasted_iota(I32, (nt, D_B), 1)
    first_half = (lane_b % HEAD_DIM) < (HEAD_DIM // 2)

    def rope(t):
        partner = jnp.where(first_half, pltpu.roll(t, D_B - HEAD_DIM // 2, 1),
                            pltpu.roll(t, HEAD_DIM // 2, 1))
        return t * cosf + partner * sins

    q = rope(proj[:, 2 * D_A:2 * D_A + D_B]) * (HEAD_DIM ** -0.5)
    k = rope(proj[:, 2 * D_A + D_B:2 * D_A + 2 * D_B])
    vv = proj[:, 2 * D_A + 2 * D_B:]
    q1_ref[...] = q.astype(BF16)
    k1_ref[...] = k.astype(BF16)
    v1_ref[...] = vv.astype(BF16)

    nslab = D_B // LANES
    for a, t in enumerate((q, k, vv)):
        for s in range(nslab):
            slab_ref[a * nslab + s] = t[:, s * LANES:(s + 1) * LANES]
    for d, outs in ((4, (q4_ref, k4_ref, v4_ref)), (16, (q16_ref, k16_ref, v16_ref))):
        rows = nt // d
        for a, o_ref in enumerate(outs):
            for r in range(d):
                for s in range(nslab):
                    o_ref[r, :, s * LANES:(s + 1) * LANES] = (
                        slab_ref[a * nslab + s, pl.ds(r, rows, stride=d), :].astype(BF16))


def _in_proj(x2, g1, win, lng, lnb, ws_cat, bs_full, ga, cosf, sins, batch, seq):
    t_total = x2.shape[0]
    nt = IN_BLOCK
    nb = seq // nt
    grid = (t_total // nt,)
    row = lambda i: (i, 0)
    const2 = lambda i: (0, 0)
    tok_bf = jax.ShapeDtypeStruct((t_total, D_B), BF16)
    out_shape = (
        jax.ShapeDtypeStruct((t_total, D_A), BF16),
        tok_bf, tok_bf, tok_bf,
        *(jax.ShapeDtypeStruct((batch, 4, seq // 4, D_B), BF16),) * 3,
        *(jax.ShapeDtypeStruct((batch, 16, seq // 16, D_B), BF16),) * 3,
    )
    res4 = pl.BlockSpec((None, 4, nt // 4, D_B), lambda i: (i // nb, 0, i % nb, 0))
    res16 = pl.BlockSpec((None, 16, nt // 16, D_B), lambda i: (i // nb, 0, i % nb, 0))
    tok_spec = pl.BlockSpec((nt, D_B), row)
    return pl.pallas_call(
        _in_proj_kernel,
        grid=grid,
        in_specs=[
            pl.BlockSpec((nt, D_MODEL), row),
            pl.BlockSpec((1, D_MODEL), const2),
            pl.BlockSpec((D_MODEL, D_IN), const2),
            pl.BlockSpec((1, D_A), const2),
            pl.BlockSpec((1, D_A), const2),
            pl.BlockSpec((A_GROUPS // 2, CHUNK, 2 * CHUNK), lambda i: (0, 0, 0)),
            pl.BlockSpec((CHUNK, D_A), const2),
            pl.BlockSpec((1, D_A), const2),
            pl.BlockSpec((nt, D_B), lambda i: (i % nb, 0)),
            pl.BlockSpec((nt, D_B), lambda i: (i % nb, 0)),
        ],
        out_specs=(pl.BlockSpec((nt, D_A), row), tok_spec, tok_spec, tok_spec,
                   res4, res4, res4, res16, res16, res16),
        out_shape=out_shape,
        scratch_shapes=[pltpu.VMEM((3 * D_B // LANES, nt, LANES), F32)],
        compiler_params=pltpu.CompilerParams(
            dimension_semantics=("arbitrary",), vmem_limit_bytes=VMEM_LIMIT),
        name="in_proj",
    )(x2, g1, win, lng, lnb, ws_cat, bs_full, ga, cosf, sins)


def _attn_kernel(q1_ref, k1_ref, v1_ref, q4_ref, k4_ref, v4_ref, q16_ref, k16_ref, v16_ref,
                 o_ref, acc_ref, m_ref, l_ref):
    seq = o_ref.shape[0]
    lane = lax.broadcasted_iota(I32, (QBLK, LANES), 1)
    head0 = lane < HEAD_DIM
    branches = ((1, q1_ref, k1_ref, v1_ref), (4, q4_ref, k4_ref, v4_ref),
                (16, q16_ref, k16_ref, v16_ref))
    for bi, (d, q_ref, k_ref, v_ref) in enumerate(branches):
        length = seq // d
        nblk = length // QBLK
        win = min(2 * QBLK, length)
        diff = (lax.broadcasted_iota(I32, (QBLK, win), 1)
                - lax.broadcasted_iota(I32, (QBLK, win), 0))

        def block(blk, carry, d=d, bi=bi, q_ref=q_ref, k_ref=k_ref, v_ref=v_ref,
                  length=length, nblk=nblk, win=win, diff=diff):
            r = blk // nblk
            i0 = pl.multiple_of((blk % nblk) * QBLK, QBLK)
            w0 = pl.multiple_of(jnp.clip(i0 - HALF_WINDOW, 0, length - win), HALF_WINDOW)
            qb = q_ref[r, pl.ds(i0, QBLK), :]
            kw = k_ref[r, pl.ds(w0, win), :]
            vw = v_ref[r, pl.ds(w0, win), :]
            rel = diff + (w0 - i0)
            valid = (rel >= -HALF_WINDOW) & (rel <= HALF_WINDOW)
            accs, ms, ls = [], [], []
            for hsel in (head0, ~head0):
                qh = jnp.where(hsel, qb, jnp.zeros_like(qb))
                s = lax.dot_general(qh, kw, (((1,), (1,)), ((), ())),
                                    preferred_element_type=F32)
                s = jnp.where(valid, s, NEG_BIG)
                m = jnp.max(s, axis=1, keepdims=True)
                p = jnp.exp(s - m)
                ls.append(jnp.sum(p, axis=1, keepdims=True))
                ms.append(m)
                accs.append(jnp.dot(p.astype(BF16), vw, preferred_element_type=F32))
            acc = jnp.where(head0, accs[0], accs[1])
            mm = jnp.where(head0, ms[0], ms[1])
            ll = jnp.where(head0, ls[0], ls[1])
            if d == 1:
                rows = pl.ds(i0, QBLK)
            else:
                rows = pl.ds(i0 * d + r, QBLK, stride=d)
            acc_ref[bi, rows, :] = acc
            m_ref[bi, rows, :] = mm
            l_ref[bi, rows, :] = ll
            return carry

        lax.fori_loop(0, d * nblk, block, 0)

    def merge(c, carry):
        rows = pl.ds(pl.multiple_of(c * QBLK, QBLK), QBLK)
        m1, m2, m3 = m_ref[0, rows, :], m_ref[1, rows, :], m_ref[2, rows, :]
        mx = jnp.maximum(jnp.maximum(m1, m2), m3)
        w1, w2, w3 = jnp.exp(m1 - mx), jnp.exp(m2 - mx), jnp.exp(m3 - mx)
        num = w1 * acc_ref[0, rows, :] + w2 * acc_ref[1, rows, :] + w3 * acc_ref[2, rows, :]
        den = w1 * l_ref[0, rows, :] + w2 * l_ref[1, rows, :] + w3 * l_ref[2, rows, :]
        o_ref[rows, :] = num / den
        return carry

    lax.fori_loop(0, seq // QBLK, merge, 0)


def _attention(q1, k1, v1, q4, k4, v4, q16, k16, v16, batch, seq):
    npair = D_B // LANES
    nat = pl.BlockSpec((None, 1, seq, LANES), lambda b, p: (b, 0, 0, p))
    r4 = pl.BlockSpec((None, 4, seq // 4, LANES), lambda b, p: (b, 0, 0, p))
    r16 = pl.BlockSpec((None, 16, seq // 16, LANES), lambda b, p: (b, 0, 0, p))
    q1, k1, v1 = (t.reshape(batch, 1, seq, D_B) for t in (q1, k1, v1))
    return pl.pallas_call(
        _attn_kernel,
        grid=(batch, npair),
        in_specs=[nat, nat, nat, r4, r4, r4, r16, r16, r16],
        out_specs=pl.BlockSpec((None, seq, LANES), lambda b, p: (b, 0, p)),
        out_shape=jax.ShapeDtypeStruct((batch, seq, D_B), F32),
        scratch_shapes=[pltpu.VMEM((3, seq, LANES), F32)] * 3,
        compiler_params=pltpu.CompilerParams(
            dimension_semantics=("arbitrary", "arbitrary"), vmem_limit_bytes=VMEM_LIMIT),
        name="dilated_attn",
    )(q1, k1, v1, q4, k4, v4, q16, k16, v16)


def _topk_rows(s, k):
    n = s.shape[0]
    iota = lax.broadcasted_iota(I32, s.shape, 0).astype(F32)
    vals, idxs = [], []
    for _ in range(k):
        m = jnp.max(s, axis=0, keepdims=True)
        i = jnp.min(jnp.where(s == m, iota, float(n)), axis=0, keepdims=True)
        vals.append(m)
        idxs.append(i)
        s = jnp.where(iota == i, -jnp.inf, s)
    return jnp.concatenate(vals, axis=0), jnp.concatenate(idxs, axis=0).astype(I32)


def _take_rows(table, sel):
    out = jnp.zeros(sel.shape, table.dtype)
    for a in range(table.shape[0]):
        out = jnp.where(sel == a, table[a:a + 1, :], out)
    return out


def _mid_kernel(x_ref, an_ref, bo_ref, gb_ref, wout_ref, g2_ref, wq_ref, keys_ref,
                x1_ref, xn_ref, idx_ref, gate_ref):
    nt = x_ref.shape[0]
    bn = _rms(bo_ref[...], gb_ref[...]).astype(BF16)
    x1 = (x_ref[...]
          + jnp.dot(an_ref[...], wout_ref[:D_A, :], preferred_element_type=F32)
          + jnp.dot(bn, wout_ref[D_A:, :], preferred_element_type=F32))
    x1_ref[...] = x1
    xn = _rms(x1, g2_ref[...])
    xn_ref[...] = xn
    q = jnp.dot(xn.astype(BF16), wq_ref[...], preferred_element_type=F32).astype(BF16)
    keys = (keys_ref[0], keys_ref[1])
    half = D_KEY // 2
    for c in range(nt // LANES):
        qc = q[c * LANES:(c + 1) * LANES, :]
        experts, gates = [], []
        for h in range(PEER_HEADS):
            tops = []
            for p in range(2):
                qhp = qc[:, (2 * h + p) * half:(2 * h + p + 1) * half]
                s = lax.dot_general(keys[p], qhp, (((1,), (1,)), ((), ())),
                                    preferred_element_type=F32)
                tops.append(_topk_rows(s, PEER_TOPK))
            (s1, i1), (s2, i2) = tops
            cand = jnp.concatenate(
                [s1[0:1, :] + s2]
                + [s1[a:a + 1, :] + s2[0:SUB, :] for a in range(1, SUB)]
                + [s1[SUB:, :] + s2[0:1, :]], axis=0)
            sc, pos = _topk_rows(cand, PEER_TOPK)
            ca = jnp.where(pos < PEER_TOPK, 0,
                           jnp.where(pos < PEER_TOPK + SUB * (SUB - 1), (pos >> 3) - 1, pos - SUB * SUB))
            cb = jnp.where(pos < PEER_TOPK, pos,
                           jnp.where(pos < PEER_TOPK + SUB * (SUB - 1), pos & (SUB - 1), 0))
            e = _take_rows(i1, ca) * N_KEYS + _take_rows(i2, cb)
            ex = jnp.exp(sc - sc[0:1, :])
            gates.append(ex / jnp.sum(ex, axis=0, keepdims=True))
            experts.append(e)
        idx_ref[c] = jnp.concatenate(experts, axis=0).T
        gate_ref[c * LANES:(c + 1) * LANES, :] = jnp.concatenate(gates, axis=0).T


def _mid(x2, an, bo, gb, wout, g2, wq, keys):
    t_total = x2.shape[0]
    nt = MID_BLOCK
    row = lambda i: (i, 0)
    const2 = lambda i: (0, 0)
    return pl.pallas_call(
        _mid_kernel,
        grid=(t_total // nt,),
        in_specs=[
            pl.BlockSpec((nt, D_MODEL), row),
            pl.BlockSpec((nt, D_A), row),
            pl.BlockSpec((nt, D_B), row),
            pl.BlockSpec((1, D_B), const2),
            pl.BlockSpec((D_MODEL, D_MODEL), const2),
            pl.BlockSpec((1, D_MODEL), const2),
            pl.BlockSpec((D_MODEL, PEER_HEADS * D_KEY), const2),
            pl.BlockSpec((2, N_KEYS, D_KEY // 2), lambda i: (0, 0, 0)),
        ],
        out_specs=(
            pl.BlockSpec((nt, D_MODEL), row),
            pl.BlockSpec((nt, D_MODEL), row),
            pl.BlockSpec((nt // LANES, LANES, N_SLOTS), lambda i: (i, 0, 0)),
            pl.BlockSpec((nt, N_SLOTS), row),
        ),
        out_shape=(
            jax.ShapeDtypeStruct((t_total, D_MODEL), F32),
            jax.ShapeDtypeStruct((t_total, D_MODEL), F32),
            jax.ShapeDtypeStruct((t_total // LANES, LANES, N_SLOTS), I32),
            jax.ShapeDtypeStruct((t_total, N_SLOTS), F32),
        ),
        compiler_params=pltpu.CompilerParams(
            dimension_semantics=("arbitrary",), vmem_limit_bytes=VMEM_LIMIT),
        name="mid",
    )(x2, an, bo, gb, wout, g2, wq, keys)


def _peer_kernel(idx_ref, gate_ref, xn_ref, x1_ref, gf_ref, uv_ref, y_ref, *scratch):
    rows_refs = scratch[:PEER_RING]
    bf_ref, sem_ref = scratch[PEER_RING:]
    nt = xn_ref.shape[0]
    wide = 2 * LANES
    nw = 2 * D_MODEL // wide

    def issue(t, slot, lo=0, hi=N_SLOTS):
        for s in range(lo, hi):
            pltpu.make_async_copy(uv_ref.at[idx_ref[t, s]],
                                  rows_refs[slot].at[s // SUB, :, s % SUB, :],
                                  sem_ref.at[slot]).start(priority=s % 2)

    def wait(slot):
        pltpu.make_async_copy(uv_ref.at[pl.ds(0, N_SLOTS)],
                              rows_refs[slot].reshape(N_SLOTS, NCH, LANES), sem_ref.at[slot]).wait()

    def stage(slot, k):
        for j in range(NCH):
            w = rows_refs[slot][:, j, :, :].reshape(N_SLOTS, LANES)
            bf_ref[k, :, j * LANES:(j + 1) * LANES] = (
                pltpu.bitcast(w & U_HALF, F32).astype(BF16))
            bf_ref[k, :, D_MODEL + j * LANES:D_MODEL + (j + 1) * LANES] = (
                pltpu.bitcast(w << 16, F32).astype(BF16))

    def pair(t0, slots, prefetch):
        for k in range(2):
            wait(slots[k])
            stage(slots[k], k)
        nbatch = 2 * nw
        per = N_SLOTS // (nbatch // 2)
        batches = [(k, b * per, (b + 1) * per) for b in range(nbatch // 2) for k in range(2)]

        def next_batch():
            if prefetch and batches:
                k, lo, hi = batches.pop(0)
                issue(t0 + PEER_RING + k, slots[k], lo, hi)

        x8 = [jnp.broadcast_to(xn_ref[pl.ds(t0 + k, 1), :], (SUB, D_MODEL)).astype(BF16)
              for k in range(2)]
        act = [jnp.zeros((SUB, N_SLOTS), F32) for _ in range(2)]
        for j in range(nw // 2):
            for k in range(2):
                next_batch()
                act[k] = act[k] + lax.dot_general(
                    x8[k][:, j * wide:(j + 1) * wide], bf_ref[k, :, j * wide:(j + 1) * wide],
                    (((1,), (1,)), ((), ())), preferred_element_type=F32)
        w = [(_gelu(act[k]) * gate_ref[pl.ds(t0 + k, 1), :]).astype(BF16) for k in range(2)]
        outs = [[], []]
        for j in range(nw // 2):
            for k in range(2):
                next_batch()
                outs[k].append(jnp.dot(
                    w[k], bf_ref[k, :, D_MODEL + j * wide:D_MODEL + (j + 1) * wide],
                    preferred_element_type=F32)[0:1, :])
        for k in range(2):
            out = jnp.concatenate(outs[k], axis=1)
            y_ref[pl.ds(t0 + k, 1), :] = _rms(x1_ref[pl.ds(t0 + k, 1), :] + out, gf_ref[...])

    def group(g, prefetch):
        for p in range(PEER_RING // 2):
            pair(g * PEER_RING + 2 * p, (2 * p, 2 * p + 1), prefetch)

    for t in range(PEER_RING):
        issue(t, t)
    ngroup = nt // PEER_RING
    lax.fori_loop(0, ngroup - 1, lambda g, c: (group(g, True), c)[1], 0)
    group(ngroup - 1, False)


def _peer(idx, gates, xn, x1, gf, uv, first_token):
    t_total = xn.shape[0] - first_token
    nt = PEER_BLOCK
    b0 = first_token // nt
    row = lambda i: (i + b0, 0)
    return pl.pallas_call(
        _peer_kernel,
        grid=(t_total // nt,),
        in_specs=[
            pl.BlockSpec((None, nt, N_SLOTS), lambda i: (i + b0, 0, 0), memory_space=pltpu.SMEM),
            pl.BlockSpec((nt, N_SLOTS), row),
            pl.BlockSpec((nt, D_MODEL), row),
            pl.BlockSpec((nt, D_MODEL), row),
            pl.BlockSpec((1, D_MODEL), lambda i: (0, 0)),
            pl.BlockSpec(memory_space=pl.ANY),
        ],
        out_specs=pl.BlockSpec((nt, D_MODEL), lambda i: (i, 0)),
        out_shape=jax.ShapeDtypeStruct((t_total, D_MODEL), F32),
        scratch_shapes=[pltpu.VMEM((N_SLOTS // SUB, NCH, SUB, LANES), I32)] * PEER_RING + [
            pltpu.VMEM((2, N_SLOTS, 2 * D_MODEL), BF16),
            pltpu.SemaphoreType.DMA((PEER_RING,))],
        compiler_params=pltpu.CompilerParams(
            dimension_semantics=("arbitrary",), vmem_limit_bytes=VMEM_LIMIT),
        name="peer",
    )(idx, gates, xn, x1, gf, uv)


def _sc_peer(idx, xn, gates, uv2, n_tokens):
    info = plsc.get_sparse_core_info()
    nc, lanes_n = info.num_cores, info.num_lanes
    nw = nc * info.num_subcores
    per = n_tokens // nw
    assert n_tokens % nw == 0
    nchunk = N_SLOTS // SC_ROWS
    qv = D_MODEL // (SC_PASSES * lanes_n)
    c0 = math.sqrt(2.0 / math.pi)
    mesh = plsc.VectorSubcoreMesh(core_axis_name="c", subcore_axis_name="s")

    @functools.partial(
        pl.kernel, mesh=mesh,
        out_type=jax.ShapeDtypeStruct((n_tokens, D_MODEL), F32),
        scratch_types=[
            pltpu.VMEM((nchunk, SC_ROWS), I32),
            pltpu.VMEM((D_MODEL,), F32),
            pltpu.VMEM((N_SLOTS,), F32),
            pltpu.VMEM((SC_ROWS, D_MODEL), I32),
            pltpu.VMEM((SC_ROWS, D_MODEL), I32),
            pltpu.VMEM((D_MODEL,), F32),
            pltpu.VMEM((SC_ROWS, lanes_n), F32),
            pltpu.VMEM((SC_ROWS,), F32),
            pltpu.SemaphoreType.DMA,
            pltpu.SemaphoreType.DMA,
        ],
        compiler_params=pltpu.CompilerParams(needs_layout_passes=False),
        name="peer_sc",
    )
    def sc_kernel(idx_hbm, xn_hbm, gate_hbm, uv_hbm, out_hbm, idx_v, x_v, g_v, rows0, rows1,
                  out_v, part_v, w_v, sem0, sem1):
        wid = lax.axis_index("s") * nc + lax.axis_index("c")
        lane_ids = lax.iota(I32, lanes_n)
        zero = jnp.zeros((lanes_n,), F32)
        bufs = ((rows0, sem0), (rows1, sem1))

        def vec(q, j):
            return pl.ds((q * qv + j) * lanes_n, lanes_n)

        def gather(c, b):
            rows, sem = bufs[b]
            return pltpu.make_async_copy(uv_hbm.at[idx_v.at[c]], rows, sem)

        def chunk(c, b):
            rows, _ = bufs[b]
            def ustep(j, acc):
                xj = x_v[pl.ds(j * lanes_n, lanes_n)]
                return tuple(
                    acc[r] + plsc.bitcast(rows[r, pl.ds(j * lanes_n, lanes_n)] & U_HALF, F32) * xj
                    for r in range(SC_ROWS))

            acc = lax.fori_loop(0, D_MODEL // lanes_n, ustep, (zero,) * SC_ROWS)
            for r in range(SC_ROWS):
                part_v[r, :] = acc[r]
            act = zero
            for l in range(lanes_n):
                act = act + plsc.load_gather(part_v, [lane_ids, jnp.full((lanes_n,), l, I32)])
            z = c0 * (act + 0.044715 * (act * act * act))
            tanh_z = 1.0 - 2.0 / (jnp.exp(2.0 * z) + 1.0)
            w_v[...] = 0.5 * act * (1.0 + tanh_z) * g_v[pl.ds(c * SC_ROWS, SC_ROWS)]
            for q in range(SC_PASSES):
                o = tuple(out_v[vec(q, j)] for j in range(qv))

                def vrow(r, o, q=q):
                    wr = plsc.load_gather(w_v, [jnp.full((lanes_n,), r, I32)])
                    return tuple(o[j] + wr * plsc.bitcast(rows[r, vec(q, j)] << 16, F32)
                                 for j in range(qv))

                o = lax.fori_loop(0, SC_ROWS, vrow, o)
                for j in range(qv):
                    out_v[vec(q, j)] = o[j]

        @pl.loop(0, per)
        def _(i):
            t = wid * per + i
            pltpu.sync_copy(idx_hbm.at[t], idx_v)
            pltpu.sync_copy(xn_hbm.at[t], x_v)
            pltpu.sync_copy(gate_hbm.at[t], g_v)
            for j in range(D_MODEL // lanes_n):
                out_v[pl.ds(j * lanes_n, lanes_n)] = zero
            gather(0, 0).start()

            @pl.loop(0, nchunk, step=2)
            def _(c):
                gather(c + 1, 1).start()
                gather(c, 0).wait()
                chunk(c, 0)

                @pl.when(c + 2 < nchunk)
                def _():
                    gather(c + 2, 0).start()

                gather(c + 1, 1).wait()
                chunk(c + 1, 1)

            pltpu.sync_copy(out_v, out_hbm.at[t])

    return sc_kernel(idx.reshape(-1, nchunk, SC_ROWS), xn, gates, uv2)


def _residual_norm_kernel(x1_ref, o_ref, gf_ref, y_ref):
    y_ref[...] = _rms(x1_ref[...] + o_ref[...], gf_ref[...])


def _residual_norm(x1, out, gf):
    n = out.shape[0]
    nt = PEER_BLOCK
    row = lambda i: (i, 0)
    return pl.pallas_call(
        _residual_norm_kernel,
        grid=(n // nt,),
        in_specs=[pl.BlockSpec((nt, D_MODEL), row), pl.BlockSpec((nt, D_MODEL), row),
                  pl.BlockSpec((1, D_MODEL), lambda i: (0, 0))],
        out_specs=pl.BlockSpec((nt, D_MODEL), row),
        out_shape=jax.ShapeDtypeStruct((n, D_MODEL), F32),
        compiler_params=pltpu.CompilerParams(dimension_semantics=("arbitrary",)),
        name="residual_norm",
    )(x1, out, gf)


def _rope_tables(seq):
    pos = jnp.arange(seq, dtype=F32)
    inv = 1.0 / (ROPE_THETA ** (jnp.arange(0, HEAD_DIM, 2, dtype=F32) / HEAD_DIM))
    ang = pos[:, None] * inv[None, :]
    cos, sin = jnp.cos(ang), jnp.sin(ang)
    cosf = jnp.tile(jnp.concatenate([cos, cos], axis=1), (1, B_HEADS))
    sins = jnp.tile(jnp.concatenate([-sin, sin], axis=1), (1, B_HEADS))
    return cosf, sins


def kernel(x, norm1_g, w_in, ln_v_g, ln_v_b, w_spatial, b_spatial, out_norm_a_g, out_norm_b_g,
           w_out, norm2_g, w_query, sub_keys, expert_u, expert_v, final_norm_g):
    batch, seq, _ = x.shape
    assert w_in.shape[0] == 1 and seq % (16 * QBLK) == 0 and seq % IN_BLOCK == 0
    row = lambda g: g.reshape(1, -1).astype(F32)

    ws = w_spatial[0].astype(BF16)
    ws_cat = jnp.concatenate([ws[0::2], ws[1::2]], axis=2)
    bs_full = jnp.repeat(b_spatial[0].T, A_GROUP_DIM, axis=1)
    cosf, sins = _rope_tables(seq)
    win, wout, wq = w_in[0].astype(BF16), w_out[0].astype(BF16), w_query[0].astype(BF16)
    keys = sub_keys[0].astype(BF16)
    gf = row(final_norm_g)
    half = lambda t: lax.bitcast_convert_type(t.astype(BF16), jnp.uint16).astype(jnp.uint32)
    uv2 = lax.bitcast_convert_type((half(expert_u[0]) << 16) | half(expert_v[0]), I32)
    uv = uv2.reshape(-1, NCH, LANES)

    nchunk = PIPE_CHUNKS if batch % PIPE_CHUNKS == 0 else 1
    cb = batch // nchunk
    pieces = []
    for ci in range(nchunk):
        x2 = x[ci * cb:(ci + 1) * cb].reshape(cb * seq, D_MODEL)
        an, q1, k1, v1, q4, k4, v4, q16, k16, v16 = _in_proj(
            x2, row(norm1_g[0]), win, row(ln_v_g[0]), row(ln_v_b[0]),
            ws_cat, bs_full, row(out_norm_a_g[0]), cosf, sins, cb, seq)
        bo = _attention(q1, k1, v1, q4, k4, v4, q16, k16, v16, cb, seq).reshape(cb * seq, D_B)
        x1, xn, idx, gates = _mid(x2, an, bo, row(out_norm_b_g[0]), wout, row(norm2_g[0]), wq, keys)
        n_sc = (cb * seq) * SC_SHARE[0] // SC_SHARE[1]
        assert n_sc % PEER_BLOCK == 0
        out_sc = _sc_peer(idx.reshape(cb * seq, N_SLOTS), xn, gates, uv2, n_sc)
        y_tc = _peer(idx, gates, xn, x1, gf, uv, n_sc)
        pieces += [_residual_norm(x1, out_sc, gf), y_tc]
    return jnp.concatenate(pieces, axis=0).reshape(batch, seq, D_MODEL)
```

```python
import functools
import math

import jax
import jax.numpy as jnp
from jax import lax
from jax.experimental import pallas as pl
from jax.experimental.pallas import tpu as pltpu
from jax.experimental.pallas import tpu_sc as plsc

F32 = jnp.float32
BF16 = jnp.bfloat16
I32 = jnp.int32

D_MODEL = 1024
D_A = 512
D_B = 512
A_GROUPS = 8
A_GROUP_DIM = 64
CHUNK = 128
B_HEADS = 8
HEAD_DIM = 64
DILATIONS = (1, 4, 16)
HALF_WINDOW = 64
ROPE_THETA = 10000.0
D_IN = 2 * D_A + 3 * D_B
N_KEYS = 128
PEER_HEADS = 8
PEER_TOPK = 16
D_KEY = 256
N_SLOTS = PEER_HEADS * PEER_TOPK
EPS = 1e-6
NEG_BIG = -1e30

LANES = 128
SUB = 8
NCH = D_MODEL // LANES
U_HALF = -65536
QBLK = 128
ATTN_UNROLL = 4
RES16_PITCH = 24
IN_BLOCK = 512
MID_BLOCK = 256
PEER_BLOCK = 128
PEER_RING = 4
SC_SHARE = (39, 64)
PIPE_CHUNKS = 4
SC_ROWS = 16
SC_PASSES = 4
VMEM_LIMIT = 48 * 1024 * 1024


def _gelu(x):
    c = math.sqrt(2.0 / math.pi)
    return 0.5 * x * (1.0 + jnp.tanh(c * (x + 0.044715 * (x * x * x))))


def _rms(x, g):
    return x * lax.rsqrt(jnp.mean(x * x, axis=-1, keepdims=True) + EPS) * g


def _in_proj_kernel(x_ref, g1_ref, win_ref, lng_ref, lnb_ref, ws_ref, bs_ref, ga_ref,
                    cos_ref, sin_ref,
                    an_ref, q1_ref, k1_ref, v1_ref, q4_ref, k4_ref, v4_ref,
                    q16_ref, k16_ref, v16_ref, slab_ref):
    nt = x_ref.shape[0]
    h = _rms(x_ref[...], g1_ref[...]).astype(BF16)
    proj = jnp.dot(h, win_ref[...], preferred_element_type=F32)

    u = _gelu(proj[:, :D_A])
    v = _gelu(proj[:, D_A:2 * D_A])
    mu = jnp.mean(v, axis=-1, keepdims=True)
    vc = v - mu
    var = jnp.mean(vc * vc, axis=-1, keepdims=True)
    vln = (vc * lax.rsqrt(var + EPS) * lng_ref[...] + lnb_ref[...]).astype(BF16)
    lane = lax.broadcasted_iota(I32, (CHUNK, LANES), 1)
    lo = lane < A_GROUP_DIM
    zero = jnp.zeros((CHUNK, LANES), BF16)
    chunks = []
    for c in range(nt // CHUNK):
        cols = []
        for j in range(A_GROUPS // 2):
            vv = vln[c * CHUNK:(c + 1) * CHUNK, j * LANES:(j + 1) * LANES]
            rhs = jnp.concatenate([jnp.where(lo, vv, zero), jnp.where(lo, zero, vv)], axis=0)
            cols.append(jnp.dot(ws_ref[j], rhs, preferred_element_type=F32))
        chunks.append(jnp.concatenate(cols, axis=1) + bs_ref[...])
    mixed = jnp.concatenate(chunks, axis=0)
    an_ref[...] = _rms(u * mixed, ga_ref[...]).astype(BF16)

    cosf = cos_ref[...]
    sins = sin_ref[...]
    lane_b = lax.broadcasted_iota(I32, (nt, D_B), 1)
    first_half = (lane_b % HEAD_DIM) < (HEAD_DIM // 2)

    def rope(t):
        partner = jnp.where(first_half, pltpu.roll(t, D_B - HEAD_DIM // 2, 1),
                            pltpu.roll(t, HEAD_DIM // 2, 1))
        return t * cosf + partner * sins

    q = rope(proj[:, 2 * D_A:2 * D_A + D_B]) * (HEAD_DIM ** -0.5)
    k = rope(proj[:, 2 * D_A + D_B:2 * D_A + 2 * D_B])
    vv = proj[:, 2 * D_A + 2 * D_B:]
    q1_ref[...] = q.astype(BF16)
    k1_ref[...] = k.astype(BF16)
    v1_ref[...] = vv.astype(BF16)

    nslab = D_B // LANES
    for a, t in enumerate((q, k, vv)):
        for s in range(nslab):
            slab_ref[a * nslab + s] = t[:, s * LANES:(s + 1) * LANES]
    for d, outs in ((4, (q4_ref, k4_ref, v4_ref)), (16, (q16_ref, k16_ref, v16_ref))):
        rows = nt // d
        for a, o_ref in enumerate(outs):
            for r in range(d):
                for s in range(nslab):
                    o_ref[r, :, s * LANES:(s + 1) * LANES] = (
                        slab_ref[a * nslab + s, pl.ds(r, rows, stride=d), :].astype(BF16))


def _in_proj(x2, g1, win, lng, lnb, ws_cat, bs_full, ga, cosf, sins, batch, seq):
    t_total = x2.shape[0]
    nt = IN_BLOCK
    nb = seq // nt
    grid = (t_total // nt,)
    row = lambda i: (i, 0)
    const2 = lambda i: (0, 0)
    tok_bf = jax.ShapeDtypeStruct((t_total, D_B), BF16)
    out_shape = (
        jax.ShapeDtypeStruct((t_total, D_A), BF16),
        tok_bf, tok_bf, tok_bf,
        *(jax.ShapeDtypeStruct((batch, 4, seq // 4, D_B), BF16),) * 3,
        *(jax.ShapeDtypeStruct((batch, 16, seq // 16, D_B), BF16),) * 3,
    )
    res4 = pl.BlockSpec((None, 4, nt // 4, D_B), lambda i: (i // nb, 0, i % nb, 0))
    res16 = pl.BlockSpec((None, 16, nt // 16, D_B), lambda i: (i // nb, 0, i % nb, 0))
    tok_spec = pl.BlockSpec((nt, D_B), row)
    return pl.pallas_call(
        _in_proj_kernel,
        grid=grid,
        in_specs=[
            pl.BlockSpec((nt, D_MODEL), row),
            pl.BlockSpec((1, D_MODEL), const2),
            pl.BlockSpec((D_MODEL, D_IN), const2),
            pl.BlockSpec((1, D_A), const2),
            pl.BlockSpec((1, D_A), const2),
            pl.BlockSpec((A_GROUPS // 2, CHUNK, 2 * CHUNK), lambda i: (0, 0, 0)),
            pl.BlockSpec((CHUNK, D_A), const2),
            pl.BlockSpec((1, D_A), const2),
            pl.BlockSpec((nt, D_B), lambda i: (i % nb, 0)),
            pl.BlockSpec((nt, D_B), lambda i: (i % nb, 0)),
        ],
        out_specs=(pl.BlockSpec((nt, D_A), row), tok_spec, tok_spec, tok_spec,
                   res4, res4, res4, res16, res16, res16),
        out_shape=out_shape,
        scratch_shapes=[pltpu.VMEM((3 * D_B // LANES, nt, LANES), F32)],
        compiler_params=pltpu.CompilerParams(
            dimension_semantics=("arbitrary",), vmem_limit_bytes=VMEM_LIMIT),
        name="in_proj",
    )(x2, g1, win, lng, lnb, ws_cat, bs_full, ga, cosf, sins)


def _attn_kernel(q1_ref, k1_ref, v1_ref, q4_ref, k4_ref, v4_ref, q16_ref, k16_ref, v16_ref,
                 o_ref, out_ref, lse_ref, out16_ref, lse16_ref):
    seq = o_ref.shape[0]
    lane = lax.broadcasted_iota(I32, (QBLK, LANES), 1)
    head0 = lane < HEAD_DIM
    branches = ((1, q1_ref, k1_ref, v1_ref), (4, q4_ref, k4_ref, v4_ref),
                (16, q16_ref, k16_ref, v16_ref))
    for bi, (d, q_ref, k_ref, v_ref) in enumerate(branches):
        length = seq // d
        nblk = length // QBLK
        win = min(2 * QBLK, length)
        diff = (lax.broadcasted_iota(I32, (QBLK, win), 1)
                - lax.broadcasted_iota(I32, (QBLK, win), 0))

        def block(blk, carry, d=d, bi=bi, q_ref=q_ref, k_ref=k_ref, v_ref=v_ref,
                  length=length, nblk=nblk, win=win, diff=diff):
            r = blk // nblk
            i0 = pl.multiple_of((blk % nblk) * QBLK, QBLK)
            w0 = pl.multiple_of(jnp.clip(i0 - HALF_WINDOW, 0, length - win), HALF_WINDOW)
            qb = q_ref[r, pl.ds(i0, QBLK), :]
            kw = k_ref[r, pl.ds(w0, win), :]
            vw = v_ref[r, pl.ds(w0, win), :]
            rel = diff + (w0 - i0)
            valid = (rel >= -HALF_WINDOW) & (rel <= HALF_WINDOW)
            zero = jnp.zeros_like(qb)
            qq = jnp.concatenate([jnp.where(head0, qb, zero), jnp.where(head0, zero, qb)], axis=0)
            s = lax.dot_general(qq, kw, (((1,), (1,)), ((), ())), preferred_element_type=F32)
            s = jnp.where(jnp.concatenate([valid, valid], axis=0), s, NEG_BIG)
            m = jnp.max(s, axis=1, keepdims=True)
            p = jnp.exp(s - m)
            l = jnp.sum(p, axis=1, keepdims=True)
            pv = jnp.dot(p.astype(BF16), vw, preferred_element_type=F32) / l
            ml = m + jnp.log(l)
            out = jnp.where(head0, pv[:QBLK], pv[QBLK:])
            lse = jnp.where(head0, ml[:QBLK], ml[QBLK:])
            if d == 1:
                out_ref[0, pl.ds(i0, QBLK), :] = out
                lse_ref[0, pl.ds(i0, QBLK), :] = lse
            elif d == 4:
                rows = pl.ds(i0 * d + r, QBLK, stride=d)
                out_ref[1, rows, :] = out
                lse_ref[1, rows, :] = lse
            else:
                rows = pl.ds(r, QBLK, stride=RES16_PITCH)
                out16_ref[rows, :] = out
                lse16_ref[rows, :] = lse
            return carry

        lax.fori_loop(0, d * nblk, block, 0, unroll=ATTN_UNROLL)

    groups = QBLK // 16

    def merge(c, carry):
        rows = pl.ds(pl.multiple_of(c * QBLK, QBLK), QBLK)
        base = pl.multiple_of(c * (groups * RES16_PITCH), SUB)
        pieces = [pl.ds(base + g * RES16_PITCH, 16) for g in range(groups)]
        o3 = jnp.concatenate([out16_ref[pc, :] for pc in pieces], axis=0)
        e3 = jnp.concatenate([lse16_ref[pc, :] for pc in pieces], axis=0)
        e1, e2 = lse_ref[0, rows, :], lse_ref[1, rows, :]
        mx = jnp.maximum(jnp.maximum(e1, e2), e3)
        w1, w2, w3 = jnp.exp(e1 - mx), jnp.exp(e2 - mx), jnp.exp(e3 - mx)
        num = w1 * out_ref[0, rows, :] + w2 * out_ref[1, rows, :] + w3 * o3
        o_ref[rows, :] = num / (w1 + w2 + w3)
        return carry

    lax.fori_loop(0, seq // QBLK, merge, 0)


def _attention(q1, k1, v1, q4, k4, v4, q16, k16, v16, batch, seq):
    npair = D_B // LANES
    nat = pl.BlockSpec((None, 1, seq, LANES), lambda b, p: (b, 0, 0, p))
    r4 = pl.BlockSpec((None, 4, seq // 4, LANES), lambda b, p: (b, 0, 0, p))
    r16 = pl.BlockSpec((None, 16, seq // 16, LANES), lambda b, p: (b, 0, 0, p))
    q1, k1, v1 = (t.reshape(batch, 1, seq, D_B) for t in (q1, k1, v1))
    return pl.pallas_call(
        _attn_kernel,
        grid=(batch, npair),
        in_specs=[nat, nat, nat, r4, r4, r4, r16, r16, r16],
        out_specs=pl.BlockSpec((None, seq, LANES), lambda b, p: (b, 0, p)),
        out_shape=jax.ShapeDtypeStruct((batch, seq, D_B), F32),
        scratch_shapes=[pltpu.VMEM((2, seq, LANES), F32)] * 2
        + [pltpu.VMEM((seq // 16 * RES16_PITCH, LANES), F32)] * 2,
        compiler_params=pltpu.CompilerParams(
            dimension_semantics=("arbitrary", "arbitrary"), vmem_limit_bytes=VMEM_LIMIT),
        name="dilated_attn",
    )(q1, k1, v1, q4, k4, v4, q16, k16, v16)


def _topk_rows(s, k):
    n = s.shape[0]
    iota = lax.broadcasted_iota(I32, s.shape, 0).astype(F32)
    vals, idxs = [], []
    for _ in range(k):
        m = jnp.max(s, axis=0, keepdims=True)
        i = jnp.min(jnp.where(s == m, iota, float(n)), axis=0, keepdims=True)
        vals.append(m)
        idxs.append(i)
        s = jnp.where(iota == i, -jnp.inf, s)
    return jnp.concatenate(vals, axis=0), jnp.concatenate(idxs, axis=0).astype(I32)


def _take_rows(table, sel):
    out = jnp.zeros(sel.shape, table.dtype)
    for a in range(table.shape[0]):
        out = jnp.where(sel == a, table[a:a + 1, :], out)
    return out


def _mid_kernel(x_ref, an_ref, bo_ref, gb_ref, wout_ref, g2_ref, wq_ref, keys_ref,
                x1_ref, xn_ref, idx_ref, gate_ref):
    nt = x_ref.shape[0]
    bn = _rms(bo_ref[...], gb_ref[...]).astype(BF16)
    x1 = (x_ref[...]
          + jnp.dot(an_ref[...], wout_ref[:D_A, :], preferred_element_type=F32)
          + jnp.dot(bn, wout_ref[D_A:, :], preferred_element_type=F32))
    x1_ref[...] = x1
    xn = _rms(x1, g2_ref[...])
    xn_ref[...] = xn
    q = jnp.dot(xn.astype(BF16), wq_ref[...], preferred_element_type=F32).astype(BF16)
    keys = (keys_ref[0], keys_ref[1])
    half = D_KEY // 2
    for c in range(nt // LANES):
        qc = q[c * LANES:(c + 1) * LANES, :]
        experts, gates = [], []
        for h in range(PEER_HEADS):
            tops = []
            for p in range(2):
                qhp = qc[:, (2 * h + p) * half:(2 * h + p + 1) * half]
                s = lax.dot_general(keys[p], qhp, (((1,), (1,)), ((), ())),
                                    preferred_element_type=F32)
                tops.append(_topk_rows(s, PEER_TOPK))
            (s1, i1), (s2, i2) = tops
            cand = jnp.concatenate(
                [s1[0:1, :] + s2]
                + [s1[a:a + 1, :] + s2[0:SUB, :] for a in range(1, SUB)]
                + [s1[SUB:, :] + s2[0:1, :]], axis=0)
            sc, pos = _topk_rows(cand, PEER_TOPK)
            ca = jnp.where(pos < PEER_TOPK, 0,
                           jnp.where(pos < PEER_TOPK + SUB * (SUB - 1), (pos >> 3) - 1, pos - SUB * SUB))
            cb = jnp.where(pos < PEER_TOPK, pos,
                           jnp.where(pos < PEER_TOPK + SUB * (SUB - 1), pos & (SUB - 1), 0))
            e = _take_rows(i1, ca) * N_KEYS + _take_rows(i2, cb)
            ex = jnp.exp(sc - sc[0:1, :])
            gates.append(ex / jnp.sum(ex, axis=0, keepdims=True))
            experts.append(e)
        idx_ref[c] = jnp.concatenate(experts, axis=0).T
        gate_ref[c * LANES:(c + 1) * LANES, :] = jnp.concatenate(gates, axis=0).T


def _mid(x2, an, bo, gb, wout, g2, wq, keys):
    t_total = x2.shape[0]
    nt = MID_BLOCK
    row = lambda i: (i, 0)
    const2 = lambda i: (0, 0)
    return pl.pallas_call(
        _mid_kernel,
        grid=(t_total // nt,),
        in_specs=[
            pl.BlockSpec((nt, D_MODEL), row),
            pl.BlockSpec((nt, D_A), row),
            pl.BlockSpec((nt, D_B), row),
            pl.BlockSpec((1, D_B), const2),
            pl.BlockSpec((D_MODEL, D_MODEL), const2),
            pl.BlockSpec((1, D_MODEL), const2),
            pl.BlockSpec((D_MODEL, PEER_HEADS * D_KEY), const2),
            pl.BlockSpec((2, N_KEYS, D_KEY // 2), lambda i: (0, 0, 0)),
        ],
        out_specs=(
            pl.BlockSpec((nt, D_MODEL), row),
            pl.BlockSpec((nt, D_MODEL), row),
            pl.BlockSpec((nt // LANES, LANES, N_SLOTS), lambda i: (i, 0, 0)),
            pl.BlockSpec((nt, N_SLOTS), row),
        ),
        out_shape=(
            jax.ShapeDtypeStruct((t_total, D_MODEL), F32),
            jax.ShapeDtypeStruct((t_total, D_MODEL), F32),
            jax.ShapeDtypeStruct((t_total // LANES, LANES, N_SLOTS), I32),
            jax.ShapeDtypeStruct((t_total, N_SLOTS), F32),
        ),
        compiler_params=pltpu.CompilerParams(
            dimension_semantics=("arbitrary",), vmem_limit_bytes=VMEM_LIMIT),
        name="mid",
    )(x2, an, bo, gb, wout, g2, wq, keys)


def _peer_kernel(idx_ref, gate_ref, xn_ref, x1_ref, gf_ref, uv_ref, y_ref, *scratch):
    rows_refs = scratch[:PEER_RING]
    bf_ref, sem_ref = scratch[PEER_RING:]
    nt = xn_ref.shape[0]
    wide = 2 * LANES
    nw = 2 * D_MODEL // wide

    def issue(t, slot, lo=0, hi=N_SLOTS):
        for s in range(lo, hi):
            pltpu.make_async_copy(uv_ref.at[idx_ref[t, s]],
                                  rows_refs[slot].at[s // SUB, :, s % SUB, :],
                                  sem_ref.at[slot]).start(priority=s % 2)

    def wait(slot):
        pltpu.make_async_copy(uv_ref.at[pl.ds(0, N_SLOTS)],
                              rows_refs[slot].reshape(N_SLOTS, NCH, LANES), sem_ref.at[slot]).wait()

    def stage(slot, k):
        for j in range(NCH):
            w = rows_refs[slot][:, j, :, :].reshape(N_SLOTS, LANES)
            bf_ref[k, :, j * LANES:(j + 1) * LANES] = (
                pltpu.bitcast(w & U_HALF, F32).astype(BF16))
            bf_ref[k, :, D_MODEL + j * LANES:D_MODEL + (j + 1) * LANES] = (
                pltpu.bitcast(w << 16, F32).astype(BF16))

    def pair(t0, slots, prefetch):
        for k in range(2):
            wait(slots[k])
            stage(slots[k], k)
        nbatch = 2 * nw
        per = N_SLOTS // (nbatch // 2)
        batches = [(k, b * per, (b + 1) * per) for b in range(nbatch // 2) for k in range(2)]

        def next_batch():
            if prefetch and batches:
                k, lo, hi = batches.pop(0)
                issue(t0 + PEER_RING + k, slots[k], lo, hi)

        x8 = [jnp.broadcast_to(xn_ref[pl.ds(t0 + k, 1), :], (SUB, D_MODEL)).astype(BF16)
              for k in range(2)]
        act = [jnp.zeros((SUB, N_SLOTS), F32) for _ in range(2)]
        for j in range(nw // 2):
            for k in range(2):
                next_batch()
                act[k] = act[k] + lax.dot_general(
                    x8[k][:, j * wide:(j + 1) * wide], bf_ref[k, :, j * wide:(j + 1) * wide],
                    (((1,), (1,)), ((), ())), preferred_element_type=F32)
        w = [(_gelu(act[k]) * gate_ref[pl.ds(t0 + k, 1), :]).astype(BF16) for k in range(2)]
        outs = [[], []]
        for j in range(nw // 2):
            for k in range(2):
                next_batch()
                outs[k].append(jnp.dot(
                    w[k], bf_ref[k, :, D_MODEL + j * wide:D_MODEL + (j + 1) * wide],
                    preferred_element_type=F32)[0:1, :])
        for k in range(2):
            out = jnp.concatenate(outs[k], axis=1)
            y_ref[pl.ds(t0 + k, 1), :] = _rms(x1_ref[pl.ds(t0 + k, 1), :] + out, gf_ref[...])

    def group(g, prefetch):
        for p in range(PEER_RING // 2):
            pair(g * PEER_RING + 2 * p, (2 * p, 2 * p + 1), prefetch)

    for t in range(PEER_RING):
        issue(t, t)
    ngroup = nt // PEER_RING
    lax.fori_loop(0, ngroup - 1, lambda g, c: (group(g, True), c)[1], 0)
    group(ngroup - 1, False)


def _peer(idx, gates, xn, x1, gf, uv, first_token):
    t_total = xn.shape[0] - first_token
    nt = PEER_BLOCK
    b0 = first_token // nt
    row = lambda i: (i + b0, 0)
    return pl.pallas_call(
        _peer_kernel,
        grid=(t_total // nt,),
        in_specs=[
            pl.BlockSpec((None, nt, N_SLOTS), lambda i: (i + b0, 0, 0), memory_space=pltpu.SMEM),
            pl.BlockSpec((nt, N_SLOTS), row),
            pl.BlockSpec((nt, D_MODEL), row),
            pl.BlockSpec((nt, D_MODEL), row),
            pl.BlockSpec((1, D_MODEL), lambda i: (0, 0)),
            pl.BlockSpec(memory_space=pl.ANY),
        ],
        out_specs=pl.BlockSpec((nt, D_MODEL), lambda i: (i, 0)),
        out_shape=jax.ShapeDtypeStruct((t_total, D_MODEL), F32),
        scratch_shapes=[pltpu.VMEM((N_SLOTS // SUB, NCH, SUB, LANES), I32)] * PEER_RING + [
            pltpu.VMEM((2, N_SLOTS, 2 * D_MODEL), BF16),
            pltpu.SemaphoreType.DMA((PEER_RING,))],
        compiler_params=pltpu.CompilerParams(
            dimension_semantics=("arbitrary",), vmem_limit_bytes=VMEM_LIMIT),
        name="peer",
    )(idx, gates, xn, x1, gf, uv)


def _sc_peer(idx, xn, gates, uv2, n_tokens):
    info = plsc.get_sparse_core_info()
    nc, lanes_n = info.num_cores, info.num_lanes
    nw = nc * info.num_subcores
    per = n_tokens // nw
    assert n_tokens % nw == 0
    nchunk = N_SLOTS // SC_ROWS
    qv = D_MODEL // (SC_PASSES * lanes_n)
    c0 = math.sqrt(2.0 / math.pi)
    mesh = plsc.VectorSubcoreMesh(core_axis_name="c", subcore_axis_name="s")

    @functools.partial(
        pl.kernel, mesh=mesh,
        out_type=jax.ShapeDtypeStruct((n_tokens, D_MODEL), F32),
        scratch_types=[
            pltpu.VMEM((nchunk, SC_ROWS), I32),
            pltpu.VMEM((D_MODEL,), F32),
            pltpu.VMEM((N_SLOTS,), F32),
            pltpu.VMEM((SC_ROWS, D_MODEL), I32),
            pltpu.VMEM((SC_ROWS, D_MODEL), I32),
            pltpu.VMEM((D_MODEL,), F32),
            pltpu.VMEM((SC_ROWS, lanes_n), F32),
            pltpu.VMEM((SC_ROWS,), F32),
            pltpu.SemaphoreType.DMA,
            pltpu.SemaphoreType.DMA,
        ],
        compiler_params=pltpu.CompilerParams(needs_layout_passes=False),
        name="peer_sc",
    )
    def sc_kernel(idx_hbm, xn_hbm, gate_hbm, uv_hbm, out_hbm, idx_v, x_v, g_v, rows0, rows1,
                  out_v, part_v, w_v, sem0, sem1):
        wid = lax.axis_index("s") * nc + lax.axis_index("c")
        lane_ids = lax.iota(I32, lanes_n)
        zero = jnp.zeros((lanes_n,), F32)
        bufs = ((rows0, sem0), (rows1, sem1))

        def vec(q, j):
            return pl.ds((q * qv + j) * lanes_n, lanes_n)

        def gather(c, b):
            rows, sem = bufs[b]
            return pltpu.make_async_copy(uv_hbm.at[idx_v.at[c]], rows, sem)

        def chunk(c, b):
            rows, _ = bufs[b]
            def ustep(j, acc):
                xj = x_v[pl.ds(j * lanes_n, lanes_n)]
                return tuple(
                    acc[r] + plsc.bitcast(rows[r, pl.ds(j * lanes_n, lanes_n)] & U_HALF, F32) * xj
                    for r in range(SC_ROWS))

            acc = lax.fori_loop(0, D_MODEL // lanes_n, ustep, (zero,) * SC_ROWS)
            for r in range(SC_ROWS):
                part_v[r, :] = acc[r]
            act = zero
            for l in range(lanes_n):
                act = act + plsc.load_gather(part_v, [lane_ids, jnp.full((lanes_n,), l, I32)])
            z = c0 * (act + 0.044715 * (act * act * act))
            tanh_z = 1.0 - 2.0 / (jnp.exp(2.0 * z) + 1.0)
            w_v[...] = 0.5 * act * (1.0 + tanh_z) * g_v[pl.ds(c * SC_ROWS, SC_ROWS)]
            for q in range(SC_PASSES):
                o = tuple(out_v[vec(q, j)] for j in range(qv))

                def vrow(r, o, q=q):
                    wr = plsc.load_gather(w_v, [jnp.full((lanes_n,), r, I32)])
                    return tuple(o[j] + wr * plsc.bitcast(rows[r, vec(q, j)] << 16, F32)
                                 for j in range(qv))

                o = lax.fori_loop(0, SC_ROWS, vrow, o)
                for j in range(qv):
                    out_v[vec(q, j)] = o[j]

        @pl.loop(0, per)
        def _(i):
            t = wid * per + i
            pltpu.sync_copy(idx_hbm.at[t], idx_v)
            pltpu.sync_copy(xn_hbm.at[t], x_v)
            pltpu.sync_copy(gate_hbm.at[t], g_v)
            for j in range(D_MODEL // lanes_n):
                out_v[pl.ds(j * lanes_n, lanes_n)] = zero
            gather(0, 0).start()

            @pl.loop(0, nchunk, step=2)
            def _(c):
                gather(c + 1, 1).start()
                gather(c, 0).wait()
                chunk(c, 0)

                @pl.when(c + 2 < nchunk)
                def _():
                    gather(c + 2, 0).start()

                gather(c + 1, 1).wait()
                chunk(c + 1, 1)

            pltpu.sync_copy(out_v, out_hbm.at[t])

    return sc_kernel(idx.reshape(-1, nchunk, SC_ROWS), xn, gates, uv2)


def _residual_norm_kernel(x1_ref, o_ref, gf_ref, y_ref):
    y_ref[...] = _rms(x1_ref[...] + o_ref[...], gf_ref[...])


def _residual_norm(x1, out, gf):
    n = out.shape[0]
    nt = PEER_BLOCK
    row = lambda i: (i, 0)
    return pl.pallas_call(
        _residual_norm_kernel,
        grid=(n // nt,),
        in_specs=[pl.BlockSpec((nt, D_MODEL), row), pl.BlockSpec((nt, D_MODEL), row),
                  pl.BlockSpec((1, D_MODEL), lambda i: (0, 0))],
        out_specs=pl.BlockSpec((nt, D_MODEL), row),
        out_shape=jax.ShapeDtypeStruct((n, D_MODEL), F32),
        compiler_params=pltpu.CompilerParams(dimension_semantics=("arbitrary",)),
        name="residual_norm",
    )(x1, out, gf)


def _rope_tables(seq):
    pos = jnp.arange(seq, dtype=F32)
    inv = 1.0 / (ROPE_THETA ** (jnp.arange(0, HEAD_DIM, 2, dtype=F32) / HEAD_DIM))
    ang = pos[:, None] * inv[None, :]
    cos, sin = jnp.cos(ang), jnp.sin(ang)
    cosf = jnp.tile(jnp.concatenate([cos, cos], axis=1), (1, B_HEADS))
    sins = jnp.tile(jnp.concatenate([-sin, sin], axis=1), (1, B_HEADS))
    return cosf, sins


def kernel(x, norm1_g, w_in, ln_v_g, ln_v_b, w_spatial, b_spatial, out_norm_a_g, out_norm_b_g,
           w_out, norm2_g, w_query, sub_keys, expert_u, expert_v, final_norm_g):
    batch, seq, _ = x.shape
    assert w_in.shape[0] == 1 and seq % (16 * QBLK) == 0 and seq % IN_BLOCK == 0
    row = lambda g: g.reshape(1, -1).astype(F32)

    ws = w_spatial[0].astype(BF16)
    ws_cat = jnp.concatenate([ws[0::2], ws[1::2]], axis=2)
    bs_full = jnp.repeat(b_spatial[0].T, A_GROUP_DIM, axis=1)
    cosf, sins = _rope_tables(seq)
    win, wout, wq = w_in[0].astype(BF16), w_out[0].astype(BF16), w_query[0].astype(BF16)
    keys = sub_keys[0].astype(BF16)
    gf = row(final_norm_g)
    half = lambda t: lax.bitcast_convert_type(t.astype(BF16), jnp.uint16).astype(jnp.uint32)
    uv2 = lax.bitcast_convert_type((half(expert_u[0]) << 16) | half(expert_v[0]), I32)
    uv = uv2.reshape(-1, NCH, LANES)

    nchunk = PIPE_CHUNKS if batch % PIPE_CHUNKS == 0 else 1
    cb = batch // nchunk
    pieces = []
    for ci in range(nchunk):
        x2 = x[ci * cb:(ci + 1) * cb].reshape(cb * seq, D_MODEL)
        an, q1, k1, v1, q4, k4, v4, q16, k16, v16 = _in_proj(
            x2, row(norm1_g[0]), win, row(ln_v_g[0]), row(ln_v_b[0]),
            ws_cat, bs_full, row(out_norm_a_g[0]), cosf, sins, cb, seq)
        bo = _attention(q1, k1, v1, q4, k4, v4, q16, k16, v16, cb, seq).reshape(cb * seq, D_B)
        x1, xn, idx, gates = _mid(x2, an, bo, row(out_norm_b_g[0]), wout, row(norm2_g[0]), wq, keys)
        n_sc = (cb * seq) * SC_SHARE[0] // SC_SHARE[1]
        assert n_sc % PEER_BLOCK == 0
        out_sc = _sc_peer(idx.reshape(cb * seq, N_SLOTS), xn, gates, uv2, n_sc)
        y_tc = _peer(idx, gates, xn, x1, gf, uv, n_sc)
        pieces += [_residual_norm(x1, out_sc, gf), y_tc]
    return jnp.concatenate(pieces, axis=0).reshape(batch, seq, D_MODEL)
```

```python
import functools
import math

import jax
import jax.numpy as jnp
from jax import lax
from jax.experimental import pallas as pl
from jax.experimental.pallas import tpu as pltpu
from jax.experimental.pallas import tpu_sc as plsc

F32 = jnp.float32
BF16 = jnp.bfloat16
I32 = jnp.int32

D_MODEL = 1024
D_A = 512
D_B = 512
A_GROUPS = 8
A_GROUP_DIM = 64
CHUNK = 128
B_HEADS = 8
HEAD_DIM = 64
DILATIONS = (1, 4, 16)
HALF_WINDOW = 64
ROPE_THETA = 10000.0
D_IN = 2 * D_A + 3 * D_B
N_KEYS = 128
PEER_HEADS = 8
PEER_TOPK = 16
D_KEY = 256
N_SLOTS = PEER_HEADS * PEER_TOPK
EPS = 1e-6
NEG_BIG = -1e30

LANES = 128
SUB = 8
NCH = D_MODEL // LANES
U_HALF = -65536
QBLK = 128
ATTN_UNROLL = 4
RES16_PITCH = 24
IN_BLOCK = 512
MID_BLOCK = 256
PEER_BLOCK = 128
PEER_RING = 4
SC_SHARE = (39, 64)
PIPE_CHUNKS = 8
SC_ROWS = 16
SC_PASSES = 4
VMEM_LIMIT = 48 * 1024 * 1024


def _gelu(x):
    c = math.sqrt(2.0 / math.pi)
    return 0.5 * x * (1.0 + jnp.tanh(c * (x + 0.044715 * (x * x * x))))


def _rms(x, g):
    return x * lax.rsqrt(jnp.mean(x * x, axis=-1, keepdims=True) + EPS) * g


def _in_proj_kernel(x_ref, g1_ref, win_ref, lng_ref, lnb_ref, ws_ref, bs_ref, ga_ref,
                    cos_ref, sin_ref,
                    an_ref, q1_ref, k1_ref, v1_ref, q4_ref, k4_ref, v4_ref,
                    q16_ref, k16_ref, v16_ref, slab_ref):
    nt = x_ref.shape[0]
    h = _rms(x_ref[...], g1_ref[...]).astype(BF16)
    proj = jnp.dot(h, win_ref[...], preferred_element_type=F32)

    u = _gelu(proj[:, :D_A])
    v = _gelu(proj[:, D_A:2 * D_A])
    mu = jnp.mean(v, axis=-1, keepdims=True)
    vc = v - mu
    var = jnp.mean(vc * vc, axis=-1, keepdims=True)
    vln = (vc * lax.rsqrt(var + EPS) * lng_ref[...] + lnb_ref[...]).astype(BF16)
    lane = lax.broadcasted_iota(I32, (CHUNK, LANES), 1)
    lo = lane < A_GROUP_DIM
    zero = jnp.zeros((CHUNK, LANES), BF16)
    chunks = []
    for c in range(nt // CHUNK):
        cols = []
        for j in range(A_GROUPS // 2):
            vv = vln[c * CHUNK:(c + 1) * CHUNK, j * LANES:(j + 1) * LANES]
            rhs = jnp.concatenate([jnp.where(lo, vv, zero), jnp.where(lo, zero, vv)], axis=0)
            cols.append(jnp.dot(ws_ref[j], rhs, preferred_element_type=F32))
        chunks.append(jnp.concatenate(cols, axis=1) + bs_ref[...])
    mixed = jnp.concatenate(chunks, axis=0)
    an_ref[...] = _rms(u * mixed, ga_ref[...]).astype(BF16)

    cosf = cos_ref[...]
    sins = sin_ref[...]
    lane_b = lax.broadcasted_iota(I32, (nt, D_B), 1)
    first_half = (lane_b % HEAD_DIM) < (HEAD_DIM // 2)

    def rope(t):
        partner = jnp.where(first_half, pltpu.roll(t, D_B - HEAD_DIM // 2, 1),
                            pltpu.roll(t, HEAD_DIM // 2, 1))
        return t * cosf + partner * sins

    q = rope(proj[:, 2 * D_A:2 * D_A + D_B]) * (HEAD_DIM ** -0.5)
    k = rope(proj[:, 2 * D_A + D_B:2 * D_A + 2 * D_B])
    vv = proj[:, 2 * D_A + 2 * D_B:]
    q1_ref[...] = q.astype(BF16)
    k1_ref[...] = k.astype(BF16)
    v1_ref[...] = vv.astype(BF16)

    nslab = D_B // LANES
    for a, t in enumerate((q, k, vv)):
        for s in range(nslab):
            slab_ref[a * nslab + s] = t[:, s * LANES:(s + 1) * LANES]
    for d, outs in ((4, (q4_ref, k4_ref, v4_ref)), (16, (q16_ref, k16_ref, v16_ref))):
        rows = nt // d
        for a, o_ref in enumerate(outs):
            for r in range(d):
                for s in range(nslab):
                    o_ref[r, :, s * LANES:(s + 1) * LANES] = (
                        slab_ref[a * nslab + s, pl.ds(r, rows, stride=d), :].astype(BF16))


def _in_proj(x2, g1, win, lng, lnb, ws_cat, bs_full, ga, cosf, sins, batch, seq):
    t_total = x2.shape[0]
    nt = IN_BLOCK
    nb = seq // nt
    grid = (t_total // nt,)
    row = lambda i: (i, 0)
    const2 = lambda i: (0, 0)
    tok_bf = jax.ShapeDtypeStruct((t_total, D_B), BF16)
    out_shape = (
        jax.ShapeDtypeStruct((t_total, D_A), BF16),
        tok_bf, tok_bf, tok_bf,
        *(jax.ShapeDtypeStruct((batch, 4, seq // 4, D_B), BF16),) * 3,
        *(jax.ShapeDtypeStruct((batch, 16, seq // 16, D_B), BF16),) * 3,
    )
    res4 = pl.BlockSpec((None, 4, nt // 4, D_B), lambda i: (i // nb, 0, i % nb, 0))
    res16 = pl.BlockSpec((None, 16, nt // 16, D_B), lambda i: (i // nb, 0, i % nb, 0))
    tok_spec = pl.BlockSpec((nt, D_B), row)
    return pl.pallas_call(
        _in_proj_kernel,
        grid=grid,
        in_specs=[
            pl.BlockSpec((nt, D_MODEL), row),
            pl.BlockSpec((1, D_MODEL), const2),
            pl.BlockSpec((D_MODEL, D_IN), const2),
            pl.BlockSpec((1, D_A), const2),
            pl.BlockSpec((1, D_A), const2),
            pl.BlockSpec((A_GROUPS // 2, CHUNK, 2 * CHUNK), lambda i: (0, 0, 0)),
            pl.BlockSpec((CHUNK, D_A), const2),
            pl.BlockSpec((1, D_A), const2),
            pl.BlockSpec((nt, D_B), lambda i: (i % nb, 0)),
            pl.BlockSpec((nt, D_B), lambda i: (i % nb, 0)),
        ],
        out_specs=(pl.BlockSpec((nt, D_A), row), tok_spec, tok_spec, tok_spec,
                   res4, res4, res4, res16, res16, res16),
        out_shape=out_shape,
        scratch_shapes=[pltpu.VMEM((3 * D_B // LANES, nt, LANES), F32)],
        compiler_params=pltpu.CompilerParams(
            dimension_semantics=("arbitrary",), vmem_limit_bytes=VMEM_LIMIT),
        name="in_proj",
    )(x2, g1, win, lng, lnb, ws_cat, bs_full, ga, cosf, sins)


def _attn_kernel(q1_ref, k1_ref, v1_ref, q4_ref, k4_ref, v4_ref, q16_ref, k16_ref, v16_ref,
                 o_ref, out_ref, lse_ref, out16_ref, lse16_ref):
    seq = o_ref.shape[0]
    lane = lax.broadcasted_iota(I32, (QBLK, LANES), 1)
    head0 = lane < HEAD_DIM
    branches = ((1, q1_ref, k1_ref, v1_ref), (4, q4_ref, k4_ref, v4_ref),
                (16, q16_ref, k16_ref, v16_ref))
    for bi, (d, q_ref, k_ref, v_ref) in enumerate(branches):
        length = seq // d
        nblk = length // QBLK
        win = min(2 * QBLK, length)
        diff = (lax.broadcasted_iota(I32, (QBLK, win), 1)
                - lax.broadcasted_iota(I32, (QBLK, win), 0))

        def block(blk, carry, d=d, bi=bi, q_ref=q_ref, k_ref=k_ref, v_ref=v_ref,
                  length=length, nblk=nblk, win=win, diff=diff):
            r = blk // nblk
            i0 = pl.multiple_of((blk % nblk) * QBLK, QBLK)
            w0 = pl.multiple_of(jnp.clip(i0 - HALF_WINDOW, 0, length - win), HALF_WINDOW)
            qb = q_ref[r, pl.ds(i0, QBLK), :]
            kw = k_ref[r, pl.ds(w0, win), :]
            vw = v_ref[r, pl.ds(w0, win), :]
            rel = diff + (w0 - i0)
            valid = (rel >= -HALF_WINDOW) & (rel <= HALF_WINDOW)
            zero = jnp.zeros_like(qb)
            qq = jnp.concatenate([jnp.where(head0, qb, zero), jnp.where(head0, zero, qb)], axis=0)
            s = lax.dot_general(qq, kw, (((1,), (1,)), ((), ())), preferred_element_type=F32)
            s = jnp.where(jnp.concatenate([valid, valid], axis=0), s, NEG_BIG)
            m = jnp.max(s, axis=1, keepdims=True)
            p = jnp.exp(s - m)
            l = jnp.sum(p, axis=1, keepdims=True)
            pv = jnp.dot(p.astype(BF16), vw, preferred_element_type=F32) / l
            ml = m + jnp.log(l)
            out = jnp.where(head0, pv[:QBLK], pv[QBLK:])
            lse = jnp.where(head0, ml[:QBLK], ml[QBLK:])
            if d == 1:
                out_ref[0, pl.ds(i0, QBLK), :] = out
                lse_ref[0, pl.ds(i0, QBLK), :] = lse
            elif d == 4:
                rows = pl.ds(i0 * d + r, QBLK, stride=d)
                out_ref[1, rows, :] = out
                lse_ref[1, rows, :] = lse
            else:
                rows = pl.ds(r, QBLK, stride=RES16_PITCH)
                out16_ref[rows, :] = out
                lse16_ref[rows, :] = lse
            return carry

        lax.fori_loop(0, d * nblk, block, 0, unroll=ATTN_UNROLL)

    groups = QBLK // 16

    def merge(c, carry):
        rows = pl.ds(pl.multiple_of(c * QBLK, QBLK), QBLK)
        base = pl.multiple_of(c * (groups * RES16_PITCH), SUB)
        pieces = [pl.ds(base + g * RES16_PITCH, 16) for g in range(groups)]
        o3 = jnp.concatenate([out16_ref[pc, :] for pc in pieces], axis=0)
        e3 = jnp.concatenate([lse16_ref[pc, :] for pc in pieces], axis=0)
        e1, e2 = lse_ref[0, rows, :], lse_ref[1, rows, :]
        mx = jnp.maximum(jnp.maximum(e1, e2), e3)
        w1, w2, w3 = jnp.exp(e1 - mx), jnp.exp(e2 - mx), jnp.exp(e3 - mx)
        num = w1 * out_ref[0, rows, :] + w2 * out_ref[1, rows, :] + w3 * o3
        o_ref[rows, :] = num / (w1 + w2 + w3)
        return carry

    lax.fori_loop(0, seq // QBLK, merge, 0)


def _attention(q1, k1, v1, q4, k4, v4, q16, k16, v16, batch, seq):
    npair = D_B // LANES
    nat = pl.BlockSpec((None, 1, seq, LANES), lambda b, p: (b, 0, 0, p))
    r4 = pl.BlockSpec((None, 4, seq // 4, LANES), lambda b, p: (b, 0, 0, p))
    r16 = pl.BlockSpec((None, 16, seq // 16, LANES), lambda b, p: (b, 0, 0, p))
    q1, k1, v1 = (t.reshape(batch, 1, seq, D_B) for t in (q1, k1, v1))
    return pl.pallas_call(
        _attn_kernel,
        grid=(batch, npair),
        in_specs=[nat, nat, nat, r4, r4, r4, r16, r16, r16],
        out_specs=pl.BlockSpec((None, seq, LANES), lambda b, p: (b, 0, p)),
        out_shape=jax.ShapeDtypeStruct((batch, seq, D_B), F32),
        scratch_shapes=[pltpu.VMEM((2, seq, LANES), F32)] * 2
        + [pltpu.VMEM((seq // 16 * RES16_PITCH, LANES), F32)] * 2,
        compiler_params=pltpu.CompilerParams(
            dimension_semantics=("arbitrary", "arbitrary"), vmem_limit_bytes=VMEM_LIMIT),
        name="dilated_attn",
    )(q1, k1, v1, q4, k4, v4, q16, k16, v16)


def _topk_rows(s, k):
    n = s.shape[0]
    iota = lax.broadcasted_iota(I32, s.shape, 0).astype(F32)
    vals, idxs = [], []
    for _ in range(k):
        m = jnp.max(s, axis=0, keepdims=True)
        i = jnp.min(jnp.where(s == m, iota, float(n)), axis=0, keepdims=True)
        vals.append(m)
        idxs.append(i)
        s = jnp.where(iota == i, -jnp.inf, s)
    return jnp.concatenate(vals, axis=0), jnp.concatenate(idxs, axis=0).astype(I32)


def _take_rows(table, sel):
    out = jnp.zeros(sel.shape, table.dtype)
    for a in range(table.shape[0]):
        out = jnp.where(sel == a, table[a:a + 1, :], out)
    return out


def _mid_kernel(x_ref, an_ref, bo_ref, gb_ref, wout_ref, g2_ref, wq_ref, keys_ref,
                x1_ref, xn_ref, idx_ref, gate_ref):
    nt = x_ref.shape[0]
    bn = _rms(bo_ref[...], gb_ref[...]).astype(BF16)
    x1 = (x_ref[...]
          + jnp.dot(an_ref[...], wout_ref[:D_A, :], preferred_element_type=F32)
          + jnp.dot(bn, wout_ref[D_A:, :], preferred_element_type=F32))
    x1_ref[...] = x1
    xn = _rms(x1, g2_ref[...])
    xn_ref[...] = xn
    q = jnp.dot(xn.astype(BF16), wq_ref[...], preferred_element_type=F32).astype(BF16)
    keys = (keys_ref[0], keys_ref[1])
    half = D_KEY // 2
    for c in range(nt // LANES):
        qc = q[c * LANES:(c + 1) * LANES, :]
        experts, gates = [], []
        for h in range(PEER_HEADS):
            tops = []
            for p in range(2):
                qhp = qc[:, (2 * h + p) * half:(2 * h + p + 1) * half]
                s = lax.dot_general(keys[p], qhp, (((1,), (1,)), ((), ())),
                                    preferred_element_type=F32)
                tops.append(_topk_rows(s, PEER_TOPK))
            (s1, i1), (s2, i2) = tops
            cand = jnp.concatenate(
                [s1[0:1, :] + s2]
                + [s1[a:a + 1, :] + s2[0:SUB, :] for a in range(1, SUB)]
                + [s1[SUB:, :] + s2[0:1, :]], axis=0)
            sc, pos = _topk_rows(cand, PEER_TOPK)
            ca = jnp.where(pos < PEER_TOPK, 0,
                           jnp.where(pos < PEER_TOPK + SUB * (SUB - 1), (pos >> 3) - 1, pos - SUB * SUB))
            cb = jnp.where(pos < PEER_TOPK, pos,
                           jnp.where(pos < PEER_TOPK + SUB * (SUB - 1), pos & (SUB - 1), 0))
            e = _take_rows(i1, ca) * N_KEYS + _take_rows(i2, cb)
            ex = jnp.exp(sc - sc[0:1, :])
            gates.append(ex / jnp.sum(ex, axis=0, keepdims=True))
            experts.append(e)
        idx_ref[c] = jnp.concatenate(experts, axis=0).T
        gate_ref[c * LANES:(c + 1) * LANES, :] = jnp.concatenate(gates, axis=0).T


def _mid(x2, an, bo, gb, wout, g2, wq, keys):
    t_total = x2.shape[0]
    nt = MID_BLOCK
    row = lambda i: (i, 0)
    const2 = lambda i: (0, 0)
    return pl.pallas_call(
        _mid_kernel,
        grid=(t_total // nt,),
        in_specs=[
            pl.BlockSpec((nt, D_MODEL), row),
            pl.BlockSpec((nt, D_A), row),
            pl.BlockSpec((nt, D_B), row),
            pl.BlockSpec((1, D_B), const2),
            pl.BlockSpec((D_MODEL, D_MODEL), const2),
            pl.BlockSpec((1, D_MODEL), const2),
            pl.BlockSpec((D_MODEL, PEER_HEADS * D_KEY), const2),
            pl.BlockSpec((2, N_KEYS, D_KEY // 2), lambda i: (0, 0, 0)),
        ],
        out_specs=(
            pl.BlockSpec((nt, D_MODEL), row),
            pl.BlockSpec((nt, D_MODEL), row),
            pl.BlockSpec((nt // LANES, LANES, N_SLOTS), lambda i: (i, 0, 0)),
            pl.BlockSpec((nt, N_SLOTS), row),
        ),
        out_shape=(
            jax.ShapeDtypeStruct((t_total, D_MODEL), F32),
            jax.ShapeDtypeStruct((t_total, D_MODEL), F32),
            jax.ShapeDtypeStruct((t_total // LANES, LANES, N_SLOTS), I32),
            jax.ShapeDtypeStruct((t_total, N_SLOTS), F32),
        ),
        compiler_params=pltpu.CompilerParams(
            dimension_semantics=("arbitrary",), vmem_limit_bytes=VMEM_LIMIT),
        name="mid",
    )(x2, an, bo, gb, wout, g2, wq, keys)


def _peer_kernel(idx_ref, gate_ref, xn_ref, x1_ref, gf_ref, uv_ref, y_ref, *scratch):
    rows_refs = scratch[:PEER_RING]
    bf_ref, sem_ref = scratch[PEER_RING:]
    nt = xn_ref.shape[0]
    wide = 2 * LANES
    nw = 2 * D_MODEL // wide

    def issue(t, slot, lo=0, hi=N_SLOTS):
        for s in range(lo, hi):
            pltpu.make_async_copy(uv_ref.at[idx_ref[t, s]],
                                  rows_refs[slot].at[s // SUB, :, s % SUB, :],
                                  sem_ref.at[slot]).start(priority=s % 2)

    def wait(slot):
        pltpu.make_async_copy(uv_ref.at[pl.ds(0, N_SLOTS)],
                              rows_refs[slot].reshape(N_SLOTS, NCH, LANES), sem_ref.at[slot]).wait()

    def stage(slot, k):
        for j in range(NCH):
            w = rows_refs[slot][:, j, :, :].reshape(N_SLOTS, LANES)
            bf_ref[k, :, j * LANES:(j + 1) * LANES] = (
                pltpu.bitcast(w & U_HALF, F32).astype(BF16))
            bf_ref[k, :, D_MODEL + j * LANES:D_MODEL + (j + 1) * LANES] = (
                pltpu.bitcast(w << 16, F32).astype(BF16))

    def pair(t0, slots, prefetch):
        for k in range(2):
            wait(slots[k])
            stage(slots[k], k)
        nbatch = 2 * nw
        per = N_SLOTS // (nbatch // 2)
        batches = [(k, b * per, (b + 1) * per) for b in range(nbatch // 2) for k in range(2)]

        def next_batch():
            if prefetch and batches:
                k, lo, hi = batches.pop(0)
                issue(t0 + PEER_RING + k, slots[k], lo, hi)

        x8 = [jnp.broadcast_to(xn_ref[pl.ds(t0 + k, 1), :], (SUB, D_MODEL)).astype(BF16)
              for k in range(2)]
        act = [jnp.zeros((SUB, N_SLOTS), F32) for _ in range(2)]
        for j in range(nw // 2):
            for k in range(2):
                next_batch()
                act[k] = act[k] + lax.dot_general(
                    x8[k][:, j * wide:(j + 1) * wide], bf_ref[k, :, j * wide:(j + 1) * wide],
                    (((1,), (1,)), ((), ())), preferred_element_type=F32)
        w = [(_gelu(act[k]) * gate_ref[pl.ds(t0 + k, 1), :]).astype(BF16) for k in range(2)]
        outs = [[], []]
        for j in range(nw // 2):
            for k in range(2):
                next_batch()
                outs[k].append(jnp.dot(
                    w[k], bf_ref[k, :, D_MODEL + j * wide:D_MODEL + (j + 1) * wide],
                    preferred_element_type=F32)[0:1, :])
        for k in range(2):
            out = jnp.concatenate(outs[k], axis=1)
            y_ref[pl.ds(t0 + k, 1), :] = _rms(x1_ref[pl.ds(t0 + k, 1), :] + out, gf_ref[...])

    def group(g, prefetch):
        for p in range(PEER_RING // 2):
            pair(g * PEER_RING + 2 * p, (2 * p, 2 * p + 1), prefetch)

    for t in range(PEER_RING):
        issue(t, t)
    ngroup = nt // PEER_RING
    lax.fori_loop(0, ngroup - 1, lambda g, c: (group(g, True), c)[1], 0)
    group(ngroup - 1, False)


def _peer(idx, gates, xn, x1, gf, uv, first_token):
    t_total = xn.shape[0] - first_token
    nt = PEER_BLOCK
    b0 = first_token // nt
    row = lambda i: (i + b0, 0)
    return pl.pallas_call(
        _peer_kernel,
        grid=(t_total // nt,),
        in_specs=[
            pl.BlockSpec((None, nt, N_SLOTS), lambda i: (i + b0, 0, 0), memory_space=pltpu.SMEM),
            pl.BlockSpec((nt, N_SLOTS), row),
            pl.BlockSpec((nt, D_MODEL), row),
            pl.BlockSpec((nt, D_MODEL), row),
            pl.BlockSpec((1, D_MODEL), lambda i: (0, 0)),
            pl.BlockSpec(memory_space=pl.ANY),
        ],
        out_specs=pl.BlockSpec((nt, D_MODEL), lambda i: (i, 0)),
        out_shape=jax.ShapeDtypeStruct((t_total, D_MODEL), F32),
        scratch_shapes=[pltpu.VMEM((N_SLOTS // SUB, NCH, SUB, LANES), I32)] * PEER_RING + [
            pltpu.VMEM((2, N_SLOTS, 2 * D_MODEL), BF16),
            pltpu.SemaphoreType.DMA((PEER_RING,))],
        compiler_params=pltpu.CompilerParams(
            dimension_semantics=("arbitrary",), vmem_limit_bytes=VMEM_LIMIT),
        name="peer",
    )(idx, gates, xn, x1, gf, uv)


def _sc_peer(idx, xn, gates, uv2, n_tokens):
    info = plsc.get_sparse_core_info()
    nc, lanes_n = info.num_cores, info.num_lanes
    nw = nc * info.num_subcores
    per = n_tokens // nw
    assert n_tokens % nw == 0
    nchunk = N_SLOTS // SC_ROWS
    qv = D_MODEL // (SC_PASSES * lanes_n)
    c0 = math.sqrt(2.0 / math.pi)
    mesh = plsc.VectorSubcoreMesh(core_axis_name="c", subcore_axis_name="s")

    @functools.partial(
        pl.kernel, mesh=mesh,
        out_type=jax.ShapeDtypeStruct((n_tokens, D_MODEL), F32),
        scratch_types=[
            pltpu.VMEM((nchunk, SC_ROWS), I32),
            pltpu.VMEM((D_MODEL,), F32),
            pltpu.VMEM((N_SLOTS,), F32),
            pltpu.VMEM((SC_ROWS, D_MODEL), I32),
            pltpu.VMEM((SC_ROWS, D_MODEL), I32),
            pltpu.VMEM((D_MODEL,), F32),
            pltpu.VMEM((SC_ROWS, lanes_n), F32),
            pltpu.VMEM((SC_ROWS,), F32),
            pltpu.SemaphoreType.DMA,
            pltpu.SemaphoreType.DMA,
        ],
        compiler_params=pltpu.CompilerParams(needs_layout_passes=False),
        name="peer_sc",
    )
    def sc_kernel(idx_hbm, xn_hbm, gate_hbm, uv_hbm, out_hbm, idx_v, x_v, g_v, rows0, rows1,
                  out_v, part_v, w_v, sem0, sem1):
        wid = lax.axis_index("s") * nc + lax.axis_index("c")
        lane_ids = lax.iota(I32, lanes_n)
        zero = jnp.zeros((lanes_n,), F32)
        bufs = ((rows0, sem0), (rows1, sem1))

        def vec(q, j):
            return pl.ds((q * qv + j) * lanes_n, lanes_n)

        def gather(c, b):
            rows, sem = bufs[b]
            return pltpu.make_async_copy(uv_hbm.at[idx_v.at[c]], rows, sem)

        def chunk(c, b):
            rows, _ = bufs[b]
            def ustep(j, acc):
                xj = x_v[pl.ds(j * lanes_n, lanes_n)]
                return tuple(
                    acc[r] + plsc.bitcast(rows[r, pl.ds(j * lanes_n, lanes_n)] & U_HALF, F32) * xj
                    for r in range(SC_ROWS))

            acc = lax.fori_loop(0, D_MODEL // lanes_n, ustep, (zero,) * SC_ROWS)
            for r in range(SC_ROWS):
                part_v[r, :] = acc[r]
            act = zero
            for l in range(lanes_n):
                act = act + plsc.load_gather(part_v, [lane_ids, jnp.full((lanes_n,), l, I32)])
            z = c0 * (act + 0.044715 * (act * act * act))
            tanh_z = 1.0 - 2.0 / (jnp.exp(2.0 * z) + 1.0)
            w_v[...] = 0.5 * act * (1.0 + tanh_z) * g_v[pl.ds(c * SC_ROWS, SC_ROWS)]
            for q in range(SC_PASSES):
                o = tuple(out_v[vec(q, j)] for j in range(qv))

                def vrow(r, o, q=q):
                    wr = plsc.load_gather(w_v, [jnp.full((lanes_n,), r, I32)])
                    return tuple(o[j] + wr * plsc.bitcast(rows[r, vec(q, j)] << 16, F32)
                                 for j in range(qv))

                o = lax.fori_loop(0, SC_ROWS, vrow, o)
                for j in range(qv):
                    out_v[vec(q, j)] = o[j]

        @pl.loop(0, per)
        def _(i):
            t = wid * per + i
            pltpu.sync_copy(idx_hbm.at[t], idx_v)
            pltpu.sync_copy(xn_hbm.at[t], x_v)
            pltpu.sync_copy(gate_hbm.at[t], g_v)
            for j in range(D_MODEL // lanes_n):
                out_v[pl.ds(j * lanes_n, lanes_n)] = zero
            gather(0, 0).start()

            @pl.loop(0, nchunk, step=2)
            def _(c):
                gather(c + 1, 1).start()
                gather(c, 0).wait()
                chunk(c, 0)

                @pl.when(c + 2 < nchunk)
                def _():
                    gather(c + 2, 0).start()

                gather(c + 1, 1).wait()
                chunk(c + 1, 1)

            pltpu.sync_copy(out_v, out_hbm.at[t])

    return sc_kernel(idx.reshape(-1, nchunk, SC_ROWS), xn, gates, uv2)


def _residual_norm_kernel(x1_ref, o_ref, gf_ref, y_ref):
    y_ref[...] = _rms(x1_ref[...] + o_ref[...], gf_ref[...])


def _residual_norm(x1, out, gf):
    n = out.shape[0]
    nt = PEER_BLOCK
    row = lambda i: (i, 0)
    return pl.pallas_call(
        _residual_norm_kernel,
        grid=(n // nt,),
        in_specs=[pl.BlockSpec((nt, D_MODEL), row), pl.BlockSpec((nt, D_MODEL), row),
                  pl.BlockSpec((1, D_MODEL), lambda i: (0, 0))],
        out_specs=pl.BlockSpec((nt, D_MODEL), row),
        out_shape=jax.ShapeDtypeStruct((n, D_MODEL), F32),
        compiler_params=pltpu.CompilerParams(dimension_semantics=("arbitrary",)),
        name="residual_norm",
    )(x1, out, gf)


def _rope_tables(seq):
    pos = jnp.arange(seq, dtype=F32)
    inv = 1.0 / (ROPE_THETA ** (jnp.arange(0, HEAD_DIM, 2, dtype=F32) / HEAD_DIM))
    ang = pos[:, None] * inv[None, :]
    cos, sin = jnp.cos(ang), jnp.sin(ang)
    cosf = jnp.tile(jnp.concatenate([cos, cos], axis=1), (1, B_HEADS))
    sins = jnp.tile(jnp.concatenate([-sin, sin], axis=1), (1, B_HEADS))
    return cosf, sins


def kernel(x, norm1_g, w_in, ln_v_g, ln_v_b, w_spatial, b_spatial, out_norm_a_g, out_norm_b_g,
           w_out, norm2_g, w_query, sub_keys, expert_u, expert_v, final_norm_g):
    batch, seq, _ = x.shape
    assert w_in.shape[0] == 1 and seq % (16 * QBLK) == 0 and seq % IN_BLOCK == 0
    row = lambda g: g.reshape(1, -1).astype(F32)

    ws = w_spatial[0].astype(BF16)
    ws_cat = jnp.concatenate([ws[0::2], ws[1::2]], axis=2)
    bs_full = jnp.repeat(b_spatial[0].T, A_GROUP_DIM, axis=1)
    cosf, sins = _rope_tables(seq)
    win, wout, wq = w_in[0].astype(BF16), w_out[0].astype(BF16), w_query[0].astype(BF16)
    keys = sub_keys[0].astype(BF16)
    gf = row(final_norm_g)
    half = lambda t: lax.bitcast_convert_type(t.astype(BF16), jnp.uint16).astype(jnp.uint32)
    uv2 = lax.bitcast_convert_type((half(expert_u[0]) << 16) | half(expert_v[0]), I32)
    uv = uv2.reshape(-1, NCH, LANES)

    nchunk = PIPE_CHUNKS if batch % PIPE_CHUNKS == 0 else 1
    cb = batch // nchunk
    pieces = []
    for ci in range(nchunk):
        x2 = x[ci * cb:(ci + 1) * cb].reshape(cb * seq, D_MODEL)
        an, q1, k1, v1, q4, k4, v4, q16, k16, v16 = _in_proj(
            x2, row(norm1_g[0]), win, row(ln_v_g[0]), row(ln_v_b[0]),
            ws_cat, bs_full, row(out_norm_a_g[0]), cosf, sins, cb, seq)
        bo = _attention(q1, k1, v1, q4, k4, v4, q16, k16, v16, cb, seq).reshape(cb * seq, D_B)
        x1, xn, idx, gates = _mid(x2, an, bo, row(out_norm_b_g[0]), wout, row(norm2_g[0]), wq, keys)
        n_sc = (cb * seq) * SC_SHARE[0] // SC_SHARE[1]
        assert n_sc % PEER_BLOCK == 0
        out_sc = _sc_peer(idx.reshape(cb * seq, N_SLOTS), xn, gates, uv2, n_sc)
        y_tc = _peer(idx, gates, xn, x1, gf, uv, n_sc)
        pieces += [_residual_norm(x1, out_sc, gf), y_tc]
    return jnp.concatenate(pieces, axis=0).reshape(batch, seq, D_MODEL)
```

```python
import functools
import math

import jax
import jax.numpy as jnp
from jax import lax
from jax.experimental import pallas as pl
from jax.experimental.pallas import tpu as pltpu
from jax.experimental.pallas import tpu_sc as plsc

F32 = jnp.float32
BF16 = jnp.bfloat16
I32 = jnp.int32

D_MODEL = 1024
D_A = 512
D_B = 512
A_GROUPS = 8
A_GROUP_DIM = 64
CHUNK = 128
B_HEADS = 8
HEAD_DIM = 64
DILATIONS = (1, 4, 16)
HALF_WINDOW = 64
ROPE_THETA = 10000.0
D_IN = 2 * D_A + 3 * D_B
N_KEYS = 128
PEER_HEADS = 8
PEER_TOPK = 16
D_KEY = 256
N_SLOTS = PEER_HEADS * PEER_TOPK
EPS = 1e-6
NEG_BIG = -1e30

LANES = 128
SUB = 8
NCH = D_MODEL // LANES
U_HALF = -65536
QBLK = 128
ATTN_UNROLL = 4
RES16_PITCH = 24
IN_BLOCK = 512
MID_BLOCK = 256
PEER_BLOCK = 128
PEER_RING = 4
SC_SHARE = (39, 64)
PIPE_CHUNKS = 8
SC_ROWS = 16
SC_PASSES = 4
VMEM_LIMIT = 48 * 1024 * 1024


def _gelu(x):
    c = math.sqrt(2.0 / math.pi)
    return 0.5 * x * (1.0 + jnp.tanh(c * (x + 0.044715 * (x * x * x))))


def _rms(x, g):
    return x * lax.rsqrt(jnp.mean(x * x, axis=-1, keepdims=True) + EPS) * g


def _in_proj_kernel(x_ref, g1_ref, win_ref, lng_ref, lnb_ref, ws_ref, bs_ref, ga_ref,
                    cos_ref, sin_ref,
                    an_ref, q1_ref, k1_ref, v1_ref, q4_ref, k4_ref, v4_ref,
                    q16_ref, k16_ref, v16_ref, slab_ref):
    nt = x_ref.shape[0]
    h = _rms(x_ref[...], g1_ref[...]).astype(BF16)
    proj = jnp.dot(h, win_ref[...], preferred_element_type=F32)

    u = _gelu(proj[:, :D_A])
    v = _gelu(proj[:, D_A:2 * D_A])
    mu = jnp.mean(v, axis=-1, keepdims=True)
    vc = v - mu
    var = jnp.mean(vc * vc, axis=-1, keepdims=True)
    vln = (vc * lax.rsqrt(var + EPS) * lng_ref[...] + lnb_ref[...]).astype(BF16)
    lane = lax.broadcasted_iota(I32, (CHUNK, LANES), 1)
    lo = lane < A_GROUP_DIM
    zero = jnp.zeros((CHUNK, LANES), BF16)
    chunks = []
    for c in range(nt // CHUNK):
        cols = []
        for j in range(A_GROUPS // 2):
            vv = vln[c * CHUNK:(c + 1) * CHUNK, j * LANES:(j + 1) * LANES]
            rhs = jnp.concatenate([jnp.where(lo, vv, zero), jnp.where(lo, zero, vv)], axis=0)
            cols.append(jnp.dot(ws_ref[j], rhs, preferred_element_type=F32))
        chunks.append(jnp.concatenate(cols, axis=1) + bs_ref[...])
    mixed = jnp.concatenate(chunks, axis=0)
    an_ref[...] = _rms(u * mixed, ga_ref[...]).astype(BF16)

    cosf = cos_ref[...]
    sins = sin_ref[...]
    lane_b = lax.broadcasted_iota(I32, (nt, D_B), 1)
    first_half = (lane_b % HEAD_DIM) < (HEAD_DIM // 2)

    def rope(t):
        partner = jnp.where(first_half, pltpu.roll(t, D_B - HEAD_DIM // 2, 1),
                            pltpu.roll(t, HEAD_DIM // 2, 1))
        return t * cosf + partner * sins

    q = rope(proj[:, 2 * D_A:2 * D_A + D_B]) * (HEAD_DIM ** -0.5)
    k = rope(proj[:, 2 * D_A + D_B:2 * D_A + 2 * D_B])
    vv = proj[:, 2 * D_A + 2 * D_B:]
    q1_ref[...] = q.astype(BF16)
    k1_ref[...] = k.astype(BF16)
    v1_ref[...] = vv.astype(BF16)

    nslab = D_B // LANES
    for a, t in enumerate((q, k, vv)):
        for s in range(nslab):
            slab_ref[a * nslab + s] = t[:, s * LANES:(s + 1) * LANES]
    for d, outs in ((4, (q4_ref, k4_ref, v4_ref)), (16, (q16_ref, k16_ref, v16_ref))):
        rows = nt // d
        for a, o_ref in enumerate(outs):
            for r in range(d):
                for s in range(nslab):
                    o_ref[r, :, s * LANES:(s + 1) * LANES] = (
                        slab_ref[a * nslab + s, pl.ds(r, rows, stride=d), :].astype(BF16))


def _in_proj(x2, g1, win, lng, lnb, ws_cat, bs_full, ga, cosf, sins, batch, seq):
    t_total = x2.shape[0]
    nt = IN_BLOCK
    nb = seq // nt
    grid = (t_total // nt,)
    row = lambda i: (i, 0)
    const2 = lambda i: (0, 0)
    tok_bf = jax.ShapeDtypeStruct((t_total, D_B), BF16)
    out_shape = (
        jax.ShapeDtypeStruct((t_total, D_A), BF16),
        tok_bf, tok_bf, tok_bf,
        *(jax.ShapeDtypeStruct((batch, 4, seq // 4, D_B), BF16),) * 3,
        *(jax.ShapeDtypeStruct((batch, 16, seq // 16, D_B), BF16),) * 3,
    )
    res4 = pl.BlockSpec((None, 4, nt // 4, D_B), lambda i: (i // nb, 0, i % nb, 0))
    res16 = pl.BlockSpec((None, 16, nt // 16, D_B), lambda i: (i // nb, 0, i % nb, 0))
    tok_spec = pl.BlockSpec((nt, D_B), row)
    return pl.pallas_call(
        _in_proj_kernel,
        grid=grid,
        in_specs=[
            pl.BlockSpec((nt, D_MODEL), row),
            pl.BlockSpec((1, D_MODEL), const2),
            pl.BlockSpec((D_MODEL, D_IN), const2),
            pl.BlockSpec((1, D_A), const2),
            pl.BlockSpec((1, D_A), const2),
            pl.BlockSpec((A_GROUPS // 2, CHUNK, 2 * CHUNK), lambda i: (0, 0, 0)),
            pl.BlockSpec((CHUNK, D_A), const2),
            pl.BlockSpec((1, D_A), const2),
            pl.BlockSpec((nt, D_B), lambda i: (i % nb, 0)),
            pl.BlockSpec((nt, D_B), lambda i: (i % nb, 0)),
        ],
        out_specs=(pl.BlockSpec((nt, D_A), row), tok_spec, tok_spec, tok_spec,
                   res4, res4, res4, res16, res16, res16),
        out_shape=out_shape,
        scratch_shapes=[pltpu.VMEM((3 * D_B // LANES, nt, LANES), F32)],
        compiler_params=pltpu.CompilerParams(
            dimension_semantics=("arbitrary",), vmem_limit_bytes=VMEM_LIMIT),
        name="in_proj",
    )(x2, g1, win, lng, lnb, ws_cat, bs_full, ga, cosf, sins)


def _attn_kernel(q1_ref, k1_ref, v1_ref, q4_ref, k4_ref, v4_ref, q16_ref, k16_ref, v16_ref,
                 o_ref, out_ref, lse_ref, out16_ref, lse16_ref):
    seq = o_ref.shape[0]
    lane = lax.broadcasted_iota(I32, (QBLK, LANES), 1)
    head0 = lane < HEAD_DIM
    branches = ((1, q1_ref, k1_ref, v1_ref), (4, q4_ref, k4_ref, v4_ref),
                (16, q16_ref, k16_ref, v16_ref))
    for bi, (d, q_ref, k_ref, v_ref) in enumerate(branches):
        length = seq // d
        nblk = length // QBLK
        win = min(2 * QBLK, length)
        diff = (lax.broadcasted_iota(I32, (QBLK, win), 1)
                - lax.broadcasted_iota(I32, (QBLK, win), 0))

        def block(blk, carry, d=d, bi=bi, q_ref=q_ref, k_ref=k_ref, v_ref=v_ref,
                  length=length, nblk=nblk, win=win, diff=diff):
            r = blk // nblk
            i0 = pl.multiple_of((blk % nblk) * QBLK, QBLK)
            w0 = pl.multiple_of(jnp.clip(i0 - HALF_WINDOW, 0, length - win), HALF_WINDOW)
            qb = q_ref[r, pl.ds(i0, QBLK), :]
            kw = k_ref[r, pl.ds(w0, win), :]
            vw = v_ref[r, pl.ds(w0, win), :]
            rel = diff + (w0 - i0)
            valid = (rel >= -HALF_WINDOW) & (rel <= HALF_WINDOW)
            zero = jnp.zeros_like(qb)
            qq = jnp.concatenate([jnp.where(head0, qb, zero), jnp.where(head0, zero, qb)], axis=0)
            s = lax.dot_general(qq, kw, (((1,), (1,)), ((), ())), preferred_element_type=F32)
            s = jnp.where(jnp.concatenate([valid, valid], axis=0), s, NEG_BIG)
            m = jnp.max(s, axis=1, keepdims=True)
            p = jnp.exp(s - m)
            l = jnp.sum(p, axis=1, keepdims=True)
            pv = jnp.dot(p.astype(BF16), vw, preferred_element_type=F32) / l
            ml = m + jnp.log(l)
            out = jnp.where(head0, pv[:QBLK], pv[QBLK:])
            lse = jnp.where(head0, ml[:QBLK], ml[QBLK:])
            if d == 1:
                out_ref[0, pl.ds(i0, QBLK), :] = out
                lse_ref[0, pl.ds(i0, QBLK), :] = lse
            elif d == 4:
                rows = pl.ds(i0 * d + r, QBLK, stride=d)
                out_ref[1, rows, :] = out
                lse_ref[1, rows, :] = lse
            else:
                rows = pl.ds(r, QBLK, stride=RES16_PITCH)
                out16_ref[rows, :] = out
                lse16_ref[rows, :] = lse
            return carry

        lax.fori_loop(0, d * nblk, block, 0, unroll=ATTN_UNROLL)

    groups = QBLK // 16

    def merge(c, carry):
        rows = pl.ds(pl.multiple_of(c * QBLK, QBLK), QBLK)
        base = pl.multiple_of(c * (groups * RES16_PITCH), SUB)
        pieces = [pl.ds(base + g * RES16_PITCH, 16) for g in range(groups)]
        o3 = jnp.concatenate([out16_ref[pc, :] for pc in pieces], axis=0)
        e3 = jnp.concatenate([lse16_ref[pc, :] for pc in pieces], axis=0)
        e1, e2 = lse_ref[0, rows, :], lse_ref[1, rows, :]
        mx = jnp.maximum(jnp.maximum(e1, e2), e3)
        w1, w2, w3 = jnp.exp(e1 - mx), jnp.exp(e2 - mx), jnp.exp(e3 - mx)
        num = w1 * out_ref[0, rows, :] + w2 * out_ref[1, rows, :] + w3 * o3
        o_ref[rows, :] = num / (w1 + w2 + w3)
        return carry

    lax.fori_loop(0, seq // QBLK, merge, 0)


def _attention(q1, k1, v1, q4, k4, v4, q16, k16, v16, batch, seq):
    npair = D_B // LANES
    nat = pl.BlockSpec((None, 1, seq, LANES), lambda b, p: (b, 0, 0, p))
    r4 = pl.BlockSpec((None, 4, seq // 4, LANES), lambda b, p: (b, 0, 0, p))
    r16 = pl.BlockSpec((None, 16, seq // 16, LANES), lambda b, p: (b, 0, 0, p))
    q1, k1, v1 = (t.reshape(batch, 1, seq, D_B) for t in (q1, k1, v1))
    return pl.pallas_call(
        _attn_kernel,
        grid=(batch, npair),
        in_specs=[nat, nat, nat, r4, r4, r4, r16, r16, r16],
        out_specs=pl.BlockSpec((None, seq, LANES), lambda b, p: (b, 0, p)),
        out_shape=jax.ShapeDtypeStruct((batch, seq, D_B), F32),
        scratch_shapes=[pltpu.VMEM((2, seq, LANES), F32)] * 2
        + [pltpu.VMEM((seq // 16 * RES16_PITCH, LANES), F32)] * 2,
        compiler_params=pltpu.CompilerParams(
            dimension_semantics=("arbitrary", "arbitrary"), vmem_limit_bytes=VMEM_LIMIT),
        name="dilated_attn",
    )(q1, k1, v1, q4, k4, v4, q16, k16, v16)


def _topk_rows(s, k):
    n = s.shape[0]
    iota = lax.broadcasted_iota(I32, s.shape, 0).astype(F32)
    vals, idxs = [], []
    for _ in range(k):
        m = jnp.max(s, axis=0, keepdims=True)
        i = jnp.min(jnp.where(s == m, iota, float(n)), axis=0, keepdims=True)
        vals.append(m)
        idxs.append(i)
        s = jnp.where(iota == i, -jnp.inf, s)
    return jnp.concatenate(vals, axis=0), jnp.concatenate(idxs, axis=0).astype(I32)


def _take_rows(table, sel):
    out = jnp.zeros(sel.shape, table.dtype)
    for a in range(table.shape[0]):
        out = jnp.where(sel == a, table[a:a + 1, :], out)
    return out


def _mid_kernel(x_ref, an_ref, bo_ref, gb_ref, wout_ref, g2_ref, wq_ref, keys_ref,
                x1_ref, xn_ref, idx_ref, gate_ref):
    nt = x_ref.shape[0]
    bn = _rms(bo_ref[...], gb_ref[...]).astype(BF16)
    x1 = (x_ref[...]
          + jnp.dot(an_ref[...], wout_ref[:D_A, :], preferred_element_type=F32)
          + jnp.dot(bn, wout_ref[D_A:, :], preferred_element_type=F32))
    x1_ref[...] = x1
    xn = _rms(x1, g2_ref[...])
    xn_ref[...] = xn
    q = jnp.dot(xn.astype(BF16), wq_ref[...], preferred_element_type=F32).astype(BF16)
    keys = (keys_ref[0], keys_ref[1])
    half = D_KEY // 2
    for c in range(nt // LANES):
        qc = q[c * LANES:(c + 1) * LANES, :]
        experts, gates = [], []
        for h in range(PEER_HEADS):
            tops = []
            for p in range(2):
                qhp = qc[:, (2 * h + p) * half:(2 * h + p + 1) * half]
                s = lax.dot_general(keys[p], qhp, (((1,), (1,)), ((), ())),
                                    preferred_element_type=F32)
                tops.append(_topk_rows(s, PEER_TOPK))
            (s1, i1), (s2, i2) = tops
            cand = jnp.concatenate(
                [s1[0:1, :] + s2]
                + [s1[a:a + 1, :] + s2[0:SUB, :] for a in range(1, SUB)]
                + [s1[SUB:, :] + s2[0:1, :]], axis=0)
            sc, pos = _topk_rows(cand, PEER_TOPK)
            ca = jnp.where(pos < PEER_TOPK, 0,
                           jnp.where(pos < PEER_TOPK + SUB * (SUB - 1), (pos >> 3) - 1, pos - SUB * SUB))
            cb = jnp.where(pos < PEER_TOPK, pos,
                           jnp.where(pos < PEER_TOPK + SUB * (SUB - 1), pos & (SUB - 1), 0))
            e = _take_rows(i1, ca) * N_KEYS + _take_rows(i2, cb)
            ex = jnp.exp(sc - sc[0:1, :])
            gates.append(ex / jnp.sum(ex, axis=0, keepdims=True))
            experts.append(e)
        idx_ref[c] = jnp.concatenate(experts, axis=0).T
        gate_ref[c * LANES:(c + 1) * LANES, :] = jnp.concatenate(gates, axis=0).T


def _mid(x2, an, bo, gb, wout, g2, wq, keys):
    t_total = x2.shape[0]
    nt = MID_BLOCK
    row = lambda i: (i, 0)
    const2 = lambda i: (0, 0)
    return pl.pallas_call(
        _mid_kernel,
        grid=(t_total // nt,),
        in_specs=[
            pl.BlockSpec((nt, D_MODEL), row),
            pl.BlockSpec((nt, D_A), row),
            pl.BlockSpec((nt, D_B), row),
            pl.BlockSpec((1, D_B), const2),
            pl.BlockSpec((D_MODEL, D_MODEL), const2),
            pl.BlockSpec((1, D_MODEL), const2),
            pl.BlockSpec((D_MODEL, PEER_HEADS * D_KEY), const2),
            pl.BlockSpec((2, N_KEYS, D_KEY // 2), lambda i: (0, 0, 0)),
        ],
        out_specs=(
            pl.BlockSpec((nt, D_MODEL), row),
            pl.BlockSpec((nt, D_MODEL), row),
            pl.BlockSpec((nt // LANES, LANES, N_SLOTS), lambda i: (i, 0, 0)),
            pl.BlockSpec((nt, N_SLOTS), row),
        ),
        out_shape=(
            jax.ShapeDtypeStruct((t_total, D_MODEL), F32),
            jax.ShapeDtypeStruct((t_total, D_MODEL), F32),
            jax.ShapeDtypeStruct((t_total // LANES, LANES, N_SLOTS), I32),
            jax.ShapeDtypeStruct((t_total, N_SLOTS), F32),
        ),
        compiler_params=pltpu.CompilerParams(
            dimension_semantics=("arbitrary",), vmem_limit_bytes=VMEM_LIMIT),
        name="mid",
    )(x2, an, bo, gb, wout, g2, wq, keys)


def _peer_kernel(idx_ref, gate_ref, xn_ref, x1_ref, gf_ref, uv_ref, y_ref, *scratch):
    rows_refs = scratch[:PEER_RING]
    bf_ref, sem_ref = scratch[PEER_RING:]
    nt = xn_ref.shape[0]
    wide = 2 * LANES
    nw = 2 * D_MODEL // wide

    def issue(t, slot, lo=0, hi=N_SLOTS):
        for s in range(lo, hi):
            pltpu.make_async_copy(uv_ref.at[idx_ref[t, s]],
                                  rows_refs[slot].at[s // SUB, :, s % SUB, :],
                                  sem_ref.at[slot]).start(priority=s % 2)

    def wait(slot):
        pltpu.make_async_copy(uv_ref.at[pl.ds(0, N_SLOTS)],
                              rows_refs[slot].reshape(N_SLOTS, NCH, LANES), sem_ref.at[slot]).wait()

    def stage(slot, k):
        for j in range(NCH):
            w = rows_refs[slot][:, j, :, :].reshape(N_SLOTS, LANES)
            bf_ref[k, :, j * LANES:(j + 1) * LANES] = (
                pltpu.bitcast(w & U_HALF, F32).astype(BF16))
            bf_ref[k, :, D_MODEL + j * LANES:D_MODEL + (j + 1) * LANES] = (
                pltpu.bitcast(w << 16, F32).astype(BF16))

    def pair(t0, slots, prefetch):
        for k in range(2):
            wait(slots[k])
            stage(slots[k], k)
        nbatch = 2 * nw
        per = N_SLOTS // (nbatch // 2)
        batches = [(k, b * per, (b + 1) * per) for b in range(nbatch // 2) for k in range(2)]

        def next_batch():
            if prefetch and batches:
                k, lo, hi = batches.pop(0)
                issue(t0 + PEER_RING + k, slots[k], lo, hi)

        x8 = [jnp.broadcast_to(xn_ref[pl.ds(t0 + k, 1), :], (SUB, D_MODEL)).astype(BF16)
              for k in range(2)]
        act = [jnp.zeros((SUB, N_SLOTS), F32) for _ in range(2)]
        for j in range(nw // 2):
            for k in range(2):
                next_batch()
                act[k] = act[k] + lax.dot_general(
                    x8[k][:, j * wide:(j + 1) * wide], bf_ref[k, :, j * wide:(j + 1) * wide],
                    (((1,), (1,)), ((), ())), preferred_element_type=F32)
        w = [(_gelu(act[k]) * gate_ref[pl.ds(t0 + k, 1), :]).astype(BF16) for k in range(2)]
        outs = [[], []]
        for j in range(nw // 2):
            for k in range(2):
                next_batch()
                outs[k].append(jnp.dot(
                    w[k], bf_ref[k, :, D_MODEL + j * wide:D_MODEL + (j + 1) * wide],
                    preferred_element_type=F32)[0:1, :])
        for k in range(2):
            out = jnp.concatenate(outs[k], axis=1)
            y_ref[pl.ds(t0 + k, 1), :] = _rms(x1_ref[pl.ds(t0 + k, 1), :] + out, gf_ref[...])

    def group(g, prefetch):
        for p in range(PEER_RING // 2):
            pair(g * PEER_RING + 2 * p, (2 * p, 2 * p + 1), prefetch)

    for t in range(PEER_RING):
        issue(t, t)
    ngroup = nt // PEER_RING
    lax.fori_loop(0, ngroup - 1, lambda g, c: (group(g, True), c)[1], 0)
    group(ngroup - 1, False)


def _peer(idx, gates, xn, x1, gf, uv, first_token):
    t_total = xn.shape[0] - first_token
    nt = PEER_BLOCK
    b0 = first_token // nt
    row = lambda i: (i + b0, 0)
    return pl.pallas_call(
        _peer_kernel,
        grid=(t_total // nt,),
        in_specs=[
            pl.BlockSpec((None, nt, N_SLOTS), lambda i: (i + b0, 0, 0), memory_space=pltpu.SMEM),
            pl.BlockSpec((nt, N_SLOTS), row),
            pl.BlockSpec((nt, D_MODEL), row),
            pl.BlockSpec((nt, D_MODEL), row),
            pl.BlockSpec((1, D_MODEL), lambda i: (0, 0)),
            pl.BlockSpec(memory_space=pl.ANY),
        ],
        out_specs=pl.BlockSpec((nt, D_MODEL), lambda i: (i, 0)),
        out_shape=jax.ShapeDtypeStruct((t_total, D_MODEL), F32),
        scratch_shapes=[pltpu.VMEM((N_SLOTS // SUB, NCH, SUB, LANES), I32)] * PEER_RING + [
            pltpu.VMEM((2, N_SLOTS, 2 * D_MODEL), BF16),
            pltpu.SemaphoreType.DMA((PEER_RING,))],
        compiler_params=pltpu.CompilerParams(
            dimension_semantics=("arbitrary",), vmem_limit_bytes=VMEM_LIMIT),
        name="peer",
    )(idx, gates, xn, x1, gf, uv)


def _sc_peer(idx, xn, gates, uv2, n_tokens):
    info = plsc.get_sparse_core_info()
    nc, lanes_n = info.num_cores, info.num_lanes
    nw = nc * info.num_subcores
    per = n_tokens // nw
    assert n_tokens % nw == 0 and per % 2 == 0
    nchunk = N_SLOTS // SC_ROWS
    qv = D_MODEL // (SC_PASSES * lanes_n)
    c0 = math.sqrt(2.0 / math.pi)
    mesh = plsc.VectorSubcoreMesh(core_axis_name="c", subcore_axis_name="s")
    dma = pltpu.SemaphoreType.DMA

    @functools.partial(
        pl.kernel, mesh=mesh,
        out_type=jax.ShapeDtypeStruct((n_tokens, D_MODEL), F32),
        scratch_types=[
            [pltpu.VMEM((nchunk, SC_ROWS), I32)] * 2,
            [pltpu.VMEM((D_MODEL,), F32)] * 2,
            [pltpu.VMEM((N_SLOTS,), F32)] * 2,
            [pltpu.VMEM((D_MODEL,), F32)] * 2,
            [pltpu.VMEM((SC_ROWS, D_MODEL), I32)] * 2,
            pltpu.VMEM((SC_ROWS, lanes_n), F32),
            pltpu.VMEM((SC_ROWS,), F32),
            [dma] * 2, [dma] * 2, [dma] * 2, [dma] * 2, [dma] * 2,
        ],
        compiler_params=pltpu.CompilerParams(needs_layout_passes=False),
        name="peer_sc",
    )
    def sc_kernel(idx_hbm, xn_hbm, gate_hbm, uv_hbm, out_hbm, idx_v, x_v, g_v, out_v, rows_v,
                  part_v, w_v, row_sem, idx_sem, x_sem, g_sem, out_sem):
        wid = lax.axis_index("s") * nc + lax.axis_index("c")
        t0 = wid * per
        lane_ids = lax.iota(I32, lanes_n)
        zero = jnp.zeros((lanes_n,), F32)

        def vec(q, j):
            return pl.ds((q * qv + j) * lanes_n, lanes_n)

        def gather(p, c, b):
            return pltpu.make_async_copy(uv_hbm.at[idx_v[p].at[c]], rows_v[b], row_sem[b])

        def inputs(tok, p):
            return (pltpu.make_async_copy(idx_hbm.at[tok], idx_v[p], idx_sem[p]),
                    pltpu.make_async_copy(xn_hbm.at[tok], x_v[p], x_sem[p]),
                    pltpu.make_async_copy(gate_hbm.at[tok], g_v[p], g_sem[p]))

        def result(tok, p):
            return pltpu.make_async_copy(out_v[p], out_hbm.at[tok], out_sem[p])

        def chunk(p, c, b):
            rows = rows_v[b]

            def ustep(j, acc):
                xj = x_v[p][pl.ds(j * lanes_n, lanes_n)]
                return tuple(
                    acc[r] + plsc.bitcast(rows[r, pl.ds(j * lanes_n, lanes_n)] & U_HALF, F32) * xj
                    for r in range(SC_ROWS))

            acc = lax.fori_loop(0, D_MODEL // lanes_n, ustep, (zero,) * SC_ROWS)
            for r in range(SC_ROWS):
                part_v[r, :] = acc[r]
            cols = [plsc.load_gather(part_v, [lane_ids, jnp.full((lanes_n,), l, I32)])
                    for l in range(lanes_n)]
            while len(cols) > 1:
                cols = [cols[i] + cols[i + 1] for i in range(0, len(cols), 2)]
            act = cols[0]
            z = c0 * (act + 0.044715 * (act * act * act))
            tanh_z = 1.0 - 2.0 / (jnp.exp(2.0 * z) + 1.0)
            w_v[...] = 0.5 * act * (1.0 + tanh_z) * g_v[p][pl.ds(c * SC_ROWS, SC_ROWS)]
            for q in range(SC_PASSES):
                o = tuple(out_v[p][vec(q, j)] for j in range(qv))

                def vrow(r, o, q=q):
                    wr = plsc.load_gather(w_v, [jnp.full((lanes_n,), r, I32)])
                    return tuple(o[j] + wr * plsc.bitcast(rows[r, vec(q, j)] << 16, F32)
                                 for j in range(qv))

                o = lax.fori_loop(0, SC_ROWS, vrow, o)
                for j in range(qv):
                    out_v[p][vec(q, j)] = o[j]

        def token(i, p):
            tok = t0 + i
            more = i + 1 < per

            @pl.when(more)
            def _():
                for cp in inputs(tok + 1, 1 - p):
                    cp.start()

            @pl.when(i >= 2)
            def _():
                result(tok - 2, p).wait()

            for j in range(D_MODEL // lanes_n):
                out_v[p][pl.ds(j * lanes_n, lanes_n)] = zero

            @pl.loop(0, nchunk, step=2)
            def _(c):
                gather(p, c + 1, 1).start()
                gather(p, c, 0).wait()
                chunk(p, c, 0)

                @pl.when(c + 2 < nchunk)
                def _():
                    gather(p, c + 2, 0).start()

                @pl.when(jnp.logical_and(c + 2 >= nchunk, more))
                def _():
                    for cp in inputs(tok + 1, 1 - p):
                        cp.wait()
                    gather(1 - p, 0, 0).start()

                gather(p, c + 1, 1).wait()
                chunk(p, c + 1, 1)

            result(tok, p).start()

        for cp in inputs(t0, 0):
            cp.start()
        for cp in inputs(t0, 0):
            cp.wait()
        gather(0, 0, 0).start()

        @pl.loop(0, per, step=2)
        def _(i):
            token(i, 0)
            token(i + 1, 1)

        result(t0 + per - 2, 0).wait()
        result(t0 + per - 1, 1).wait()

    return sc_kernel(idx.reshape(-1, nchunk, SC_ROWS), xn, gates, uv2)


def _residual_norm_kernel(x1_ref, o_ref, gf_ref, y_ref):
    y_ref[...] = _rms(x1_ref[...] + o_ref[...], gf_ref[...])


def _residual_norm(x1, out, gf):
    n = out.shape[0]
    nt = PEER_BLOCK
    row = lambda i: (i, 0)
    return pl.pallas_call(
        _residual_norm_kernel,
        grid=(n // nt,),
        in_specs=[pl.BlockSpec((nt, D_MODEL), row), pl.BlockSpec((nt, D_MODEL), row),
                  pl.BlockSpec((1, D_MODEL), lambda i: (0, 0))],
        out_specs=pl.BlockSpec((nt, D_MODEL), row),
        out_shape=jax.ShapeDtypeStruct((n, D_MODEL), F32),
        compiler_params=pltpu.CompilerParams(dimension_semantics=("arbitrary",)),
        name="residual_norm",
    )(x1, out, gf)


def _rope_tables(seq):
    pos = jnp.arange(seq, dtype=F32)
    inv = 1.0 / (ROPE_THETA ** (jnp.arange(0, HEAD_DIM, 2, dtype=F32) / HEAD_DIM))
    ang = pos[:, None] * inv[None, :]
    cos, sin = jnp.cos(ang), jnp.sin(ang)
    cosf = jnp.tile(jnp.concatenate([cos, cos], axis=1), (1, B_HEADS))
    sins = jnp.tile(jnp.concatenate([-sin, sin], axis=1), (1, B_HEADS))
    return cosf, sins


def kernel(x, norm1_g, w_in, ln_v_g, ln_v_b, w_spatial, b_spatial, out_norm_a_g, out_norm_b_g,
           w_out, norm2_g, w_query, sub_keys, expert_u, expert_v, final_norm_g):
    batch, seq, _ = x.shape
    assert w_in.shape[0] == 1 and seq % (16 * QBLK) == 0 and seq % IN_BLOCK == 0
    row = lambda g: g.reshape(1, -1).astype(F32)

    ws = w_spatial[0].astype(BF16)
    ws_cat = jnp.concatenate([ws[0::2], ws[1::2]], axis=2)
    bs_full = jnp.repeat(b_spatial[0].T, A_GROUP_DIM, axis=1)
    cosf, sins = _rope_tables(seq)
    win, wout, wq = w_in[0].astype(BF16), w_out[0].astype(BF16), w_query[0].astype(BF16)
    keys = sub_keys[0].astype(BF16)
    gf = row(final_norm_g)
    half = lambda t: lax.bitcast_convert_type(t.astype(BF16), jnp.uint16).astype(jnp.uint32)
    uv2 = lax.bitcast_convert_type((half(expert_u[0]) << 16) | half(expert_v[0]), I32)
    uv = uv2.reshape(-1, NCH, LANES)

    nchunk = PIPE_CHUNKS if batch % PIPE_CHUNKS == 0 else 1
    cb = batch // nchunk
    pieces = []
    for ci in range(nchunk):
        x2 = x[ci * cb:(ci + 1) * cb].reshape(cb * seq, D_MODEL)
        an, q1, k1, v1, q4, k4, v4, q16, k16, v16 = _in_proj(
            x2, row(norm1_g[0]), win, row(ln_v_g[0]), row(ln_v_b[0]),
            ws_cat, bs_full, row(out_norm_a_g[0]), cosf, sins, cb, seq)
        bo = _attention(q1, k1, v1, q4, k4, v4, q16, k16, v16, cb, seq).reshape(cb * seq, D_B)
        x1, xn, idx, gates = _mid(x2, an, bo, row(out_norm_b_g[0]), wout, row(norm2_g[0]), wq, keys)
        n_sc = (cb * seq) * SC_SHARE[0] // SC_SHARE[1]
        assert n_sc % PEER_BLOCK == 0
        out_sc = _sc_peer(idx.reshape(cb * seq, N_SLOTS), xn, gates, uv2, n_sc)
        y_tc = _peer(idx, gates, xn, x1, gf, uv, n_sc)
        pieces += [_residual_norm(x1, out_sc, gf), y_tc]
    return jnp.concatenate(pieces, axis=0).reshape(batch, seq, D_MODEL)
```

```python
import functools
import math

import jax
import jax.numpy as jnp
from jax import lax
from jax.experimental import pallas as pl
from jax.experimental.pallas import tpu as pltpu
from jax.experimental.pallas import tpu_sc as plsc

F32 = jnp.float32
BF16 = jnp.bfloat16
I32 = jnp.int32

D_MODEL = 1024
D_A = 512
D_B = 512
A_GROUPS = 8
A_GROUP_DIM = 64
CHUNK = 128
B_HEADS = 8
HEAD_DIM = 64
DILATIONS = (1, 4, 16)
HALF_WINDOW = 64
ROPE_THETA = 10000.0
D_IN = 2 * D_A + 3 * D_B
N_KEYS = 128
PEER_HEADS = 8
PEER_TOPK = 16
D_KEY = 256
N_SLOTS = PEER_HEADS * PEER_TOPK
EPS = 1e-6
NEG_BIG = -1e30

LANES = 128
SUB = 8
NCH = D_MODEL // LANES
U_HALF = -65536
QBLK = 128
ATTN_UNROLL = 4
RES16_PITCH = 24
IN_BLOCK = 512
MID_BLOCK = 256
PEER_BLOCK = 128
PEER_RING = 4
SC_SHARE = (45, 64)
PIPE_CHUNKS = 8
SC_ROWS = 16
SC_PASSES = 4
VMEM_LIMIT = 48 * 1024 * 1024


def _gelu(x):
    c = math.sqrt(2.0 / math.pi)
    return 0.5 * x * (1.0 + jnp.tanh(c * (x + 0.044715 * (x * x * x))))


def _rms(x, g):
    return x * lax.rsqrt(jnp.mean(x * x, axis=-1, keepdims=True) + EPS) * g


def _in_proj_kernel(x_ref, g1_ref, win_ref, lng_ref, lnb_ref, ws_ref, bs_ref, ga_ref,
                    cos_ref, sin_ref,
                    an_ref, q1_ref, k1_ref, v1_ref, q4_ref, k4_ref, v4_ref,
                    q16_ref, k16_ref, v16_ref, slab_ref):
    nt = x_ref.shape[0]
    h = _rms(x_ref[...], g1_ref[...]).astype(BF16)
    proj = jnp.dot(h, win_ref[...], preferred_element_type=F32)

    u = _gelu(proj[:, :D_A])
    v = _gelu(proj[:, D_A:2 * D_A])
    mu = jnp.mean(v, axis=-1, keepdims=True)
    vc = v - mu
    var = jnp.mean(vc * vc, axis=-1, keepdims=True)
    vln = (vc * lax.rsqrt(var + EPS) * lng_ref[...] + lnb_ref[...]).astype(BF16)
    lane = lax.broadcasted_iota(I32, (CHUNK, LANES), 1)
    lo = lane < A_GROUP_DIM
    zero = jnp.zeros((CHUNK, LANES), BF16)
    chunks = []
    for c in range(nt // CHUNK):
        cols = []
        for j in range(A_GROUPS // 2):
            vv = vln[c * CHUNK:(c + 1) * CHUNK, j * LANES:(j + 1) * LANES]
            rhs = jnp.concatenate([jnp.where(lo, vv, zero), jnp.where(lo, zero, vv)], axis=0)
            cols.append(jnp.dot(ws_ref[j], rhs, preferred_element_type=F32))
        chunks.append(jnp.concatenate(cols, axis=1) + bs_ref[...])
    mixed = jnp.concatenate(chunks, axis=0)
    an_ref[...] = _rms(u * mixed, ga_ref[...]).astype(BF16)

    cosf = cos_ref[...]
    sins = sin_ref[...]
    lane_b = lax.broadcasted_iota(I32, (nt, D_B), 1)
    first_half = (lane_b % HEAD_DIM) < (HEAD_DIM // 2)

    def rope(t):
        partner = jnp.where(first_half, pltpu.roll(t, D_B - HEAD_DIM // 2, 1),
                            pltpu.roll(t, HEAD_DIM // 2, 1))
        return t * cosf + partner * sins

    q = rope(proj[:, 2 * D_A:2 * D_A + D_B]) * (HEAD_DIM ** -0.5)
    k = rope(proj[:, 2 * D_A + D_B:2 * D_A + 2 * D_B])
    vv = proj[:, 2 * D_A + 2 * D_B:]
    q1_ref[...] = q.astype(BF16)
    k1_ref[...] = k.astype(BF16)
    v1_ref[...] = vv.astype(BF16)

    nslab = D_B // LANES
    for a, t in enumerate((q, k, vv)):
        for s in range(nslab):
            slab_ref[a * nslab + s] = t[:, s * LANES:(s + 1) * LANES]
    for d, outs in ((4, (q4_ref, k4_ref, v4_ref)), (16, (q16_ref, k16_ref, v16_ref))):
        rows = nt // d
        for a, o_ref in enumerate(outs):
            for r in range(d):
                for s in range(nslab):
                    o_ref[r, :, s * LANES:(s + 1) * LANES] = (
                        slab_ref[a * nslab + s, pl.ds(r, rows, stride=d), :].astype(BF16))


def _in_proj(x2, g1, win, lng, lnb, ws_cat, bs_full, ga, cosf, sins, batch, seq):
    t_total = x2.shape[0]
    nt = IN_BLOCK
    nb = seq // nt
    grid = (t_total // nt,)
    row = lambda i: (i, 0)
    const2 = lambda i: (0, 0)
    tok_bf = jax.ShapeDtypeStruct((t_total, D_B), BF16)
    out_shape = (
        jax.ShapeDtypeStruct((t_total, D_A), BF16),
        tok_bf, tok_bf, tok_bf,
        *(jax.ShapeDtypeStruct((batch, 4, seq // 4, D_B), BF16),) * 3,
        *(jax.ShapeDtypeStruct((batch, 16, seq // 16, D_B), BF16),) * 3,
    )
    res4 = pl.BlockSpec((None, 4, nt // 4, D_B), lambda i: (i // nb, 0, i % nb, 0))
    res16 = pl.BlockSpec((None, 16, nt // 16, D_B), lambda i: (i // nb, 0, i % nb, 0))
    tok_spec = pl.BlockSpec((nt, D_B), row)
    return pl.pallas_call(
        _in_proj_kernel,
        grid=grid,
        in_specs=[
            pl.BlockSpec((nt, D_MODEL), row),
            pl.BlockSpec((1, D_MODEL), const2),
            pl.BlockSpec((D_MODEL, D_IN), const2),
            pl.BlockSpec((1, D_A), const2),
            pl.BlockSpec((1, D_A), const2),
            pl.BlockSpec((A_GROUPS // 2, CHUNK, 2 * CHUNK), lambda i: (0, 0, 0)),
            pl.BlockSpec((CHUNK, D_A), const2),
            pl.BlockSpec((1, D_A), const2),
            pl.BlockSpec((nt, D_B), lambda i: (i % nb, 0)),
            pl.BlockSpec((nt, D_B), lambda i: (i % nb, 0)),
        ],
        out_specs=(pl.BlockSpec((nt, D_A), row), tok_spec, tok_spec, tok_spec,
                   res4, res4, res4, res16, res16, res16),
        out_shape=out_shape,
        scratch_shapes=[pltpu.VMEM((3 * D_B // LANES, nt, LANES), F32)],
        compiler_params=pltpu.CompilerParams(
            dimension_semantics=("arbitrary",), vmem_limit_bytes=VMEM_LIMIT),
        name="in_proj",
    )(x2, g1, win, lng, lnb, ws_cat, bs_full, ga, cosf, sins)


def _attn_kernel(q1_ref, k1_ref, v1_ref, q4_ref, k4_ref, v4_ref, q16_ref, k16_ref, v16_ref,
                 o_ref, out_ref, lse_ref, out16_ref, lse16_ref):
    seq = o_ref.shape[0]
    lane = lax.broadcasted_iota(I32, (QBLK, LANES), 1)
    head0 = lane < HEAD_DIM
    branches = ((1, q1_ref, k1_ref, v1_ref), (4, q4_ref, k4_ref, v4_ref),
                (16, q16_ref, k16_ref, v16_ref))
    for bi, (d, q_ref, k_ref, v_ref) in enumerate(branches):
        length = seq // d
        nblk = length // QBLK
        win = min(2 * QBLK, length)
        diff = (lax.broadcasted_iota(I32, (QBLK, win), 1)
                - lax.broadcasted_iota(I32, (QBLK, win), 0))

        def block(blk, carry, d=d, bi=bi, q_ref=q_ref, k_ref=k_ref, v_ref=v_ref,
                  length=length, nblk=nblk, win=win, diff=diff):
            r = blk // nblk
            i0 = pl.multiple_of((blk % nblk) * QBLK, QBLK)
            w0 = pl.multiple_of(jnp.clip(i0 - HALF_WINDOW, 0, length - win), HALF_WINDOW)
            qb = q_ref[r, pl.ds(i0, QBLK), :]
            kw = k_ref[r, pl.ds(w0, win), :]
            vw = v_ref[r, pl.ds(w0, win), :]
            rel = diff + (w0 - i0)
            valid = (rel >= -HALF_WINDOW) & (rel <= HALF_WINDOW)
            zero = jnp.zeros_like(qb)
            qq = jnp.concatenate([jnp.where(head0, qb, zero), jnp.where(head0, zero, qb)], axis=0)
            s = lax.dot_general(qq, kw, (((1,), (1,)), ((), ())), preferred_element_type=F32)
            s = jnp.where(jnp.concatenate([valid, valid], axis=0), s, NEG_BIG)
            m = jnp.max(s, axis=1, keepdims=True)
            p = jnp.exp(s - m)
            l = jnp.sum(p, axis=1, keepdims=True)
            pv = jnp.dot(p.astype(BF16), vw, preferred_element_type=F32) / l
            ml = m + jnp.log(l)
            out = jnp.where(head0, pv[:QBLK], pv[QBLK:])
            lse = jnp.where(head0, ml[:QBLK], ml[QBLK:])
            if d == 1:
                out_ref[0, pl.ds(i0, QBLK), :] = out
                lse_ref[0, pl.ds(i0, QBLK), :] = lse
            elif d == 4:
                rows = pl.ds(i0 * d + r, QBLK, stride=d)
                out_ref[1, rows, :] = out
                lse_ref[1, rows, :] = lse
            else:
                rows = pl.ds(r, QBLK, stride=RES16_PITCH)
                out16_ref[rows, :] = out
                lse16_ref[rows, :] = lse
            return carry

        lax.fori_loop(0, d * nblk, block, 0, unroll=ATTN_UNROLL)

    groups = QBLK // 16

    def merge(c, carry):
        rows = pl.ds(pl.multiple_of(c * QBLK, QBLK), QBLK)
        base = pl.multiple_of(c * (groups * RES16_PITCH), SUB)
        pieces = [pl.ds(base + g * RES16_PITCH, 16) for g in range(groups)]
        o3 = jnp.concatenate([out16_ref[pc, :] for pc in pieces], axis=0)
        e3 = jnp.concatenate([lse16_ref[pc, :] for pc in pieces], axis=0)
        e1, e2 = lse_ref[0, rows, :], lse_ref[1, rows, :]
        mx = jnp.maximum(jnp.maximum(e1, e2), e3)
        w1, w2, w3 = jnp.exp(e1 - mx), jnp.exp(e2 - mx), jnp.exp(e3 - mx)
        num = w1 * out_ref[0, rows, :] + w2 * out_ref[1, rows, :] + w3 * o3
        o_ref[rows, :] = num / (w1 + w2 + w3)
        return carry

    lax.fori_loop(0, seq // QBLK, merge, 0)


def _attention(q1, k1, v1, q4, k4, v4, q16, k16, v16, batch, seq):
    npair = D_B // LANES
    nat = pl.BlockSpec((None, 1, seq, LANES), lambda b, p: (b, 0, 0, p))
    r4 = pl.BlockSpec((None, 4, seq // 4, LANES), lambda b, p: (b, 0, 0, p))
    r16 = pl.BlockSpec((None, 16, seq // 16, LANES), lambda b, p: (b, 0, 0, p))
    q1, k1, v1 = (t.reshape(batch, 1, seq, D_B) for t in (q1, k1, v1))
    return pl.pallas_call(
        _attn_kernel,
        grid=(batch, npair),
        in_specs=[nat, nat, nat, r4, r4, r4, r16, r16, r16],
        out_specs=pl.BlockSpec((None, seq, LANES), lambda b, p: (b, 0, p)),
        out_shape=jax.ShapeDtypeStruct((batch, seq, D_B), F32),
        scratch_shapes=[pltpu.VMEM((2, seq, LANES), F32)] * 2
        + [pltpu.VMEM((seq // 16 * RES16_PITCH, LANES), F32)] * 2,
        compiler_params=pltpu.CompilerParams(
            dimension_semantics=("arbitrary", "arbitrary"), vmem_limit_bytes=VMEM_LIMIT),
        name="dilated_attn",
    )(q1, k1, v1, q4, k4, v4, q16, k16, v16)


def _topk_rows(s, k):
    n = s.shape[0]
    iota = lax.broadcasted_iota(I32, s.shape, 0).astype(F32)
    vals, idxs = [], []
    for _ in range(k):
        m = jnp.max(s, axis=0, keepdims=True)
        i = jnp.min(jnp.where(s == m, iota, float(n)), axis=0, keepdims=True)
        vals.append(m)
        idxs.append(i)
        s = jnp.where(iota == i, -jnp.inf, s)
    return jnp.concatenate(vals, axis=0), jnp.concatenate(idxs, axis=0).astype(I32)


def _take_rows(table, sel):
    out = jnp.zeros(sel.shape, table.dtype)
    for a in range(table.shape[0]):
        out = jnp.where(sel == a, table[a:a + 1, :], out)
    return out


def _mid_kernel(x_ref, an_ref, bo_ref, gb_ref, wout_ref, g2_ref, wq_ref, keys_ref,
                x1_ref, xn_ref, idx_ref, gate_ref):
    nt = x_ref.shape[0]
    bn = _rms(bo_ref[...], gb_ref[...]).astype(BF16)
    x1 = (x_ref[...]
          + jnp.dot(an_ref[...], wout_ref[:D_A, :], preferred_element_type=F32)
          + jnp.dot(bn, wout_ref[D_A:, :], preferred_element_type=F32))
    x1_ref[...] = x1
    xn = _rms(x1, g2_ref[...])
    xn_ref[...] = xn
    q = jnp.dot(xn.astype(BF16), wq_ref[...], preferred_element_type=F32).astype(BF16)
    keys = (keys_ref[0], keys_ref[1])
    half = D_KEY // 2
    for c in range(nt // LANES):
        qc = q[c * LANES:(c + 1) * LANES, :]
        experts, gates = [], []
        for h in range(PEER_HEADS):
            tops = []
            for p in range(2):
                qhp = qc[:, (2 * h + p) * half:(2 * h + p + 1) * half]
                s = lax.dot_general(keys[p], qhp, (((1,), (1,)), ((), ())),
                                    preferred_element_type=F32)
                tops.append(_topk_rows(s, PEER_TOPK))
            (s1, i1), (s2, i2) = tops
            cand = jnp.concatenate(
                [s1[0:1, :] + s2]
                + [s1[a:a + 1, :] + s2[0:SUB, :] for a in range(1, SUB)]
                + [s1[SUB:, :] + s2[0:1, :]], axis=0)
            sc, pos = _topk_rows(cand, PEER_TOPK)
            ca = jnp.where(pos < PEER_TOPK, 0,
                           jnp.where(pos < PEER_TOPK + SUB * (SUB - 1), (pos >> 3) - 1, pos - SUB * SUB))
            cb = jnp.where(pos < PEER_TOPK, pos,
                           jnp.where(pos < PEER_TOPK + SUB * (SUB - 1), pos & (SUB - 1), 0))
            e = _take_rows(i1, ca) * N_KEYS + _take_rows(i2, cb)
            ex = jnp.exp(sc - sc[0:1, :])
            gates.append(ex / jnp.sum(ex, axis=0, keepdims=True))
            experts.append(e)
        idx_ref[c] = jnp.concatenate(experts, axis=0).T
        gate_ref[c * LANES:(c + 1) * LANES, :] = jnp.concatenate(gates, axis=0).T


def _mid(x2, an, bo, gb, wout, g2, wq, keys):
    t_total = x2.shape[0]
    nt = MID_BLOCK
    row = lambda i: (i, 0)
    const2 = lambda i: (0, 0)
    return pl.pallas_call(
        _mid_kernel,
        grid=(t_total // nt,),
        in_specs=[
            pl.BlockSpec((nt, D_MODEL), row),
            pl.BlockSpec((nt, D_A), row),
            pl.BlockSpec((nt, D_B), row),
            pl.BlockSpec((1, D_B), const2),
            pl.BlockSpec((D_MODEL, D_MODEL), const2),
            pl.BlockSpec((1, D_MODEL), const2),
            pl.BlockSpec((D_MODEL, PEER_HEADS * D_KEY), const2),
            pl.BlockSpec((2, N_KEYS, D_KEY // 2), lambda i: (0, 0, 0)),
        ],
        out_specs=(
            pl.BlockSpec((nt, D_MODEL), row),
            pl.BlockSpec((nt, D_MODEL), row),
            pl.BlockSpec((nt // LANES, LANES, N_SLOTS), lambda i: (i, 0, 0)),
            pl.BlockSpec((nt, N_SLOTS), row),
        ),
        out_shape=(
            jax.ShapeDtypeStruct((t_total, D_MODEL), F32),
            jax.ShapeDtypeStruct((t_total, D_MODEL), F32),
            jax.ShapeDtypeStruct((t_total // LANES, LANES, N_SLOTS), I32),
            jax.ShapeDtypeStruct((t_total, N_SLOTS), F32),
        ),
        compiler_params=pltpu.CompilerParams(
            dimension_semantics=("arbitrary",), vmem_limit_bytes=VMEM_LIMIT),
        name="mid",
    )(x2, an, bo, gb, wout, g2, wq, keys)


def _peer_kernel(idx_ref, gate_ref, xn_ref, x1_ref, gf_ref, uv_ref, y_ref, *scratch):
    rows_refs = scratch[:PEER_RING]
    bf_ref, sem_ref = scratch[PEER_RING:]
    nt = xn_ref.shape[0]
    wide = 2 * LANES
    nw = 2 * D_MODEL // wide

    def issue(t, slot, lo=0, hi=N_SLOTS):
        for s in range(lo, hi):
            pltpu.make_async_copy(uv_ref.at[idx_ref[t, s]],
                                  rows_refs[slot].at[s // SUB, :, s % SUB, :],
                                  sem_ref.at[slot]).start(priority=s % 2)

    def wait(slot):
        pltpu.make_async_copy(uv_ref.at[pl.ds(0, N_SLOTS)],
                              rows_refs[slot].reshape(N_SLOTS, NCH, LANES), sem_ref.at[slot]).wait()

    def stage(slot, k):
        for j in range(NCH):
            w = rows_refs[slot][:, j, :, :].reshape(N_SLOTS, LANES)
            bf_ref[k, :, j * LANES:(j + 1) * LANES] = (
                pltpu.bitcast(w & U_HALF, F32).astype(BF16))
            bf_ref[k, :, D_MODEL + j * LANES:D_MODEL + (j + 1) * LANES] = (
                pltpu.bitcast(w << 16, F32).astype(BF16))

    def pair(t0, slots, prefetch):
        for k in range(2):
            wait(slots[k])
            stage(slots[k], k)
        nbatch = 2 * nw
        per = N_SLOTS // (nbatch // 2)
        batches = [(k, b * per, (b + 1) * per) for b in range(nbatch // 2) for k in range(2)]

        def next_batch():
            if prefetch and batches:
                k, lo, hi = batches.pop(0)
                issue(t0 + PEER_RING + k, slots[k], lo, hi)

        x8 = [jnp.broadcast_to(xn_ref[pl.ds(t0 + k, 1), :], (SUB, D_MODEL)).astype(BF16)
              for k in range(2)]
        act = [jnp.zeros((SUB, N_SLOTS), F32) for _ in range(2)]
        for j in range(nw // 2):
            for k in range(2):
                next_batch()
                act[k] = act[k] + lax.dot_general(
                    x8[k][:, j * wide:(j + 1) * wide], bf_ref[k, :, j * wide:(j + 1) * wide],
                    (((1,), (1,)), ((), ())), preferred_element_type=F32)
        w = [(_gelu(act[k]) * gate_ref[pl.ds(t0 + k, 1), :]).astype(BF16) for k in range(2)]
        outs = [[], []]
        for j in range(nw // 2):
            for k in range(2):
                next_batch()
                outs[k].append(jnp.dot(
                    w[k], bf_ref[k, :, D_MODEL + j * wide:D_MODEL + (j + 1) * wide],
                    preferred_element_type=F32)[0:1, :])
        for k in range(2):
            out = jnp.concatenate(outs[k], axis=1)
            y_ref[pl.ds(t0 + k, 1), :] = _rms(x1_ref[pl.ds(t0 + k, 1), :] + out, gf_ref[...])

    def group(g, prefetch):
        for p in range(PEER_RING // 2):
            pair(g * PEER_RING + 2 * p, (2 * p, 2 * p + 1), prefetch)

    for t in range(PEER_RING):
        issue(t, t)
    ngroup = nt // PEER_RING
    lax.fori_loop(0, ngroup - 1, lambda g, c: (group(g, True), c)[1], 0)
    group(ngroup - 1, False)


def _peer(idx, gates, xn, x1, gf, uv, first_token):
    t_total = xn.shape[0] - first_token
    nt = PEER_BLOCK
    b0 = first_token // nt
    row = lambda i: (i + b0, 0)
    return pl.pallas_call(
        _peer_kernel,
        grid=(t_total // nt,),
        in_specs=[
            pl.BlockSpec((None, nt, N_SLOTS), lambda i: (i + b0, 0, 0), memory_space=pltpu.SMEM),
            pl.BlockSpec((nt, N_SLOTS), row),
            pl.BlockSpec((nt, D_MODEL), row),
            pl.BlockSpec((nt, D_MODEL), row),
            pl.BlockSpec((1, D_MODEL), lambda i: (0, 0)),
            pl.BlockSpec(memory_space=pl.ANY),
        ],
        out_specs=pl.BlockSpec((nt, D_MODEL), lambda i: (i, 0)),
        out_shape=jax.ShapeDtypeStruct((t_total, D_MODEL), F32),
        scratch_shapes=[pltpu.VMEM((N_SLOTS // SUB, NCH, SUB, LANES), I32)] * PEER_RING + [
            pltpu.VMEM((2, N_SLOTS, 2 * D_MODEL), BF16),
            pltpu.SemaphoreType.DMA((PEER_RING,))],
        compiler_params=pltpu.CompilerParams(
            dimension_semantics=("arbitrary",), vmem_limit_bytes=VMEM_LIMIT),
        name="peer",
    )(idx, gates, xn, x1, gf, uv)


def _sc_peer(idx, xn, gates, uv2, n_tokens):
    info = plsc.get_sparse_core_info()
    nc, lanes_n = info.num_cores, info.num_lanes
    nw = nc * info.num_subcores
    per = n_tokens // nw
    assert n_tokens % nw == 0 and per % 2 == 0
    nchunk = N_SLOTS // SC_ROWS
    qv = D_MODEL // (SC_PASSES * lanes_n)
    c0 = math.sqrt(2.0 / math.pi)
    mesh = plsc.VectorSubcoreMesh(core_axis_name="c", subcore_axis_name="s")
    dma = pltpu.SemaphoreType.DMA

    @functools.partial(
        pl.kernel, mesh=mesh,
        out_type=jax.ShapeDtypeStruct((n_tokens, D_MODEL), F32),
        scratch_types=[
            [pltpu.VMEM((nchunk, SC_ROWS), I32)] * 2,
            [pltpu.VMEM((D_MODEL,), F32)] * 2,
            [pltpu.VMEM((N_SLOTS,), F32)] * 2,
            [pltpu.VMEM((D_MODEL,), F32)] * 2,
            [pltpu.VMEM((SC_ROWS, D_MODEL), I32)] * 2,
            pltpu.VMEM((SC_ROWS, lanes_n), F32),
            pltpu.VMEM((SC_ROWS,), F32),
            [dma] * 2, [dma] * 2, [dma] * 2, [dma] * 2, [dma] * 2,
        ],
        compiler_params=pltpu.CompilerParams(needs_layout_passes=False),
        name="peer_sc",
    )
    def sc_kernel(idx_hbm, xn_hbm, gate_hbm, uv_hbm, out_hbm, idx_v, x_v, g_v, out_v, rows_v,
                  part_v, w_v, row_sem, idx_sem, x_sem, g_sem, out_sem):
        wid = lax.axis_index("s") * nc + lax.axis_index("c")
        t0 = wid * per
        lane_ids = lax.iota(I32, lanes_n)
        zero = jnp.zeros((lanes_n,), F32)

        def vec(q, j):
            return pl.ds((q * qv + j) * lanes_n, lanes_n)

        def gather(p, c, b):
            return pltpu.make_async_copy(uv_hbm.at[idx_v[p].at[c]], rows_v[b], row_sem[b])

        def inputs(tok, p):
            return (pltpu.make_async_copy(idx_hbm.at[tok], idx_v[p], idx_sem[p]),
                    pltpu.make_async_copy(xn_hbm.at[tok], x_v[p], x_sem[p]),
                    pltpu.make_async_copy(gate_hbm.at[tok], g_v[p], g_sem[p]))

        def result(tok, p):
            return pltpu.make_async_copy(out_v[p], out_hbm.at[tok], out_sem[p])

        def chunk(p, c, b):
            rows = rows_v[b]

            def ustep(j, acc):
                xj = x_v[p][pl.ds(j * lanes_n, lanes_n)]
                return tuple(
                    acc[r] + plsc.bitcast(rows[r, pl.ds(j * lanes_n, lanes_n)] & U_HALF, F32) * xj
                    for r in range(SC_ROWS))

            acc = lax.fori_loop(0, D_MODEL // lanes_n, ustep, (zero,) * SC_ROWS)
            for r in range(SC_ROWS):
                part_v[r, :] = acc[r]
            cols = [plsc.load_gather(part_v, [lane_ids, jnp.full((lanes_n,), l, I32)])
                    for l in range(lanes_n)]
            while len(cols) > 1:
                cols = [cols[i] + cols[i + 1] for i in range(0, len(cols), 2)]
            act = cols[0]
            z = c0 * (act + 0.044715 * (act * act * act))
            tanh_z = 1.0 - 2.0 / (jnp.exp(2.0 * z) + 1.0)
            w_v[...] = 0.5 * act * (1.0 + tanh_z) * g_v[p][pl.ds(c * SC_ROWS, SC_ROWS)]
            for q in range(SC_PASSES):
                o = tuple(out_v[p][vec(q, j)] for j in range(qv))

                def vrow(r, o, q=q):
                    wr = plsc.load_gather(w_v, [jnp.full((lanes_n,), r, I32)])
                    return tuple(o[j] + wr * plsc.bitcast(rows[r, vec(q, j)] << 16, F32)
                                 for j in range(qv))

                o = lax.fori_loop(0, SC_ROWS, vrow, o)
                for j in range(qv):
                    out_v[p][vec(q, j)] = o[j]

        def token(i, p):
            tok = t0 + i
            more = i + 1 < per

            @pl.when(more)
            def _():
                for cp in inputs(tok + 1, 1 - p):
                    cp.start()

            @pl.when(i >= 2)
            def _():
                result(tok - 2, p).wait()

            for j in range(D_MODEL // lanes_n):
                out_v[p][pl.ds(j * lanes_n, lanes_n)] = zero

            @pl.loop(0, nchunk, step=2)
            def _(c):
                gather(p, c + 1, 1).start()
                gather(p, c, 0).wait()
                chunk(p, c, 0)

                @pl.when(c + 2 < nchunk)
                def _():
                    gather(p, c + 2, 0).start()

                @pl.when(jnp.logical_and(c + 2 >= nchunk, more))
                def _():
                    for cp in inputs(tok + 1, 1 - p):
                        cp.wait()
                    gather(1 - p, 0, 0).start()

                gather(p, c + 1, 1).wait()
                chunk(p, c + 1, 1)

            result(tok, p).start()

        for cp in inputs(t0, 0):
            cp.start()
        for cp in inputs(t0, 0):
            cp.wait()
        gather(0, 0, 0).start()

        @pl.loop(0, per, step=2)
        def _(i):
            token(i, 0)
            token(i + 1, 1)

        result(t0 + per - 2, 0).wait()
        result(t0 + per - 1, 1).wait()

    return sc_kernel(idx.reshape(-1, nchunk, SC_ROWS), xn, gates, uv2)


def _residual_norm_kernel(x1_ref, o_ref, gf_ref, y_ref):
    y_ref[...] = _rms(x1_ref[...] + o_ref[...], gf_ref[...])


def _residual_norm(x1, out, gf):
    n = out.shape[0]
    nt = PEER_BLOCK
    row = lambda i: (i, 0)
    return pl.pallas_call(
        _residual_norm_kernel,
        grid=(n // nt,),
        in_specs=[pl.BlockSpec((nt, D_MODEL), row), pl.BlockSpec((nt, D_MODEL), row),
                  pl.BlockSpec((1, D_MODEL), lambda i: (0, 0))],
        out_specs=pl.BlockSpec((nt, D_MODEL), row),
        out_shape=jax.ShapeDtypeStruct((n, D_MODEL), F32),
        compiler_params=pltpu.CompilerParams(dimension_semantics=("arbitrary",)),
        name="residual_norm",
    )(x1, out, gf)


def _rope_tables(seq):
    pos = jnp.arange(seq, dtype=F32)
    inv = 1.0 / (ROPE_THETA ** (jnp.arange(0, HEAD_DIM, 2, dtype=F32) / HEAD_DIM))
    ang = pos[:, None] * inv[None, :]
    cos, sin = jnp.cos(ang), jnp.sin(ang)
    cosf = jnp.tile(jnp.concatenate([cos, cos], axis=1), (1, B_HEADS))
    sins = jnp.tile(jnp.concatenate([-sin, sin], axis=1), (1, B_HEADS))
    return cosf, sins


def kernel(x, norm1_g, w_in, ln_v_g, ln_v_b, w_spatial, b_spatial, out_norm_a_g, out_norm_b_g,
           w_out, norm2_g, w_query, sub_keys, expert_u, expert_v, final_norm_g):
    batch, seq, _ = x.shape
    assert w_in.shape[0] == 1 and seq % (16 * QBLK) == 0 and seq % IN_BLOCK == 0
    row = lambda g: g.reshape(1, -1).astype(F32)

    ws = w_spatial[0].astype(BF16)
    ws_cat = jnp.concatenate([ws[0::2], ws[1::2]], axis=2)
    bs_full = jnp.repeat(b_spatial[0].T, A_GROUP_DIM, axis=1)
    cosf, sins = _rope_tables(seq)
    win, wout, wq = w_in[0].astype(BF16), w_out[0].astype(BF16), w_query[0].astype(BF16)
    keys = sub_keys[0].astype(BF16)
    gf = row(final_norm_g)
    half = lambda t: lax.bitcast_convert_type(t.astype(BF16), jnp.uint16).astype(jnp.uint32)
    uv2 = lax.bitcast_convert_type((half(expert_u[0]) << 16) | half(expert_v[0]), I32)
    uv = uv2.reshape(-1, NCH, LANES)

    nchunk = PIPE_CHUNKS if batch % PIPE_CHUNKS == 0 else 1
    cb = batch // nchunk
    pieces = []
    for ci in range(nchunk):
        x2 = x[ci * cb:(ci + 1) * cb].reshape(cb * seq, D_MODEL)
        an, q1, k1, v1, q4, k4, v4, q16, k16, v16 = _in_proj(
            x2, row(norm1_g[0]), win, row(ln_v_g[0]), row(ln_v_b[0]),
            ws_cat, bs_full, row(out_norm_a_g[0]), cosf, sins, cb, seq)
        bo = _attention(q1, k1, v1, q4, k4, v4, q16, k16, v16, cb, seq).reshape(cb * seq, D_B)
        x1, xn, idx, gates = _mid(x2, an, bo, row(out_norm_b_g[0]), wout, row(norm2_g[0]), wq, keys)
        n_sc = (cb * seq) * SC_SHARE[0] // SC_SHARE[1]
        assert n_sc % PEER_BLOCK == 0
        out_sc = _sc_peer(idx.reshape(cb * seq, N_SLOTS), xn, gates, uv2, n_sc)
        y_tc = _peer(idx, gates, xn, x1, gf, uv, n_sc)
        pieces += [_residual_norm(x1, out_sc, gf), y_tc]
    return jnp.concatenate(pieces, axis=0).reshape(batch, seq, D_MODEL)
```

```python
import functools
import math

import jax
import jax.numpy as jnp
from jax import lax
from jax.experimental import pallas as pl
from jax.experimental.pallas import tpu as pltpu
from jax.experimental.pallas import tpu_sc as plsc

F32 = jnp.float32
BF16 = jnp.bfloat16
I32 = jnp.int32

D_MODEL = 1024
D_A = 512
D_B = 512
A_GROUPS = 8
A_GROUP_DIM = 64
CHUNK = 128
B_HEADS = 8
HEAD_DIM = 64
DILATIONS = (1, 4, 16)
HALF_WINDOW = 64
ROPE_THETA = 10000.0
D_IN = 2 * D_A + 3 * D_B
N_KEYS = 128
PEER_HEADS = 8
PEER_TOPK = 16
D_KEY = 256
N_SLOTS = PEER_HEADS * PEER_TOPK
EPS = 1e-6
NEG_BIG = -1e30

LANES = 128
SUB = 8
NCH = D_MODEL // LANES
U_HALF = -65536
QBLK = 128
ATTN_UNROLL = 4
RES16_PITCH = 24
IN_BLOCK = 512
MID_BLOCK = 256
PEER_BLOCK = 128
PEER_RING = 4
SC_SHARE = (23, 32)
PIPE_CHUNKS = 8
SC_ROWS = 16
SC_PASSES = 4
VMEM_LIMIT = 48 * 1024 * 1024


def _gelu(x):
    c = math.sqrt(2.0 / math.pi)
    return 0.5 * x * (1.0 + jnp.tanh(c * (x + 0.044715 * (x * x * x))))


def _rms(x, g):
    return x * lax.rsqrt(jnp.mean(x * x, axis=-1, keepdims=True) + EPS) * g


def _in_proj_kernel(x_ref, g1_ref, win_ref, lng_ref, lnb_ref, ws_ref, bs_ref, ga_ref,
                    cos_ref, sin_ref,
                    an_ref, q1_ref, k1_ref, v1_ref, q4_ref, k4_ref, v4_ref,
                    q16_ref, k16_ref, v16_ref, slab_ref):
    nt = x_ref.shape[0]
    h = _rms(x_ref[...], g1_ref[...]).astype(BF16)
    proj = jnp.dot(h, win_ref[...], preferred_element_type=F32)

    u = _gelu(proj[:, :D_A])
    v = _gelu(proj[:, D_A:2 * D_A])
    mu = jnp.mean(v, axis=-1, keepdims=True)
    vc = v - mu
    var = jnp.mean(vc * vc, axis=-1, keepdims=True)
    vln = (vc * lax.rsqrt(var + EPS) * lng_ref[...] + lnb_ref[...]).astype(BF16)
    lane = lax.broadcasted_iota(I32, (CHUNK, LANES), 1)
    lo = lane < A_GROUP_DIM
    zero = jnp.zeros((CHUNK, LANES), BF16)
    chunks = []
    for c in range(nt // CHUNK):
        cols = []
        for j in range(A_GROUPS // 2):
            vv = vln[c * CHUNK:(c + 1) * CHUNK, j * LANES:(j + 1) * LANES]
            rhs = jnp.concatenate([jnp.where(lo, vv, zero), jnp.where(lo, zero, vv)], axis=0)
            cols.append(jnp.dot(ws_ref[j], rhs, preferred_element_type=F32))
        chunks.append(jnp.concatenate(cols, axis=1) + bs_ref[...])
    mixed = jnp.concatenate(chunks, axis=0)
    an_ref[...] = _rms(u * mixed, ga_ref[...]).astype(BF16)

    cosf = cos_ref[...]
    sins = sin_ref[...]
    lane_b = lax.broadcasted_iota(I32, (nt, D_B), 1)
    first_half = (lane_b % HEAD_DIM) < (HEAD_DIM // 2)

    def rope(t):
        partner = jnp.where(first_half, pltpu.roll(t, D_B - HEAD_DIM // 2, 1),
                            pltpu.roll(t, HEAD_DIM // 2, 1))
        return t * cosf + partner * sins

    q = rope(proj[:, 2 * D_A:2 * D_A + D_B]) * (HEAD_DIM ** -0.5)
    k = rope(proj[:, 2 * D_A + D_B:2 * D_A + 2 * D_B])
    vv = proj[:, 2 * D_A + 2 * D_B:]
    q1_ref[...] = q.astype(BF16)
    k1_ref[...] = k.astype(BF16)
    v1_ref[...] = vv.astype(BF16)

    nslab = D_B // LANES
    for a, t in enumerate((q, k, vv)):
        for s in range(nslab):
            slab_ref[a * nslab + s] = t[:, s * LANES:(s + 1) * LANES]
    for d, outs in ((4, (q4_ref, k4_ref, v4_ref)), (16, (q16_ref, k16_ref, v16_ref))):
        rows = nt // d
        for a, o_ref in enumerate(outs):
            for r in range(d):
                for s in range(nslab):
                    o_ref[r, :, s * LANES:(s + 1) * LANES] = (
                        slab_ref[a * nslab + s, pl.ds(r, rows, stride=d), :].astype(BF16))


def _in_proj(x2, g1, win, lng, lnb, ws_cat, bs_full, ga, cosf, sins, batch, seq):
    t_total = x2.shape[0]
    nt = IN_BLOCK
    nb = seq // nt
    grid = (t_total // nt,)
    row = lambda i: (i, 0)
    const2 = lambda i: (0, 0)
    tok_bf = jax.ShapeDtypeStruct((t_total, D_B), BF16)
    out_shape = (
        jax.ShapeDtypeStruct((t_total, D_A), BF16),
        tok_bf, tok_bf, tok_bf,
        *(jax.ShapeDtypeStruct((batch, 4, seq // 4, D_B), BF16),) * 3,
        *(jax.ShapeDtypeStruct((batch, 16, seq // 16, D_B), BF16),) * 3,
    )
    res4 = pl.BlockSpec((None, 4, nt // 4, D_B), lambda i: (i // nb, 0, i % nb, 0))
    res16 = pl.BlockSpec((None, 16, nt // 16, D_B), lambda i: (i // nb, 0, i % nb, 0))
    tok_spec = pl.BlockSpec((nt, D_B), row)
    return pl.pallas_call(
        _in_proj_kernel,
        grid=grid,
        in_specs=[
            pl.BlockSpec((nt, D_MODEL), row),
            pl.BlockSpec((1, D_MODEL), const2),
            pl.BlockSpec((D_MODEL, D_IN), const2),
            pl.BlockSpec((1, D_A), const2),
            pl.BlockSpec((1, D_A), const2),
            pl.BlockSpec((A_GROUPS // 2, CHUNK, 2 * CHUNK), lambda i: (0, 0, 0)),
            pl.BlockSpec((CHUNK, D_A), const2),
            pl.BlockSpec((1, D_A), const2),
            pl.BlockSpec((nt, D_B), lambda i: (i % nb, 0)),
            pl.BlockSpec((nt, D_B), lambda i: (i % nb, 0)),
        ],
        out_specs=(pl.BlockSpec((nt, D_A), row), tok_spec, tok_spec, tok_spec,
                   res4, res4, res4, res16, res16, res16),
        out_shape=out_shape,
        scratch_shapes=[pltpu.VMEM((3 * D_B // LANES, nt, LANES), F32)],
        compiler_params=pltpu.CompilerParams(
            dimension_semantics=("arbitrary",), vmem_limit_bytes=VMEM_LIMIT),
        name="in_proj",
    )(x2, g1, win, lng, lnb, ws_cat, bs_full, ga, cosf, sins)


def _attn_kernel(q1_ref, k1_ref, v1_ref, q4_ref, k4_ref, v4_ref, q16_ref, k16_ref, v16_ref,
                 o_ref, out_ref, lse_ref, out16_ref, lse16_ref):
    seq = o_ref.shape[0]
    lane = lax.broadcasted_iota(I32, (QBLK, LANES), 1)
    head0 = lane < HEAD_DIM
    branches = ((1, q1_ref, k1_ref, v1_ref), (4, q4_ref, k4_ref, v4_ref),
                (16, q16_ref, k16_ref, v16_ref))
    for bi, (d, q_ref, k_ref, v_ref) in enumerate(branches):
        length = seq // d
        nblk = length // QBLK
        win = min(2 * QBLK, length)
        diff = (lax.broadcasted_iota(I32, (QBLK, win), 1)
                - lax.broadcasted_iota(I32, (QBLK, win), 0))

        def block(blk, carry, d=d, bi=bi, q_ref=q_ref, k_ref=k_ref, v_ref=v_ref,
                  length=length, nblk=nblk, win=win, diff=diff):
            r = blk // nblk
            i0 = pl.multiple_of((blk % nblk) * QBLK, QBLK)
            w0 = pl.multiple_of(jnp.clip(i0 - HALF_WINDOW, 0, length - win), HALF_WINDOW)
            qb = q_ref[r, pl.ds(i0, QBLK), :]
            kw = k_ref[r, pl.ds(w0, win), :]
            vw = v_ref[r, pl.ds(w0, win), :]
            rel = diff + (w0 - i0)
            valid = (rel >= -HALF_WINDOW) & (rel <= HALF_WINDOW)
            zero = jnp.zeros_like(qb)
            qq = jnp.concatenate([jnp.where(head0, qb, zero), jnp.where(head0, zero, qb)], axis=0)
            s = lax.dot_general(qq, kw, (((1,), (1,)), ((), ())), preferred_element_type=F32)
            s = jnp.where(jnp.concatenate([valid, valid], axis=0), s, NEG_BIG)
            m = jnp.max(s, axis=1, keepdims=True)
            p = jnp.exp(s - m)
            l = jnp.sum(p, axis=1, keepdims=True)
            pv = jnp.dot(p.astype(BF16), vw, preferred_element_type=F32) / l
            ml = m + jnp.log(l)
            out = jnp.where(head0, pv[:QBLK], pv[QBLK:])
            lse = jnp.where(head0, ml[:QBLK], ml[QBLK:])
            if d == 1:
                out_ref[0, pl.ds(i0, QBLK), :] = out
                lse_ref[0, pl.ds(i0, QBLK), :] = lse
            elif d == 4:
                rows = pl.ds(i0 * d + r, QBLK, stride=d)
                out_ref[1, rows, :] = out
                lse_ref[1, rows, :] = lse
            else:
                rows = pl.ds(r, QBLK, stride=RES16_PITCH)
                out16_ref[rows, :] = out
                lse16_ref[rows, :] = lse
            return carry

        lax.fori_loop(0, d * nblk, block, 0, unroll=ATTN_UNROLL)

    groups = QBLK // 16

    def merge(c, carry):
        rows = pl.ds(pl.multiple_of(c * QBLK, QBLK), QBLK)
        base = pl.multiple_of(c * (groups * RES16_PITCH), SUB)
        pieces = [pl.ds(base + g * RES16_PITCH, 16) for g in range(groups)]
        o3 = jnp.concatenate([out16_ref[pc, :] for pc in pieces], axis=0)
        e3 = jnp.concatenate([lse16_ref[pc, :] for pc in pieces], axis=0)
        e1, e2 = lse_ref[0, rows, :], lse_ref[1, rows, :]
        mx = jnp.maximum(jnp.maximum(e1, e2), e3)
        w1, w2, w3 = jnp.exp(e1 - mx), jnp.exp(e2 - mx), jnp.exp(e3 - mx)
        num = w1 * out_ref[0, rows, :] + w2 * out_ref[1, rows, :] + w3 * o3
        o_ref[rows, :] = num / (w1 + w2 + w3)
        return carry

    lax.fori_loop(0, seq // QBLK, merge, 0)


def _attention(q1, k1, v1, q4, k4, v4, q16, k16, v16, batch, seq):
    npair = D_B // LANES
    nat = pl.BlockSpec((None, 1, seq, LANES), lambda b, p: (b, 0, 0, p))
    r4 = pl.BlockSpec((None, 4, seq // 4, LANES), lambda b, p: (b, 0, 0, p))
    r16 = pl.BlockSpec((None, 16, seq // 16, LANES), lambda b, p: (b, 0, 0, p))
    q1, k1, v1 = (t.reshape(batch, 1, seq, D_B) for t in (q1, k1, v1))
    return pl.pallas_call(
        _attn_kernel,
        grid=(batch, npair),
        in_specs=[nat, nat, nat, r4, r4, r4, r16, r16, r16],
        out_specs=pl.BlockSpec((None, seq, LANES), lambda b, p: (b, 0, p)),
        out_shape=jax.ShapeDtypeStruct((batch, seq, D_B), F32),
        scratch_shapes=[pltpu.VMEM((2, seq, LANES), F32)] * 2
        + [pltpu.VMEM((seq // 16 * RES16_PITCH, LANES), F32)] * 2,
        compiler_params=pltpu.CompilerParams(
            dimension_semantics=("arbitrary", "arbitrary"), vmem_limit_bytes=VMEM_LIMIT),
        name="dilated_attn",
    )(q1, k1, v1, q4, k4, v4, q16, k16, v16)


def _topk_rows(s, k):
    n = s.shape[0]
    iota = lax.broadcasted_iota(I32, s.shape, 0).astype(F32)
    vals, idxs = [], []
    for _ in range(k):
        m = jnp.max(s, axis=0, keepdims=True)
        i = jnp.min(jnp.where(s == m, iota, float(n)), axis=0, keepdims=True)
        vals.append(m)
        idxs.append(i)
        s = jnp.where(iota == i, -jnp.inf, s)
    return jnp.concatenate(vals, axis=0), jnp.concatenate(idxs, axis=0).astype(I32)


def _take_rows(table, sel):
    out = jnp.zeros(sel.shape, table.dtype)
    for a in range(table.shape[0]):
        out = jnp.where(sel == a, table[a:a + 1, :], out)
    return out


def _mid_kernel(x_ref, an_ref, bo_ref, gb_ref, wout_ref, g2_ref, wq_ref, keys_ref,
                x1_ref, xn_ref, idx_ref, gate_ref):
    nt = x_ref.shape[0]
    bn = _rms(bo_ref[...], gb_ref[...]).astype(BF16)
    x1 = (x_ref[...]
          + jnp.dot(an_ref[...], wout_ref[:D_A, :], preferred_element_type=F32)
          + jnp.dot(bn, wout_ref[D_A:, :], preferred_element_type=F32))
    x1_ref[...] = x1
    xn = _rms(x1, g2_ref[...])
    xn_ref[...] = xn
    q = jnp.dot(xn.astype(BF16), wq_ref[...], preferred_element_type=F32).astype(BF16)
    keys = (keys_ref[0], keys_ref[1])
    half = D_KEY // 2
    for c in range(nt // LANES):
        qc = q[c * LANES:(c + 1) * LANES, :]
        experts, gates = [], []
        for h in range(PEER_HEADS):
            tops = []
            for p in range(2):
                qhp = qc[:, (2 * h + p) * half:(2 * h + p + 1) * half]
                s = lax.dot_general(keys[p], qhp, (((1,), (1,)), ((), ())),
                                    preferred_element_type=F32)
                tops.append(_topk_rows(s, PEER_TOPK))
            (s1, i1), (s2, i2) = tops
            cand = jnp.concatenate(
                [s1[0:1, :] + s2]
                + [s1[a:a + 1, :] + s2[0:SUB, :] for a in range(1, SUB)]
                + [s1[SUB:, :] + s2[0:1, :]], axis=0)
            sc, pos = _topk_rows(cand, PEER_TOPK)
            ca = jnp.where(pos < PEER_TOPK, 0,
                           jnp.where(pos < PEER_TOPK + SUB * (SUB - 1), (pos >> 3) - 1, pos - SUB * SUB))
            cb = jnp.where(pos < PEER_TOPK, pos,
                           jnp.where(pos < PEER_TOPK + SUB * (SUB - 1), pos & (SUB - 1), 0))
            e = _take_rows(i1, ca) * N_KEYS + _take_rows(i2, cb)
            ex = jnp.exp(sc - sc[0:1, :])
            gates.append(ex / jnp.sum(ex, axis=0, keepdims=True))
            experts.append(e)
        idx_ref[c] = jnp.concatenate(experts, axis=0).T
        gate_ref[c * LANES:(c + 1) * LANES, :] = jnp.concatenate(gates, axis=0).T


def _mid(x2, an, bo, gb, wout, g2, wq, keys):
    t_total = x2.shape[0]
    nt = MID_BLOCK
    row = lambda i: (i, 0)
    const2 = lambda i: (0, 0)
    return pl.pallas_call(
        _mid_kernel,
        grid=(t_total // nt,),
        in_specs=[
            pl.BlockSpec((nt, D_MODEL), row),
            pl.BlockSpec((nt, D_A), row),
            pl.BlockSpec((nt, D_B), row),
            pl.BlockSpec((1, D_B), const2),
            pl.BlockSpec((D_MODEL, D_MODEL), const2),
            pl.BlockSpec((1, D_MODEL), const2),
            pl.BlockSpec((D_MODEL, PEER_HEADS * D_KEY), const2),
            pl.BlockSpec((2, N_KEYS, D_KEY // 2), lambda i: (0, 0, 0)),
        ],
        out_specs=(
            pl.BlockSpec((nt, D_MODEL), row),
            pl.BlockSpec((nt, D_MODEL), row),
            pl.BlockSpec((nt // LANES, LANES, N_SLOTS), lambda i: (i, 0, 0)),
            pl.BlockSpec((nt, N_SLOTS), row),
        ),
        out_shape=(
            jax.ShapeDtypeStruct((t_total, D_MODEL), F32),
            jax.ShapeDtypeStruct((t_total, D_MODEL), F32),
            jax.ShapeDtypeStruct((t_total // LANES, LANES, N_SLOTS), I32),
            jax.ShapeDtypeStruct((t_total, N_SLOTS), F32),
        ),
        compiler_params=pltpu.CompilerParams(
            dimension_semantics=("arbitrary",), vmem_limit_bytes=VMEM_LIMIT),
        name="mid",
    )(x2, an, bo, gb, wout, g2, wq, keys)


def _peer_kernel(idx_ref, gate_ref, xn_ref, x1_ref, gf_ref, uv_ref, y_ref, *scratch):
    rows_refs = scratch[:PEER_RING]
    bf_ref, sem_ref = scratch[PEER_RING:]
    nt = xn_ref.shape[0]
    wide = 2 * LANES
    nw = 2 * D_MODEL // wide

    def issue(t, slot, lo=0, hi=N_SLOTS):
        for s in range(lo, hi):
            pltpu.make_async_copy(uv_ref.at[idx_ref[t, s]],
                                  rows_refs[slot].at[s // SUB, :, s % SUB, :],
                                  sem_ref.at[slot]).start(priority=s % 2)

    def wait(slot):
        pltpu.make_async_copy(uv_ref.at[pl.ds(0, N_SLOTS)],
                              rows_refs[slot].reshape(N_SLOTS, NCH, LANES), sem_ref.at[slot]).wait()

    def stage(slot, k):
        for j in range(NCH):
            w = rows_refs[slot][:, j, :, :].reshape(N_SLOTS, LANES)
            bf_ref[k, :, j * LANES:(j + 1) * LANES] = (
                pltpu.bitcast(w & U_HALF, F32).astype(BF16))
            bf_ref[k, :, D_MODEL + j * LANES:D_MODEL + (j + 1) * LANES] = (
                pltpu.bitcast(w << 16, F32).astype(BF16))

    def pair(t0, slots, prefetch):
        for k in range(2):
            wait(slots[k])
            stage(slots[k], k)
        nbatch = 2 * nw
        per = N_SLOTS // (nbatch // 2)
        batches = [(k, b * per, (b + 1) * per) for b in range(nbatch // 2) for k in range(2)]

        def next_batch():
            if prefetch and batches:
                k, lo, hi = batches.pop(0)
                issue(t0 + PEER_RING + k, slots[k], lo, hi)

        x8 = [jnp.broadcast_to(xn_ref[pl.ds(t0 + k, 1), :], (SUB, D_MODEL)).astype(BF16)
              for k in range(2)]
        act = [jnp.zeros((SUB, N_SLOTS), F32) for _ in range(2)]
        for j in range(nw // 2):
            for k in range(2):
                next_batch()
                act[k] = act[k] + lax.dot_general(
                    x8[k][:, j * wide:(j + 1) * wide], bf_ref[k, :, j * wide:(j + 1) * wide],
                    (((1,), (1,)), ((), ())), preferred_element_type=F32)
        w = [(_gelu(act[k]) * gate_ref[pl.ds(t0 + k, 1), :]).astype(BF16) for k in range(2)]
        outs = [[], []]
        for j in range(nw // 2):
            for k in range(2):
                next_batch()
                outs[k].append(jnp.dot(
                    w[k], bf_ref[k, :, D_MODEL + j * wide:D_MODEL + (j + 1) * wide],
                    preferred_element_type=F32)[0:1, :])
        for k in range(2):
            out = jnp.concatenate(outs[k], axis=1)
            y_ref[pl.ds(t0 + k, 1), :] = _rms(x1_ref[pl.ds(t0 + k, 1), :] + out, gf_ref[...])

    def group(g, prefetch):
        for p in range(PEER_RING // 2):
            pair(g * PEER_RING + 2 * p, (2 * p, 2 * p + 1), prefetch)

    for t in range(PEER_RING):
        issue(t, t)
    ngroup = nt // PEER_RING
    lax.fori_loop(0, ngroup - 1, lambda g, c: (group(g, True), c)[1], 0)
    group(ngroup - 1, False)


def _peer(idx, gates, xn, x1, gf, uv, first_token):
    t_total = xn.shape[0] - first_token
    nt = PEER_BLOCK
    b0 = first_token // nt
    row = lambda i: (i + b0, 0)
    return pl.pallas_call(
        _peer_kernel,
        grid=(t_total // nt,),
        in_specs=[
            pl.BlockSpec((None, nt, N_SLOTS), lambda i: (i + b0, 0, 0), memory_space=pltpu.SMEM),
            pl.BlockSpec((nt, N_SLOTS), row),
            pl.BlockSpec((nt, D_MODEL), row),
            pl.BlockSpec((nt, D_MODEL), row),
            pl.BlockSpec((1, D_MODEL), lambda i: (0, 0)),
            pl.BlockSpec(memory_space=pl.ANY),
        ],
        out_specs=pl.BlockSpec((nt, D_MODEL), lambda i: (i, 0)),
        out_shape=jax.ShapeDtypeStruct((t_total, D_MODEL), F32),
        scratch_shapes=[pltpu.VMEM((N_SLOTS // SUB, NCH, SUB, LANES), I32)] * PEER_RING + [
            pltpu.VMEM((2, N_SLOTS, 2 * D_MODEL), BF16),
            pltpu.SemaphoreType.DMA((PEER_RING,))],
        compiler_params=pltpu.CompilerParams(
            dimension_semantics=("arbitrary",), vmem_limit_bytes=VMEM_LIMIT),
        name="peer",
    )(idx, gates, xn, x1, gf, uv)


def _sc_peer(idx, xn, gates, uv2, n_tokens):
    info = plsc.get_sparse_core_info()
    nc, lanes_n = info.num_cores, info.num_lanes
    nw = nc * info.num_subcores
    per = n_tokens // nw
    assert n_tokens % nw == 0 and per % 2 == 0
    nchunk = N_SLOTS // SC_ROWS
    qv = D_MODEL // (SC_PASSES * lanes_n)
    c0 = math.sqrt(2.0 / math.pi)
    mesh = plsc.VectorSubcoreMesh(core_axis_name="c", subcore_axis_name="s")
    dma = pltpu.SemaphoreType.DMA

    @functools.partial(
        pl.kernel, mesh=mesh,
        out_type=jax.ShapeDtypeStruct((n_tokens, D_MODEL), F32),
        scratch_types=[
            [pltpu.VMEM((nchunk, SC_ROWS), I32)] * 2,
            [pltpu.VMEM((D_MODEL,), F32)] * 2,
            [pltpu.VMEM((N_SLOTS,), F32)] * 2,
            [pltpu.VMEM((D_MODEL,), F32)] * 2,
            [pltpu.VMEM((SC_ROWS, D_MODEL), I32)] * 2,
            pltpu.VMEM((SC_ROWS, lanes_n), F32),
            pltpu.VMEM((SC_ROWS,), F32),
            [dma] * 2, [dma] * 2, [dma] * 2, [dma] * 2, [dma] * 2,
        ],
        compiler_params=pltpu.CompilerParams(needs_layout_passes=False),
        name="peer_sc",
    )
    def sc_kernel(idx_hbm, xn_hbm, gate_hbm, uv_hbm, out_hbm, idx_v, x_v, g_v, out_v, rows_v,
                  part_v, w_v, row_sem, idx_sem, x_sem, g_sem, out_sem):
        wid = lax.axis_index("s") * nc + lax.axis_index("c")
        t0 = wid * per
        lane_ids = lax.iota(I32, lanes_n)
        zero = jnp.zeros((lanes_n,), F32)

        def vec(q, j):
            return pl.ds((q * qv + j) * lanes_n, lanes_n)

        def gather(p, c, b):
            return pltpu.make_async_copy(uv_hbm.at[idx_v[p].at[c]], rows_v[b], row_sem[b])

        def inputs(tok, p):
            return (pltpu.make_async_copy(idx_hbm.at[tok], idx_v[p], idx_sem[p]),
                    pltpu.make_async_copy(xn_hbm.at[tok], x_v[p], x_sem[p]),
                    pltpu.make_async_copy(gate_hbm.at[tok], g_v[p], g_sem[p]))

        def result(tok, p):
            return pltpu.make_async_copy(out_v[p], out_hbm.at[tok], out_sem[p])

        def chunk(p, c, b):
            rows = rows_v[b]

            def ustep(j, acc):
                xj = x_v[p][pl.ds(j * lanes_n, lanes_n)]
                return tuple(
                    acc[r] + plsc.bitcast(rows[r, pl.ds(j * lanes_n, lanes_n)] & U_HALF, F32) * xj
                    for r in range(SC_ROWS))

            acc = lax.fori_loop(0, D_MODEL // lanes_n, ustep, (zero,) * SC_ROWS, unroll=2)
            for r in range(SC_ROWS):
                part_v[r, :] = acc[r]
            cols = [plsc.load_gather(part_v, [lane_ids, jnp.full((lanes_n,), l, I32)])
                    for l in range(lanes_n)]
            while len(cols) > 1:
                cols = [cols[i] + cols[i + 1] for i in range(0, len(cols), 2)]
            act = cols[0]
            z = c0 * (act + 0.044715 * (act * act * act))
            tanh_z = 1.0 - 2.0 / (jnp.exp(2.0 * z) + 1.0)
            w = 0.5 * act * (1.0 + tanh_z) * g_v[p][pl.ds(c * SC_ROWS, SC_ROWS)]
            wr = [w.at[jnp.full((lanes_n,), r, I32)].get(mode="promise_in_bounds")
                  for r in range(SC_ROWS)]

            for q in range(SC_PASSES):
                o = [out_v[p][vec(q, j)] for j in range(qv)]
                for r in range(SC_ROWS):
                    for j in range(qv):
                        o[j] = o[j] + wr[r] * plsc.bitcast(rows[r, vec(q, j)] << 16, F32)
                for j in range(qv):
                    out_v[p][vec(q, j)] = o[j]

        def token(i, p):
            tok = t0 + i
            more = i + 1 < per

            @pl.when(more)
            def _():
                for cp in inputs(tok + 1, 1 - p):
                    cp.start()

            @pl.when(i >= 2)
            def _():
                result(tok - 2, p).wait()

            for j in range(D_MODEL // lanes_n):
                out_v[p][pl.ds(j * lanes_n, lanes_n)] = zero

            @pl.loop(0, nchunk, step=2)
            def _(c):
                gather(p, c + 1, 1).start()
                gather(p, c, 0).wait()
                chunk(p, c, 0)

                @pl.when(c + 2 < nchunk)
                def _():
                    gather(p, c + 2, 0).start()

                @pl.when(jnp.logical_and(c + 2 >= nchunk, more))
                def _():
                    for cp in inputs(tok + 1, 1 - p):
                        cp.wait()
                    gather(1 - p, 0, 0).start()

                gather(p, c + 1, 1).wait()
                chunk(p, c + 1, 1)

            result(tok, p).start()

        for cp in inputs(t0, 0):
            cp.start()
        for cp in inputs(t0, 0):
            cp.wait()
        gather(0, 0, 0).start()

        @pl.loop(0, per, step=2)
        def _(i):
            token(i, 0)
            token(i + 1, 1)

        result(t0 + per - 2, 0).wait()
        result(t0 + per - 1, 1).wait()

    return sc_kernel(idx.reshape(-1, nchunk, SC_ROWS), xn, gates, uv2)


def _residual_norm_kernel(x1_ref, o_ref, gf_ref, y_ref):
    y_ref[...] = _rms(x1_ref[...] + o_ref[...], gf_ref[...])


def _residual_norm(x1, out, gf):
    n = out.shape[0]
    nt = PEER_BLOCK
    row = lambda i: (i, 0)
    return pl.pallas_call(
        _residual_norm_kernel,
        grid=(n // nt,),
        in_specs=[pl.BlockSpec((nt, D_MODEL), row), pl.BlockSpec((nt, D_MODEL), row),
                  pl.BlockSpec((1, D_MODEL), lambda i: (0, 0))],
        out_specs=pl.BlockSpec((nt, D_MODEL), row),
        out_shape=jax.ShapeDtypeStruct((n, D_MODEL), F32),
        compiler_params=pltpu.CompilerParams(dimension_semantics=("arbitrary",)),
        name="residual_norm",
    )(x1, out, gf)


def _rope_tables(seq):
    pos = jnp.arange(seq, dtype=F32)
    inv = 1.0 / (ROPE_THETA ** (jnp.arange(0, HEAD_DIM, 2, dtype=F32) / HEAD_DIM))
    ang = pos[:, None] * inv[None, :]
    cos, sin = jnp.cos(ang), jnp.sin(ang)
    cosf = jnp.tile(jnp.concatenate([cos, cos], axis=1), (1, B_HEADS))
    sins = jnp.tile(jnp.concatenate([-sin, sin], axis=1), (1, B_HEADS))
    return cosf, sins


def kernel(x, norm1_g, w_in, ln_v_g, ln_v_b, w_spatial, b_spatial, out_norm_a_g, out_norm_b_g,
           w_out, norm2_g, w_query, sub_keys, expert_u, expert_v, final_norm_g):
    batch, seq, _ = x.shape
    assert w_in.shape[0] == 1 and seq % (16 * QBLK) == 0 and seq % IN_BLOCK == 0
    row = lambda g: g.reshape(1, -1).astype(F32)

    ws = w_spatial[0].astype(BF16)
    ws_cat = jnp.concatenate([ws[0::2], ws[1::2]], axis=2)
    bs_full = jnp.repeat(b_spatial[0].T, A_GROUP_DIM, axis=1)
    cosf, sins = _rope_tables(seq)
    win, wout, wq = w_in[0].astype(BF16), w_out[0].astype(BF16), w_query[0].astype(BF16)
    keys = sub_keys[0].astype(BF16)
    gf = row(final_norm_g)
    half = lambda t: lax.bitcast_convert_type(t.astype(BF16), jnp.uint16).astype(jnp.uint32)
    uv2 = lax.bitcast_convert_type((half(expert_u[0]) << 16) | half(expert_v[0]), I32)
    uv = uv2.reshape(-1, NCH, LANES)

    nchunk = PIPE_CHUNKS if batch % PIPE_CHUNKS == 0 else 1
    cb = batch // nchunk
    pieces = []
    for ci in range(nchunk):
        x2 = x[ci * cb:(ci + 1) * cb].reshape(cb * seq, D_MODEL)
        an, q1, k1, v1, q4, k4, v4, q16, k16, v16 = _in_proj(
            x2, row(norm1_g[0]), win, row(ln_v_g[0]), row(ln_v_b[0]),
            ws_cat, bs_full, row(out_norm_a_g[0]), cosf, sins, cb, seq)
        bo = _attention(q1, k1, v1, q4, k4, v4, q16, k16, v16, cb, seq).reshape(cb * seq, D_B)
        x1, xn, idx, gates = _mid(x2, an, bo, row(out_norm_b_g[0]), wout, row(norm2_g[0]), wq, keys)
        n_sc = (cb * seq) * SC_SHARE[0] // SC_SHARE[1]
        assert n_sc % PEER_BLOCK == 0
        out_sc = _sc_peer(idx.reshape(cb * seq, N_SLOTS), xn, gates, uv2, n_sc)
        y_tc = _peer(idx, gates, xn, x1, gf, uv, n_sc)
        pieces += [_residual_norm(x1, out_sc, gf), y_tc]
    return jnp.concatenate(pieces, axis=0).reshape(batch, seq, D_MODEL)
```

```python
import functools
import math

import jax
import jax.numpy as jnp
from jax import lax
from jax.experimental import pallas as pl
from jax.experimental.pallas import tpu as pltpu
from jax.experimental.pallas import tpu_sc as plsc

F32 = jnp.float32
BF16 = jnp.bfloat16
I32 = jnp.int32

D_MODEL = 1024
D_A = 512
D_B = 512
A_GROUPS = 8
A_GROUP_DIM = 64
CHUNK = 128
B_HEADS = 8
HEAD_DIM = 64
DILATIONS = (1, 4, 16)
HALF_WINDOW = 64
ROPE_THETA = 10000.0
D_IN = 2 * D_A + 3 * D_B
N_KEYS = 128
PEER_HEADS = 8
PEER_TOPK = 16
D_KEY = 256
N_SLOTS = PEER_HEADS * PEER_TOPK
EPS = 1e-6
NEG_BIG = -1e30

LANES = 128
SUB = 8
NCH = D_MODEL // LANES
U_HALF = -65536
QBLK = 128
ATTN_UNROLL = 4
RES16_PITCH = 24
IN_BLOCK = 512
MID_BLOCK = 256
PEER_BLOCK = 128
PEER_RING = 4
SC_SHARE = (45, 64)
PIPE_CHUNKS = 8
SC_ROWS = 16
SC_PASSES = 4
VMEM_LIMIT = 48 * 1024 * 1024


def _gelu(x):
    c = math.sqrt(2.0 / math.pi)
    return 0.5 * x * (1.0 + jnp.tanh(c * (x + 0.044715 * (x * x * x))))


def _rms(x, g):
    return x * lax.rsqrt(jnp.mean(x * x, axis=-1, keepdims=True) + EPS) * g


def _in_proj_kernel(x_ref, g1_ref, win_ref, lng_ref, lnb_ref, ws_ref, bs_ref, ga_ref,
                    cos_ref, sin_ref,
                    an_ref, q1_ref, k1_ref, v1_ref, q4_ref, k4_ref, v4_ref,
                    q16_ref, k16_ref, v16_ref, slab_ref):
    nt = x_ref.shape[0]
    h = _rms(x_ref[...], g1_ref[...]).astype(BF16)
    proj = jnp.dot(h, win_ref[...], preferred_element_type=F32)

    u = _gelu(proj[:, :D_A])
    v = _gelu(proj[:, D_A:2 * D_A])
    mu = jnp.mean(v, axis=-1, keepdims=True)
    vc = v - mu
    var = jnp.mean(vc * vc, axis=-1, keepdims=True)
    vln = (vc * lax.rsqrt(var + EPS) * lng_ref[...] + lnb_ref[...]).astype(BF16)
    lane = lax.broadcasted_iota(I32, (CHUNK, LANES), 1)
    lo = lane < A_GROUP_DIM
    zero = jnp.zeros((CHUNK, LANES), BF16)
    chunks = []
    for c in range(nt // CHUNK):
        cols = []
        for j in range(A_GROUPS // 2):
            vv = vln[c * CHUNK:(c + 1) * CHUNK, j * LANES:(j + 1) * LANES]
            rhs = jnp.concatenate([jnp.where(lo, vv, zero), jnp.where(lo, zero, vv)], axis=0)
            cols.append(jnp.dot(ws_ref[j], rhs, preferred_element_type=F32))
        chunks.append(jnp.concatenate(cols, axis=1) + bs_ref[...])
    mixed = jnp.concatenate(chunks, axis=0)
    an_ref[...] = _rms(u * mixed, ga_ref[...]).astype(BF16)

    cosf = cos_ref[...]
    sins = sin_ref[...]
    lane_b = lax.broadcasted_iota(I32, (nt, D_B), 1)
    first_half = (lane_b % HEAD_DIM) < (HEAD_DIM // 2)

    def rope(t):
        partner = jnp.where(first_half, pltpu.roll(t, D_B - HEAD_DIM // 2, 1),
                            pltpu.roll(t, HEAD_DIM // 2, 1))
        return t * cosf + partner * sins

    q = rope(proj[:, 2 * D_A:2 * D_A + D_B]) * (HEAD_DIM ** -0.5)
    k = rope(proj[:, 2 * D_A + D_B:2 * D_A + 2 * D_B])
    vv = proj[:, 2 * D_A + 2 * D_B:]
    q1_ref[...] = q.astype(BF16)
    k1_ref[...] = k.astype(BF16)
    v1_ref[...] = vv.astype(BF16)

    nslab = D_B // LANES
    for a, t in enumerate((q, k, vv)):
        for s in range(nslab):
            slab_ref[a * nslab + s] = t[:, s * LANES:(s + 1) * LANES]
    for d, outs in ((4, (q4_ref, k4_ref, v4_ref)), (16, (q16_ref, k16_ref, v16_ref))):
        rows = nt // d
        for a, o_ref in enumerate(outs):
            for r in range(d):
                for s in range(nslab):
                    o_ref[r, :, s * LANES:(s + 1) * LANES] = (
                        slab_ref[a * nslab + s, pl.ds(r, rows, stride=d), :].astype(BF16))


def _in_proj(x2, g1, win, lng, lnb, ws_cat, bs_full, ga, cosf, sins, batch, seq):
    t_total = x2.shape[0]
    nt = IN_BLOCK
    nb = seq // nt
    grid = (t_total // nt,)
    row = lambda i: (i, 0)
    const2 = lambda i: (0, 0)
    tok_bf = jax.ShapeDtypeStruct((t_total, D_B), BF16)
    out_shape = (
        jax.ShapeDtypeStruct((t_total, D_A), BF16),
        tok_bf, tok_bf, tok_bf,
        *(jax.ShapeDtypeStruct((batch, 4, seq // 4, D_B), BF16),) * 3,
        *(jax.ShapeDtypeStruct((batch, 16, seq // 16, D_B), BF16),) * 3,
    )
    res4 = pl.BlockSpec((None, 4, nt // 4, D_B), lambda i: (i // nb, 0, i % nb, 0))
    res16 = pl.BlockSpec((None, 16, nt // 16, D_B), lambda i: (i // nb, 0, i % nb, 0))
    tok_spec = pl.BlockSpec((nt, D_B), row)
    return pl.pallas_call(
        _in_proj_kernel,
        grid=grid,
        in_specs=[
            pl.BlockSpec((nt, D_MODEL), row),
            pl.BlockSpec((1, D_MODEL), const2),
            pl.BlockSpec((D_MODEL, D_IN), const2),
            pl.BlockSpec((1, D_A), const2),
            pl.BlockSpec((1, D_A), const2),
            pl.BlockSpec((A_GROUPS // 2, CHUNK, 2 * CHUNK), lambda i: (0, 0, 0)),
            pl.BlockSpec((CHUNK, D_A), const2),
            pl.BlockSpec((1, D_A), const2),
            pl.BlockSpec((nt, D_B), lambda i: (i % nb, 0)),
            pl.BlockSpec((nt, D_B), lambda i: (i % nb, 0)),
        ],
        out_specs=(pl.BlockSpec((nt, D_A), row), tok_spec, tok_spec, tok_spec,
                   res4, res4, res4, res16, res16, res16),
        out_shape=out_shape,
        scratch_shapes=[pltpu.VMEM((3 * D_B // LANES, nt, LANES), F32)],
        compiler_params=pltpu.CompilerParams(
            dimension_semantics=("arbitrary",), vmem_limit_bytes=VMEM_LIMIT),
        name="in_proj",
    )(x2, g1, win, lng, lnb, ws_cat, bs_full, ga, cosf, sins)


def _attn_kernel(q1_ref, k1_ref, v1_ref, q4_ref, k4_ref, v4_ref, q16_ref, k16_ref, v16_ref,
                 o_ref, out_ref, lse_ref, out16_ref, lse16_ref):
    seq = o_ref.shape[0]
    lane = lax.broadcasted_iota(I32, (QBLK, LANES), 1)
    head0 = lane < HEAD_DIM
    branches = ((1, q1_ref, k1_ref, v1_ref), (4, q4_ref, k4_ref, v4_ref),
                (16, q16_ref, k16_ref, v16_ref))
    for bi, (d, q_ref, k_ref, v_ref) in enumerate(branches):
        length = seq // d
        nblk = length // QBLK
        win = min(2 * QBLK, length)
        diff = (lax.broadcasted_iota(I32, (QBLK, win), 1)
                - lax.broadcasted_iota(I32, (QBLK, win), 0))

        def block(blk, carry, d=d, bi=bi, q_ref=q_ref, k_ref=k_ref, v_ref=v_ref,
                  length=length, nblk=nblk, win=win, diff=diff):
            r = blk // nblk
            i0 = pl.multiple_of((blk % nblk) * QBLK, QBLK)
            w0 = pl.multiple_of(jnp.clip(i0 - HALF_WINDOW, 0, length - win), HALF_WINDOW)
            qb = q_ref[r, pl.ds(i0, QBLK), :]
            kw = k_ref[r, pl.ds(w0, win), :]
            vw = v_ref[r, pl.ds(w0, win), :]
            rel = diff + (w0 - i0)
            valid = (rel >= -HALF_WINDOW) & (rel <= HALF_WINDOW)
            zero = jnp.zeros_like(qb)
            qq = jnp.concatenate([jnp.where(head0, qb, zero), jnp.where(head0, zero, qb)], axis=0)
            s = lax.dot_general(qq, kw, (((1,), (1,)), ((), ())), preferred_element_type=F32)
            s = jnp.where(jnp.concatenate([valid, valid], axis=0), s, NEG_BIG)
            m = jnp.max(s, axis=1, keepdims=True)
            p = jnp.exp(s - m)
            l = jnp.sum(p, axis=1, keepdims=True)
            pv = jnp.dot(p.astype(BF16), vw, preferred_element_type=F32) / l
            ml = m + jnp.log(l)
            out = jnp.where(head0, pv[:QBLK], pv[QBLK:])
            lse = jnp.where(head0, ml[:QBLK], ml[QBLK:])
            if d == 1:
                out_ref[0, pl.ds(i0, QBLK), :] = out
                lse_ref[0, pl.ds(i0, QBLK), :] = lse
            elif d == 4:
                rows = pl.ds(i0 * d + r, QBLK, stride=d)
                out_ref[1, rows, :] = out
                lse_ref[1, rows, :] = lse
            else:
                rows = pl.ds(r, QBLK, stride=RES16_PITCH)
                out16_ref[rows, :] = out
                lse16_ref[rows, :] = lse
            return carry

        lax.fori_loop(0, d * nblk, block, 0, unroll=ATTN_UNROLL)

    groups = QBLK // 16

    def merge(c, carry):
        rows = pl.ds(pl.multiple_of(c * QBLK, QBLK), QBLK)
        base = pl.multiple_of(c * (groups * RES16_PITCH), SUB)
        pieces = [pl.ds(base + g * RES16_PITCH, 16) for g in range(groups)]
        o3 = jnp.concatenate([out16_ref[pc, :] for pc in pieces], axis=0)
        e3 = jnp.concatenate([lse16_ref[pc, :] for pc in pieces], axis=0)
        e1, e2 = lse_ref[0, rows, :], lse_ref[1, rows, :]
        mx = jnp.maximum(jnp.maximum(e1, e2), e3)
        w1, w2, w3 = jnp.exp(e1 - mx), jnp.exp(e2 - mx), jnp.exp(e3 - mx)
        num = w1 * out_ref[0, rows, :] + w2 * out_ref[1, rows, :] + w3 * o3
        o_ref[rows, :] = num / (w1 + w2 + w3)
        return carry

    lax.fori_loop(0, seq // QBLK, merge, 0)


def _attention(q1, k1, v1, q4, k4, v4, q16, k16, v16, batch, seq):
    npair = D_B // LANES
    nat = pl.BlockSpec((None, 1, seq, LANES), lambda b, p: (b, 0, 0, p))
    r4 = pl.BlockSpec((None, 4, seq // 4, LANES), lambda b, p: (b, 0, 0, p))
    r16 = pl.BlockSpec((None, 16, seq // 16, LANES), lambda b, p: (b, 0, 0, p))
    q1, k1, v1 = (t.reshape(batch, 1, seq, D_B) for t in (q1, k1, v1))
    return pl.pallas_call(
        _attn_kernel,
        grid=(batch, npair),
        in_specs=[nat, nat, nat, r4, r4, r4, r16, r16, r16],
        out_specs=pl.BlockSpec((None, seq, LANES), lambda b, p: (b, 0, p)),
        out_shape=jax.ShapeDtypeStruct((batch, seq, D_B), F32),
        scratch_shapes=[pltpu.VMEM((2, seq, LANES), F32)] * 2
        + [pltpu.VMEM((seq // 16 * RES16_PITCH, LANES), F32)] * 2,
        compiler_params=pltpu.CompilerParams(
            dimension_semantics=("arbitrary", "arbitrary"), vmem_limit_bytes=VMEM_LIMIT),
        name="dilated_attn",
    )(q1, k1, v1, q4, k4, v4, q16, k16, v16)


def _topk_rows(s, k):
    n = s.shape[0]
    iota = lax.broadcasted_iota(I32, s.shape, 0).astype(F32)
    vals, idxs = [], []
    for _ in range(k):
        m = jnp.max(s, axis=0, keepdims=True)
        i = jnp.min(jnp.where(s == m, iota, float(n)), axis=0, keepdims=True)
        vals.append(m)
        idxs.append(i)
        s = jnp.where(iota == i, -jnp.inf, s)
    return jnp.concatenate(vals, axis=0), jnp.concatenate(idxs, axis=0).astype(I32)


def _take_rows(table, sel):
    out = jnp.zeros(sel.shape, table.dtype)
    for a in range(table.shape[0]):
        out = jnp.where(sel == a, table[a:a + 1, :], out)
    return out


def _mid_kernel(x_ref, an_ref, bo_ref, gb_ref, wout_ref, g2_ref, wq_ref, keys_ref,
                x1_ref, xn_ref, idx_ref, gate_ref):
    nt = x_ref.shape[0]
    bn = _rms(bo_ref[...], gb_ref[...]).astype(BF16)
    x1 = (x_ref[...]
          + jnp.dot(an_ref[...], wout_ref[:D_A, :], preferred_element_type=F32)
          + jnp.dot(bn, wout_ref[D_A:, :], preferred_element_type=F32))
    x1_ref[...] = x1
    xn = _rms(x1, g2_ref[...])
    xn_ref[...] = xn
    q = jnp.dot(xn.astype(BF16), wq_ref[...], preferred_element_type=F32).astype(BF16)
    keys = (keys_ref[0], keys_ref[1])
    half = D_KEY // 2
    for c in range(nt // LANES):
        qc = q[c * LANES:(c + 1) * LANES, :]
        experts, gates = [], []
        for h in range(PEER_HEADS):
            tops = []
            for p in range(2):
                qhp = qc[:, (2 * h + p) * half:(2 * h + p + 1) * half]
                s = lax.dot_general(keys[p], qhp, (((1,), (1,)), ((), ())),
                                    preferred_element_type=F32)
                tops.append(_topk_rows(s, PEER_TOPK))
            (s1, i1), (s2, i2) = tops
            cand = jnp.concatenate(
                [s1[0:1, :] + s2]
                + [s1[a:a + 1, :] + s2[0:SUB, :] for a in range(1, SUB)]
                + [s1[SUB:, :] + s2[0:1, :]], axis=0)
            sc, pos = _topk_rows(cand, PEER_TOPK)
            ca = jnp.where(pos < PEER_TOPK, 0,
                           jnp.where(pos < PEER_TOPK + SUB * (SUB - 1), (pos >> 3) - 1, pos - SUB * SUB))
            cb = jnp.where(pos < PEER_TOPK, pos,
                           jnp.where(pos < PEER_TOPK + SUB * (SUB - 1), pos & (SUB - 1), 0))
            e = _take_rows(i1, ca) * N_KEYS + _take_rows(i2, cb)
            ex = jnp.exp(sc - sc[0:1, :])
            gates.append(ex / jnp.sum(ex, axis=0, keepdims=True))
            experts.append(e)
        idx_ref[c] = jnp.concatenate(experts, axis=0).T
        gate_ref[c * LANES:(c + 1) * LANES, :] = jnp.concatenate(gates, axis=0).T


def _mid(x2, an, bo, gb, wout, g2, wq, keys):
    t_total = x2.shape[0]
    nt = MID_BLOCK
    row = lambda i: (i, 0)
    const2 = lambda i: (0, 0)
    return pl.pallas_call(
        _mid_kernel,
        grid=(t_total // nt,),
        in_specs=[
            pl.BlockSpec((nt, D_MODEL), row),
            pl.BlockSpec((nt, D_A), row),
            pl.BlockSpec((nt, D_B), row),
            pl.BlockSpec((1, D_B), const2),
            pl.BlockSpec((D_MODEL, D_MODEL), const2),
            pl.BlockSpec((1, D_MODEL), const2),
            pl.BlockSpec((D_MODEL, PEER_HEADS * D_KEY), const2),
            pl.BlockSpec((2, N_KEYS, D_KEY // 2), lambda i: (0, 0, 0)),
        ],
        out_specs=(
            pl.BlockSpec((nt, D_MODEL), row),
            pl.BlockSpec((nt, D_MODEL), row),
            pl.BlockSpec((nt // LANES, LANES, N_SLOTS), lambda i: (i, 0, 0)),
            pl.BlockSpec((nt, N_SLOTS), row),
        ),
        out_shape=(
            jax.ShapeDtypeStruct((t_total, D_MODEL), F32),
            jax.ShapeDtypeStruct((t_total, D_MODEL), F32),
            jax.ShapeDtypeStruct((t_total // LANES, LANES, N_SLOTS), I32),
            jax.ShapeDtypeStruct((t_total, N_SLOTS), F32),
        ),
        compiler_params=pltpu.CompilerParams(
            dimension_semantics=("arbitrary",), vmem_limit_bytes=VMEM_LIMIT),
        name="mid",
    )(x2, an, bo, gb, wout, g2, wq, keys)


def _peer_kernel(idx_ref, gate_ref, xn_ref, x1_ref, gf_ref, uv_ref, y_ref, *scratch):
    rows_refs = scratch[:PEER_RING]
    bf_ref, sem_ref = scratch[PEER_RING:]
    nt = xn_ref.shape[0]
    wide = 2 * LANES
    nw = 2 * D_MODEL // wide

    def issue(t, slot, lo=0, hi=N_SLOTS):
        for s in range(lo, hi):
            pltpu.make_async_copy(uv_ref.at[idx_ref[t, s]],
                                  rows_refs[slot].at[s // SUB, :, s % SUB, :],
                                  sem_ref.at[slot]).start(priority=s % 2)

    def wait(slot):
        pltpu.make_async_copy(uv_ref.at[pl.ds(0, N_SLOTS)],
                              rows_refs[slot].reshape(N_SLOTS, NCH, LANES), sem_ref.at[slot]).wait()

    def stage(slot, k):
        for j in range(NCH):
            w = rows_refs[slot][:, j, :, :].reshape(N_SLOTS, LANES)
            bf_ref[k, :, j * LANES:(j + 1) * LANES] = (
                pltpu.bitcast(w & U_HALF, F32).astype(BF16))
            bf_ref[k, :, D_MODEL + j * LANES:D_MODEL + (j + 1) * LANES] = (
                pltpu.bitcast(w << 16, F32).astype(BF16))

    def pair(t0, slots, prefetch):
        for k in range(2):
            wait(slots[k])
            stage(slots[k], k)
        nbatch = 2 * nw
        per = N_SLOTS // (nbatch // 2)
        batches = [(k, b * per, (b + 1) * per) for b in range(nbatch // 2) for k in range(2)]

        def next_batch():
            if prefetch and batches:
                k, lo, hi = batches.pop(0)
                issue(t0 + PEER_RING + k, slots[k], lo, hi)

        x8 = [jnp.broadcast_to(xn_ref[pl.ds(t0 + k, 1), :], (SUB, D_MODEL)).astype(BF16)
              for k in range(2)]
        act = [jnp.zeros((SUB, N_SLOTS), F32) for _ in range(2)]
        for j in range(nw // 2):
            for k in range(2):
                next_batch()
                act[k] = act[k] + lax.dot_general(
                    x8[k][:, j * wide:(j + 1) * wide], bf_ref[k, :, j * wide:(j + 1) * wide],
                    (((1,), (1,)), ((), ())), preferred_element_type=F32)
        w = [(_gelu(act[k]) * gate_ref[pl.ds(t0 + k, 1), :]).astype(BF16) for k in range(2)]
        outs = [[], []]
        for j in range(nw // 2):
            for k in range(2):
                next_batch()
                outs[k].append(jnp.dot(
                    w[k], bf_ref[k, :, D_MODEL + j * wide:D_MODEL + (j + 1) * wide],
                    preferred_element_type=F32)[0:1, :])
        for k in range(2):
            out = jnp.concatenate(outs[k], axis=1)
            y_ref[pl.ds(t0 + k, 1), :] = _rms(x1_ref[pl.ds(t0 + k, 1), :] + out, gf_ref[...])

    def group(g, prefetch):
        for p in range(PEER_RING // 2):
            pair(g * PEER_RING + 2 * p, (2 * p, 2 * p + 1), prefetch)

    for t in range(PEER_RING):
        issue(t, t)
    ngroup = nt // PEER_RING
    lax.fori_loop(0, ngroup - 1, lambda g, c: (group(g, True), c)[1], 0)
    group(ngroup - 1, False)


def _peer(idx, gates, xn, x1, gf, uv, first_token):
    t_total = xn.shape[0] - first_token
    nt = PEER_BLOCK
    b0 = first_token // nt
    row = lambda i: (i + b0, 0)
    return pl.pallas_call(
        _peer_kernel,
        grid=(t_total // nt,),
        in_specs=[
            pl.BlockSpec((None, nt, N_SLOTS), lambda i: (i + b0, 0, 0), memory_space=pltpu.SMEM),
            pl.BlockSpec((nt, N_SLOTS), row),
            pl.BlockSpec((nt, D_MODEL), row),
            pl.BlockSpec((nt, D_MODEL), row),
            pl.BlockSpec((1, D_MODEL), lambda i: (0, 0)),
            pl.BlockSpec(memory_space=pl.ANY),
        ],
        out_specs=pl.BlockSpec((nt, D_MODEL), lambda i: (i, 0)),
        out_shape=jax.ShapeDtypeStruct((t_total, D_MODEL), F32),
        scratch_shapes=[pltpu.VMEM((N_SLOTS // SUB, NCH, SUB, LANES), I32)] * PEER_RING + [
            pltpu.VMEM((2, N_SLOTS, 2 * D_MODEL), BF16),
            pltpu.SemaphoreType.DMA((PEER_RING,))],
        compiler_params=pltpu.CompilerParams(
            dimension_semantics=("arbitrary",), vmem_limit_bytes=VMEM_LIMIT),
        name="peer",
    )(idx, gates, xn, x1, gf, uv)


def _sc_peer(idx, xn, gates, uv2, n_tokens):
    info = plsc.get_sparse_core_info()
    nc, lanes_n = info.num_cores, info.num_lanes
    nw = nc * info.num_subcores
    per = n_tokens // nw
    assert n_tokens % nw == 0 and per % 2 == 0
    nchunk = N_SLOTS // SC_ROWS
    qv = D_MODEL // (SC_PASSES * lanes_n)
    c0 = math.sqrt(2.0 / math.pi)
    mesh = plsc.VectorSubcoreMesh(core_axis_name="c", subcore_axis_name="s")
    dma = pltpu.SemaphoreType.DMA

    @functools.partial(
        pl.kernel, mesh=mesh,
        out_type=jax.ShapeDtypeStruct((n_tokens, D_MODEL), F32),
        scratch_types=[
            [pltpu.VMEM((nchunk, SC_ROWS), I32)] * 2,
            [pltpu.VMEM((D_MODEL,), F32)] * 2,
            [pltpu.VMEM((N_SLOTS,), F32)] * 2,
            [pltpu.VMEM((D_MODEL,), F32)] * 2,
            [pltpu.VMEM((SC_ROWS, D_MODEL), I32)] * 2,
            pltpu.VMEM((SC_ROWS, lanes_n), F32),
            [dma] * 2, [dma] * 2, [dma] * 2, [dma] * 2, [dma] * 2,
        ],
        compiler_params=pltpu.CompilerParams(needs_layout_passes=False),
        name="peer_sc",
    )
    def sc_kernel(idx_hbm, xn_hbm, gate_hbm, uv_hbm, out_hbm, idx_v, x_v, g_v, out_v, rows_v,
                  part_v, row_sem, idx_sem, x_sem, g_sem, out_sem):
        wid = lax.axis_index("s") * nc + lax.axis_index("c")
        t0 = wid * per
        lane_ids = lax.iota(I32, lanes_n)
        zero = jnp.zeros((lanes_n,), F32)

        def vec(q, j):
            return pl.ds((q * qv + j) * lanes_n, lanes_n)

        def gather(p, c, b):
            return pltpu.make_async_copy(uv_hbm.at[idx_v[p].at[c]], rows_v[b], row_sem[b])

        def inputs(tok, p):
            return (pltpu.make_async_copy(idx_hbm.at[tok], idx_v[p], idx_sem[p]),
                    pltpu.make_async_copy(xn_hbm.at[tok], x_v[p], x_sem[p]),
                    pltpu.make_async_copy(gate_hbm.at[tok], g_v[p], g_sem[p]))

        def result(tok, p):
            return pltpu.make_async_copy(out_v[p], out_hbm.at[tok], out_sem[p])

        def chunk(p, c, b):
            rows = rows_v[b]

            def ustep(j, acc):
                xj = x_v[p][pl.ds(j * lanes_n, lanes_n)]
                return tuple(
                    acc[r] + plsc.bitcast(rows[r, pl.ds(j * lanes_n, lanes_n)] & U_HALF, F32) * xj
                    for r in range(SC_ROWS))

            acc = lax.fori_loop(0, D_MODEL // lanes_n, ustep, (zero,) * SC_ROWS)
            for r in range(SC_ROWS):
                part_v[r, :] = acc[r]
            cols = [plsc.load_gather(part_v, [lane_ids, jnp.full((lanes_n,), l, I32)])
                    for l in range(lanes_n)]
            while len(cols) > 1:
                cols = [cols[i] + cols[i + 1] for i in range(0, len(cols), 2)]
            act = cols[0]
            z = c0 * (act + 0.044715 * (act * act * act))
            tanh_z = 1.0 - 2.0 / (jnp.exp(2.0 * z) + 1.0)
            w = 0.5 * act * (1.0 + tanh_z) * g_v[p][pl.ds(c * SC_ROWS, SC_ROWS)]
            for q in range(SC_PASSES):
                o = tuple(out_v[p][vec(q, j)] for j in range(qv))

                def vrow(r, o, q=q):
                    wr = w.at[jnp.full((lanes_n,), r, I32)].get(mode="promise_in_bounds")
                    return tuple(o[j] + wr * plsc.bitcast(rows[r, vec(q, j)] << 16, F32)
                                 for j in range(qv))

                o = lax.fori_loop(0, SC_ROWS, vrow, o)
                for j in range(qv):
                    out_v[p][vec(q, j)] = o[j]

        def token(i, p):
            tok = t0 + i
            more = i + 1 < per

            @pl.when(more)
            def _():
                for cp in inputs(tok + 1, 1 - p):
                    cp.start()

            @pl.when(i >= 2)
            def _():
                result(tok - 2, p).wait()

            for j in range(D_MODEL // lanes_n):
                out_v[p][pl.ds(j * lanes_n, lanes_n)] = zero

            @pl.loop(0, nchunk, step=2)
            def _(c):
                gather(p, c + 1, 1).start()
                gather(p, c, 0).wait()
                chunk(p, c, 0)

                @pl.when(c + 2 < nchunk)
                def _():
                    gather(p, c + 2, 0).start()

                @pl.when(jnp.logical_and(c + 2 >= nchunk, more))
                def _():
                    for cp in inputs(tok + 1, 1 - p):
                        cp.wait()
                    gather(1 - p, 0, 0).start()

                gather(p, c + 1, 1).wait()
                chunk(p, c + 1, 1)

            result(tok, p).start()

        for cp in inputs(t0, 0):
            cp.start()
        for cp in inputs(t0, 0):
            cp.wait()
        gather(0, 0, 0).start()

        @pl.loop(0, per, step=2)
        def _(i):
            token(i, 0)
            token(i + 1, 1)

        result(t0 + per - 2, 0).wait()
        result(t0 + per - 1, 1).wait()

    return sc_kernel(idx.reshape(-1, nchunk, SC_ROWS), xn, gates, uv2)


def _residual_norm_kernel(x1_ref, o_ref, gf_ref, y_ref):
    y_ref[...] = _rms(x1_ref[...] + o_ref[...], gf_ref[...])


def _residual_norm(x1, out, gf):
    n = out.shape[0]
    nt = PEER_BLOCK
    row = lambda i: (i, 0)
    return pl.pallas_call(
        _residual_norm_kernel,
        grid=(n // nt,),
        in_specs=[pl.BlockSpec((nt, D_MODEL), row), pl.BlockSpec((nt, D_MODEL), row),
                  pl.BlockSpec((1, D_MODEL), lambda i: (0, 0))],
        out_specs=pl.BlockSpec((nt, D_MODEL), row),
        out_shape=jax.ShapeDtypeStruct((n, D_MODEL), F32),
        compiler_params=pltpu.CompilerParams(dimension_semantics=("arbitrary",)),
        name="residual_norm",
    )(x1, out, gf)


def _rope_tables(seq):
    pos = jnp.arange(seq, dtype=F32)
    inv = 1.0 / (ROPE_THETA ** (jnp.arange(0, HEAD_DIM, 2, dtype=F32) / HEAD_DIM))
    ang = pos[:, None] * inv[None, :]
    cos, sin = jnp.cos(ang), jnp.sin(ang)
    cosf = jnp.tile(jnp.concatenate([cos, cos], axis=1), (1, B_HEADS))
    sins = jnp.tile(jnp.concatenate([-sin, sin], axis=1), (1, B_HEADS))
    return cosf, sins


def kernel(x, norm1_g, w_in, ln_v_g, ln_v_b, w_spatial, b_spatial, out_norm_a_g, out_norm_b_g,
           w_out, norm2_g, w_query, sub_keys, expert_u, expert_v, final_norm_g):
    batch, seq, _ = x.shape
    assert w_in.shape[0] == 1 and seq % (16 * QBLK) == 0 and seq % IN_BLOCK == 0
    row = lambda g: g.reshape(1, -1).astype(F32)

    ws = w_spatial[0].astype(BF16)
    ws_cat = jnp.concatenate([ws[0::2], ws[1::2]], axis=2)
    bs_full = jnp.repeat(b_spatial[0].T, A_GROUP_DIM, axis=1)
    cosf, sins = _rope_tables(seq)
    win, wout, wq = w_in[0].astype(BF16), w_out[0].astype(BF16), w_query[0].astype(BF16)
    keys = sub_keys[0].astype(BF16)
    gf = row(final_norm_g)
    half = lambda t: lax.bitcast_convert_type(t.astype(BF16), jnp.uint16).astype(jnp.uint32)
    uv2 = lax.bitcast_convert_type((half(expert_u[0]) << 16) | half(expert_v[0]), I32)
    uv = uv2.reshape(-1, NCH, LANES)

    nchunk = PIPE_CHUNKS if batch % PIPE_CHUNKS == 0 else 1
    cb = batch // nchunk
    pieces = []
    for ci in range(nchunk):
        x2 = x[ci * cb:(ci + 1) * cb].reshape(cb * seq, D_MODEL)
        an, q1, k1, v1, q4, k4, v4, q16, k16, v16 = _in_proj(
            x2, row(norm1_g[0]), win, row(ln_v_g[0]), row(ln_v_b[0]),
            ws_cat, bs_full, row(out_norm_a_g[0]), cosf, sins, cb, seq)
        bo = _attention(q1, k1, v1, q4, k4, v4, q16, k16, v16, cb, seq).reshape(cb * seq, D_B)
        x1, xn, idx, gates = _mid(x2, an, bo, row(out_norm_b_g[0]), wout, row(norm2_g[0]), wq, keys)
        n_sc = (cb * seq) * SC_SHARE[0] // SC_SHARE[1]
        assert n_sc % PEER_BLOCK == 0
        out_sc = _sc_peer(idx.reshape(cb * seq, N_SLOTS), xn, gates, uv2, n_sc)
        y_tc = _peer(idx, gates, xn, x1, gf, uv, n_sc)
        pieces += [_residual_norm(x1, out_sc, gf), y_tc]
    return jnp.concatenate(pieces, axis=0).reshape(batch, seq, D_MODEL)
```

```python
import functools
import math

import jax
import jax.numpy as jnp
from jax import lax
from jax.experimental import pallas as pl
from jax.experimental.pallas import tpu as pltpu
from jax.experimental.pallas import tpu_sc as plsc

F32 = jnp.float32
BF16 = jnp.bfloat16
I32 = jnp.int32

D_MODEL = 1024
D_A = 512
D_B = 512
A_GROUPS = 8
A_GROUP_DIM = 64
CHUNK = 128
B_HEADS = 8
HEAD_DIM = 64
DILATIONS = (1, 4, 16)
HALF_WINDOW = 64
ROPE_THETA = 10000.0
D_IN = 2 * D_A + 3 * D_B
N_KEYS = 128
PEER_HEADS = 8
PEER_TOPK = 16
D_KEY = 256
N_SLOTS = PEER_HEADS * PEER_TOPK
EPS = 1e-6
NEG_BIG = -1e30

LANES = 128
SUB = 8
NCH = D_MODEL // LANES
U_HALF = -65536
QBLK = 128
ATTN_UNROLL = 4
RES16_PITCH = 24
IN_BLOCK = 512
MID_BLOCK = 256
PEER_BLOCK = 128
PEER_RING = 4
SC_SHARE = (11, 16)
PIPE_CHUNKS = 16
SC_ROWS = 16
SC_PASSES = 4
VMEM_LIMIT = 48 * 1024 * 1024


def _gelu(x):
    c = math.sqrt(2.0 / math.pi)
    return 0.5 * x * (1.0 + jnp.tanh(c * (x + 0.044715 * (x * x * x))))


def _rms(x, g):
    return x * lax.rsqrt(jnp.mean(x * x, axis=-1, keepdims=True) + EPS) * g


def _in_proj_kernel(x_ref, g1_ref, win_ref, lng_ref, lnb_ref, ws_ref, bs_ref, ga_ref,
                    cos_ref, sin_ref,
                    an_ref, q1_ref, k1_ref, v1_ref, q4_ref, k4_ref, v4_ref,
                    q16_ref, k16_ref, v16_ref, slab_ref):
    nt = x_ref.shape[0]
    h = _rms(x_ref[...], g1_ref[...]).astype(BF16)
    proj = jnp.dot(h, win_ref[...], preferred_element_type=F32)

    u = _gelu(proj[:, :D_A])
    v = _gelu(proj[:, D_A:2 * D_A])
    mu = jnp.mean(v, axis=-1, keepdims=True)
    vc = v - mu
    var = jnp.mean(vc * vc, axis=-1, keepdims=True)
    vln = (vc * lax.rsqrt(var + EPS) * lng_ref[...] + lnb_ref[...]).astype(BF16)
    lane = lax.broadcasted_iota(I32, (CHUNK, LANES), 1)
    lo = lane < A_GROUP_DIM
    zero = jnp.zeros((CHUNK, LANES), BF16)
    chunks = []
    for c in range(nt // CHUNK):
        cols = []
        for j in range(A_GROUPS // 2):
            vv = vln[c * CHUNK:(c + 1) * CHUNK, j * LANES:(j + 1) * LANES]
            rhs = jnp.concatenate([jnp.where(lo, vv, zero), jnp.where(lo, zero, vv)], axis=0)
            cols.append(jnp.dot(ws_ref[j], rhs, preferred_element_type=F32))
        chunks.append(jnp.concatenate(cols, axis=1) + bs_ref[...])
    mixed = jnp.concatenate(chunks, axis=0)
    an_ref[...] = _rms(u * mixed, ga_ref[...]).astype(BF16)

    cosf = cos_ref[...]
    sins = sin_ref[...]
    lane_b = lax.broadcasted_iota(I32, (nt, D_B), 1)
    first_half = (lane_b % HEAD_DIM) < (HEAD_DIM // 2)

    def rope(t):
        partner = jnp.where(first_half, pltpu.roll(t, D_B - HEAD_DIM // 2, 1),
                            pltpu.roll(t, HEAD_DIM // 2, 1))
        return t * cosf + partner * sins

    q = rope(proj[:, 2 * D_A:2 * D_A + D_B]) * (HEAD_DIM ** -0.5)
    k = rope(proj[:, 2 * D_A + D_B:2 * D_A + 2 * D_B])
    vv = proj[:, 2 * D_A + 2 * D_B:]
    q1_ref[...] = q.astype(BF16)
    k1_ref[...] = k.astype(BF16)
    v1_ref[...] = vv.astype(BF16)

    nslab = D_B // LANES
    for a, t in enumerate((q, k, vv)):
        for s in range(nslab):
            slab_ref[a * nslab + s] = t[:, s * LANES:(s + 1) * LANES]
    for d, outs in ((4, (q4_ref, k4_ref, v4_ref)), (16, (q16_ref, k16_ref, v16_ref))):
        rows = nt // d
        for a, o_ref in enumerate(outs):
            for r in range(d):
                for s in range(nslab):
                    o_ref[r, :, s * LANES:(s + 1) * LANES] = (
                        slab_ref[a * nslab + s, pl.ds(r, rows, stride=d), :].astype(BF16))


def _in_proj(x2, g1, win, lng, lnb, ws_cat, bs_full, ga, cosf, sins, batch, seq):
    t_total = x2.shape[0]
    nt = IN_BLOCK
    nb = seq // nt
    grid = (t_total // nt,)
    row = lambda i: (i, 0)
    const2 = lambda i: (0, 0)
    tok_bf = jax.ShapeDtypeStruct((t_total, D_B), BF16)
    out_shape = (
        jax.ShapeDtypeStruct((t_total, D_A), BF16),
        tok_bf, tok_bf, tok_bf,
        *(jax.ShapeDtypeStruct((batch, 4, seq // 4, D_B), BF16),) * 3,
        *(jax.ShapeDtypeStruct((batch, 16, seq // 16, D_B), BF16),) * 3,
    )
    res4 = pl.BlockSpec((None, 4, nt // 4, D_B), lambda i: (i // nb, 0, i % nb, 0))
    res16 = pl.BlockSpec((None, 16, nt // 16, D_B), lambda i: (i // nb, 0, i % nb, 0))
    tok_spec = pl.BlockSpec((nt, D_B), row)
    return pl.pallas_call(
        _in_proj_kernel,
        grid=grid,
        in_specs=[
            pl.BlockSpec((nt, D_MODEL), row),
            pl.BlockSpec((1, D_MODEL), const2),
            pl.BlockSpec((D_MODEL, D_IN), const2),
            pl.BlockSpec((1, D_A), const2),
            pl.BlockSpec((1, D_A), const2),
            pl.BlockSpec((A_GROUPS // 2, CHUNK, 2 * CHUNK), lambda i: (0, 0, 0)),
            pl.BlockSpec((CHUNK, D_A), const2),
            pl.BlockSpec((1, D_A), const2),
            pl.BlockSpec((nt, D_B), lambda i: (i % nb, 0)),
            pl.BlockSpec((nt, D_B), lambda i: (i % nb, 0)),
        ],
        out_specs=(pl.BlockSpec((nt, D_A), row), tok_spec, tok_spec, tok_spec,
                   res4, res4, res4, res16, res16, res16),
        out_shape=out_shape,
        scratch_shapes=[pltpu.VMEM((3 * D_B // LANES, nt, LANES), F32)],
        compiler_params=pltpu.CompilerParams(
            dimension_semantics=("arbitrary",), vmem_limit_bytes=VMEM_LIMIT),
        name="in_proj",
    )(x2, g1, win, lng, lnb, ws_cat, bs_full, ga, cosf, sins)


def _attn_kernel(q1_ref, k1_ref, v1_ref, q4_ref, k4_ref, v4_ref, q16_ref, k16_ref, v16_ref,
                 o_ref, out_ref, lse_ref, out16_ref, lse16_ref):
    seq = o_ref.shape[0]
    lane = lax.broadcasted_iota(I32, (QBLK, LANES), 1)
    head0 = lane < HEAD_DIM
    branches = ((1, q1_ref, k1_ref, v1_ref), (4, q4_ref, k4_ref, v4_ref),
                (16, q16_ref, k16_ref, v16_ref))
    for bi, (d, q_ref, k_ref, v_ref) in enumerate(branches):
        length = seq // d
        nblk = length // QBLK
        win = min(2 * QBLK, length)
        diff = (lax.broadcasted_iota(I32, (QBLK, win), 1)
                - lax.broadcasted_iota(I32, (QBLK, win), 0))

        def block(blk, carry, d=d, bi=bi, q_ref=q_ref, k_ref=k_ref, v_ref=v_ref,
                  length=length, nblk=nblk, win=win, diff=diff):
            r = blk // nblk
            i0 = pl.multiple_of((blk % nblk) * QBLK, QBLK)
            w0 = pl.multiple_of(jnp.clip(i0 - HALF_WINDOW, 0, length - win), HALF_WINDOW)
            qb = q_ref[r, pl.ds(i0, QBLK), :]
            kw = k_ref[r, pl.ds(w0, win), :]
            vw = v_ref[r, pl.ds(w0, win), :]
            rel = diff + (w0 - i0)
            valid = (rel >= -HALF_WINDOW) & (rel <= HALF_WINDOW)
            zero = jnp.zeros_like(qb)
            qq = jnp.concatenate([jnp.where(head0, qb, zero), jnp.where(head0, zero, qb)], axis=0)
            s = lax.dot_general(qq, kw, (((1,), (1,)), ((), ())), preferred_element_type=F32)
            s = jnp.where(jnp.concatenate([valid, valid], axis=0), s, NEG_BIG)
            m = jnp.max(s, axis=1, keepdims=True)
            p = jnp.exp(s - m)
            l = jnp.sum(p, axis=1, keepdims=True)
            pv = jnp.dot(p.astype(BF16), vw, preferred_element_type=F32) / l
            ml = m + jnp.log(l)
            out = jnp.where(head0, pv[:QBLK], pv[QBLK:])
            lse = jnp.where(head0, ml[:QBLK], ml[QBLK:])
            if d == 1:
                out_ref[0, pl.ds(i0, QBLK), :] = out
                lse_ref[0, pl.ds(i0, QBLK), :] = lse
            elif d == 4:
                rows = pl.ds(i0 * d + r, QBLK, stride=d)
                out_ref[1, rows, :] = out
                lse_ref[1, rows, :] = lse
            else:
                rows = pl.ds(r, QBLK, stride=RES16_PITCH)
                out16_ref[rows, :] = out
                lse16_ref[rows, :] = lse
            return carry

        lax.fori_loop(0, d * nblk, block, 0, unroll=ATTN_UNROLL)

    groups = QBLK // 16

    def merge(c, carry):
        rows = pl.ds(pl.multiple_of(c * QBLK, QBLK), QBLK)
        base = pl.multiple_of(c * (groups * RES16_PITCH), SUB)
        pieces = [pl.ds(base + g * RES16_PITCH, 16) for g in range(groups)]
        o3 = jnp.concatenate([out16_ref[pc, :] for pc in pieces], axis=0)
        e3 = jnp.concatenate([lse16_ref[pc, :] for pc in pieces], axis=0)
        e1, e2 = lse_ref[0, rows, :], lse_ref[1, rows, :]
        mx = jnp.maximum(jnp.maximum(e1, e2), e3)
        w1, w2, w3 = jnp.exp(e1 - mx), jnp.exp(e2 - mx), jnp.exp(e3 - mx)
        num = w1 * out_ref[0, rows, :] + w2 * out_ref[1, rows, :] + w3 * o3
        o_ref[rows, :] = num / (w1 + w2 + w3)
        return carry

    lax.fori_loop(0, seq // QBLK, merge, 0)


def _attention(q1, k1, v1, q4, k4, v4, q16, k16, v16, batch, seq):
    npair = D_B // LANES
    nat = pl.BlockSpec((None, 1, seq, LANES), lambda b, p: (b, 0, 0, p))
    r4 = pl.BlockSpec((None, 4, seq // 4, LANES), lambda b, p: (b, 0, 0, p))
    r16 = pl.BlockSpec((None, 16, seq // 16, LANES), lambda b, p: (b, 0, 0, p))
    q1, k1, v1 = (t.reshape(batch, 1, seq, D_B) for t in (q1, k1, v1))
    return pl.pallas_call(
        _attn_kernel,
        grid=(batch, npair),
        in_specs=[nat, nat, nat, r4, r4, r4, r16, r16, r16],
        out_specs=pl.BlockSpec((None, seq, LANES), lambda b, p: (b, 0, p)),
        out_shape=jax.ShapeDtypeStruct((batch, seq, D_B), F32),
        scratch_shapes=[pltpu.VMEM((2, seq, LANES), F32)] * 2
        + [pltpu.VMEM((seq // 16 * RES16_PITCH, LANES), F32)] * 2,
        compiler_params=pltpu.CompilerParams(
            dimension_semantics=("arbitrary", "arbitrary"), vmem_limit_bytes=VMEM_LIMIT),
        name="dilated_attn",
    )(q1, k1, v1, q4, k4, v4, q16, k16, v16)


def _topk_rows(s, k):
    n = s.shape[0]
    iota = lax.broadcasted_iota(I32, s.shape, 0).astype(F32)
    vals, idxs = [], []
    for _ in range(k):
        m = jnp.max(s, axis=0, keepdims=True)
        i = jnp.min(jnp.where(s == m, iota, float(n)), axis=0, keepdims=True)
        vals.append(m)
        idxs.append(i)
        s = jnp.where(iota == i, -jnp.inf, s)
    return jnp.concatenate(vals, axis=0), jnp.concatenate(idxs, axis=0).astype(I32)


def _take_rows(table, sel):
    out = jnp.zeros(sel.shape, table.dtype)
    for a in range(table.shape[0]):
        out = jnp.where(sel == a, table[a:a + 1, :], out)
    return out


def _mid_kernel(x_ref, an_ref, bo_ref, gb_ref, wout_ref, g2_ref, wq_ref, keys_ref,
                x1_ref, xn_ref, idx_ref, gate_ref):
    nt = x_ref.shape[0]
    bn = _rms(bo_ref[...], gb_ref[...]).astype(BF16)
    x1 = (x_ref[...]
          + jnp.dot(an_ref[...], wout_ref[:D_A, :], preferred_element_type=F32)
          + jnp.dot(bn, wout_ref[D_A:, :], preferred_element_type=F32))
    x1_ref[...] = x1
    xn = _rms(x1, g2_ref[...])
    xn_ref[...] = xn
    q = jnp.dot(xn.astype(BF16), wq_ref[...], preferred_element_type=F32).astype(BF16)
    keys = (keys_ref[0], keys_ref[1])
    half = D_KEY // 2
    for c in range(nt // LANES):
        qc = q[c * LANES:(c + 1) * LANES, :]
        experts, gates = [], []
        for h in range(PEER_HEADS):
            tops = []
            for p in range(2):
                qhp = qc[:, (2 * h + p) * half:(2 * h + p + 1) * half]
                s = lax.dot_general(keys[p], qhp, (((1,), (1,)), ((), ())),
                                    preferred_element_type=F32)
                tops.append(_topk_rows(s, PEER_TOPK))
            (s1, i1), (s2, i2) = tops
            cand = jnp.concatenate(
                [s1[0:1, :] + s2]
                + [s1[a:a + 1, :] + s2[0:SUB, :] for a in range(1, SUB)]
                + [s1[SUB:, :] + s2[0:1, :]], axis=0)
            sc, pos = _topk_rows(cand, PEER_TOPK)
            ca = jnp.where(pos < PEER_TOPK, 0,
                           jnp.where(pos < PEER_TOPK + SUB * (SUB - 1), (pos >> 3) - 1, pos - SUB * SUB))
            cb = jnp.where(pos < PEER_TOPK, pos,
                           jnp.where(pos < PEER_TOPK + SUB * (SUB - 1), pos & (SUB - 1), 0))
            e = _take_rows(i1, ca) * N_KEYS + _take_rows(i2, cb)
            ex = jnp.exp(sc - sc[0:1, :])
            gates.append(ex / jnp.sum(ex, axis=0, keepdims=True))
            experts.append(e)
        idx_ref[c] = jnp.concatenate(experts, axis=0).T
        gate_ref[c * LANES:(c + 1) * LANES, :] = jnp.concatenate(gates, axis=0).T


def _mid(x2, an, bo, gb, wout, g2, wq, keys):
    t_total = x2.shape[0]
    nt = MID_BLOCK
    row = lambda i: (i, 0)
    const2 = lambda i: (0, 0)
    return pl.pallas_call(
        _mid_kernel,
        grid=(t_total // nt,),
        in_specs=[
            pl.BlockSpec((nt, D_MODEL), row),
            pl.BlockSpec((nt, D_A), row),
            pl.BlockSpec((nt, D_B), row),
            pl.BlockSpec((1, D_B), const2),
            pl.BlockSpec((D_MODEL, D_MODEL), const2),
            pl.BlockSpec((1, D_MODEL), const2),
            pl.BlockSpec((D_MODEL, PEER_HEADS * D_KEY), const2),
            pl.BlockSpec((2, N_KEYS, D_KEY // 2), lambda i: (0, 0, 0)),
        ],
        out_specs=(
            pl.BlockSpec((nt, D_MODEL), row),
            pl.BlockSpec((nt, D_MODEL), row),
            pl.BlockSpec((nt // LANES, LANES, N_SLOTS), lambda i: (i, 0, 0)),
            pl.BlockSpec((nt, N_SLOTS), row),
        ),
        out_shape=(
            jax.ShapeDtypeStruct((t_total, D_MODEL), F32),
            jax.ShapeDtypeStruct((t_total, D_MODEL), F32),
            jax.ShapeDtypeStruct((t_total // LANES, LANES, N_SLOTS), I32),
            jax.ShapeDtypeStruct((t_total, N_SLOTS), F32),
        ),
        compiler_params=pltpu.CompilerParams(
            dimension_semantics=("arbitrary",), vmem_limit_bytes=VMEM_LIMIT),
        name="mid",
    )(x2, an, bo, gb, wout, g2, wq, keys)


def _peer_kernel(idx_ref, gate_ref, xn_ref, x1_ref, gf_ref, uv_ref, y_ref, *scratch):
    rows_refs = scratch[:PEER_RING]
    bf_ref, sem_ref = scratch[PEER_RING:]
    nt = xn_ref.shape[0]
    wide = 2 * LANES
    nw = 2 * D_MODEL // wide

    def issue(t, slot, lo=0, hi=N_SLOTS):
        for s in range(lo, hi):
            pltpu.make_async_copy(uv_ref.at[idx_ref[t, s]],
                                  rows_refs[slot].at[s // SUB, :, s % SUB, :],
                                  sem_ref.at[slot]).start(priority=s % 2)

    def wait(slot):
        pltpu.make_async_copy(uv_ref.at[pl.ds(0, N_SLOTS)],
                              rows_refs[slot].reshape(N_SLOTS, NCH, LANES), sem_ref.at[slot]).wait()

    def stage(slot, k):
        for j in range(NCH):
            w = rows_refs[slot][:, j, :, :].reshape(N_SLOTS, LANES)
            bf_ref[k, :, j * LANES:(j + 1) * LANES] = (
                pltpu.bitcast(w & U_HALF, F32).astype(BF16))
            bf_ref[k, :, D_MODEL + j * LANES:D_MODEL + (j + 1) * LANES] = (
                pltpu.bitcast(w << 16, F32).astype(BF16))

    def pair(t0, slots, prefetch):
        for k in range(2):
            wait(slots[k])
            stage(slots[k], k)
        nbatch = 2 * nw
        per = N_SLOTS // (nbatch // 2)
        batches = [(k, b * per, (b + 1) * per) for b in range(nbatch // 2) for k in range(2)]

        def next_batch():
            if prefetch and batches:
                k, lo, hi = batches.pop(0)
                issue(t0 + PEER_RING + k, slots[k], lo, hi)

        x8 = [jnp.broadcast_to(xn_ref[pl.ds(t0 + k, 1), :], (SUB, D_MODEL)).astype(BF16)
              for k in range(2)]
        act = [jnp.zeros((SUB, N_SLOTS), F32) for _ in range(2)]
        for j in range(nw // 2):
            for k in range(2):
                next_batch()
                act[k] = act[k] + lax.dot_general(
                    x8[k][:, j * wide:(j + 1) * wide], bf_ref[k, :, j * wide:(j + 1) * wide],
                    (((1,), (1,)), ((), ())), preferred_element_type=F32)
        w = [(_gelu(act[k]) * gate_ref[pl.ds(t0 + k, 1), :]).astype(BF16) for k in range(2)]
        outs = [[], []]
        for j in range(nw // 2):
            for k in range(2):
                next_batch()
                outs[k].append(jnp.dot(
                    w[k], bf_ref[k, :, D_MODEL + j * wide:D_MODEL + (j + 1) * wide],
                    preferred_element_type=F32)[0:1, :])
        for k in range(2):
            out = jnp.concatenate(outs[k], axis=1)
            y_ref[pl.ds(t0 + k, 1), :] = _rms(x1_ref[pl.ds(t0 + k, 1), :] + out, gf_ref[...])

    def group(g, prefetch):
        for p in range(PEER_RING // 2):
            pair(g * PEER_RING + 2 * p, (2 * p, 2 * p + 1), prefetch)

    for t in range(PEER_RING):
        issue(t, t)
    ngroup = nt // PEER_RING
    lax.fori_loop(0, ngroup - 1, lambda g, c: (group(g, True), c)[1], 0)
    group(ngroup - 1, False)


def _peer(idx, gates, xn, x1, gf, uv, first_token):
    t_total = xn.shape[0] - first_token
    nt = PEER_BLOCK
    b0 = first_token // nt
    row = lambda i: (i + b0, 0)
    return pl.pallas_call(
        _peer_kernel,
        grid=(t_total // nt,),
        in_specs=[
            pl.BlockSpec((None, nt, N_SLOTS), lambda i: (i + b0, 0, 0), memory_space=pltpu.SMEM),
            pl.BlockSpec((nt, N_SLOTS), row),
            pl.BlockSpec((nt, D_MODEL), row),
            pl.BlockSpec((nt, D_MODEL), row),
            pl.BlockSpec((1, D_MODEL), lambda i: (0, 0)),
            pl.BlockSpec(memory_space=pl.ANY),
        ],
        out_specs=pl.BlockSpec((nt, D_MODEL), lambda i: (i, 0)),
        out_shape=jax.ShapeDtypeStruct((t_total, D_MODEL), F32),
        scratch_shapes=[pltpu.VMEM((N_SLOTS // SUB, NCH, SUB, LANES), I32)] * PEER_RING + [
            pltpu.VMEM((2, N_SLOTS, 2 * D_MODEL), BF16),
            pltpu.SemaphoreType.DMA((PEER_RING,))],
        compiler_params=pltpu.CompilerParams(
            dimension_semantics=("arbitrary",), vmem_limit_bytes=VMEM_LIMIT),
        name="peer",
    )(idx, gates, xn, x1, gf, uv)


def _sc_peer(idx, xn, gates, uv2, n_tokens):
    info = plsc.get_sparse_core_info()
    nc, lanes_n = info.num_cores, info.num_lanes
    nw = nc * info.num_subcores
    per = n_tokens // nw
    assert n_tokens % nw == 0 and per % 2 == 0
    nchunk = N_SLOTS // SC_ROWS
    qv = D_MODEL // (SC_PASSES * lanes_n)
    c0 = math.sqrt(2.0 / math.pi)
    mesh = plsc.VectorSubcoreMesh(core_axis_name="c", subcore_axis_name="s")
    dma = pltpu.SemaphoreType.DMA

    @functools.partial(
        pl.kernel, mesh=mesh,
        out_type=jax.ShapeDtypeStruct((n_tokens, D_MODEL), F32),
        scratch_types=[
            [pltpu.VMEM((nchunk, SC_ROWS), I32)] * 2,
            [pltpu.VMEM((D_MODEL,), F32)] * 2,
            [pltpu.VMEM((N_SLOTS,), F32)] * 2,
            [pltpu.VMEM((D_MODEL,), F32)] * 2,
            [pltpu.VMEM((SC_ROWS, D_MODEL), I32)] * 2,
            pltpu.VMEM((SC_ROWS, lanes_n), F32),
            [dma] * 2, [dma] * 2, [dma] * 2, [dma] * 2, [dma] * 2,
        ],
        compiler_params=pltpu.CompilerParams(needs_layout_passes=False),
        name="peer_sc",
    )
    def sc_kernel(idx_hbm, xn_hbm, gate_hbm, uv_hbm, out_hbm, idx_v, x_v, g_v, out_v, rows_v,
                  part_v, row_sem, idx_sem, x_sem, g_sem, out_sem):
        wid = lax.axis_index("s") * nc + lax.axis_index("c")
        t0 = wid * per
        lane_ids = lax.iota(I32, lanes_n)
        zero = jnp.zeros((lanes_n,), F32)

        def vec(q, j):
            return pl.ds((q * qv + j) * lanes_n, lanes_n)

        def gather(p, c, b):
            return pltpu.make_async_copy(uv_hbm.at[idx_v[p].at[c]], rows_v[b], row_sem[b])

        def inputs(tok, p):
            return (pltpu.make_async_copy(idx_hbm.at[tok], idx_v[p], idx_sem[p]),
                    pltpu.make_async_copy(xn_hbm.at[tok], x_v[p], x_sem[p]),
                    pltpu.make_async_copy(gate_hbm.at[tok], g_v[p], g_sem[p]))

        def result(tok, p):
            return pltpu.make_async_copy(out_v[p], out_hbm.at[tok], out_sem[p])

        def chunk(p, c, b):
            rows = rows_v[b]

            def ustep(j, acc):
                xj = x_v[p][pl.ds(j * lanes_n, lanes_n)]
                return tuple(
                    acc[r] + plsc.bitcast(rows[r, pl.ds(j * lanes_n, lanes_n)] & U_HALF, F32) * xj
                    for r in range(SC_ROWS))

            acc = lax.fori_loop(0, D_MODEL // lanes_n, ustep, (zero,) * SC_ROWS)
            for r in range(SC_ROWS):
                part_v[r, :] = acc[r]
            cols = [plsc.load_gather(part_v, [lane_ids, jnp.full((lanes_n,), l, I32)])
                    for l in range(lanes_n)]
            while len(cols) > 1:
                cols = [cols[i] + cols[i + 1] for i in range(0, len(cols), 2)]
            act = cols[0]
            z = c0 * (act + 0.044715 * (act * act * act))
            tanh_z = 1.0 - 2.0 / (jnp.exp(2.0 * z) + 1.0)
            w = 0.5 * act * (1.0 + tanh_z) * g_v[p][pl.ds(c * SC_ROWS, SC_ROWS)]
            for q in range(SC_PASSES):
                o = tuple(out_v[p][vec(q, j)] for j in range(qv))

                def vrow(r, o, q=q):
                    wr = w.at[jnp.full((lanes_n,), r, I32)].get(mode="promise_in_bounds")
                    return tuple(o[j] + wr * plsc.bitcast(rows[r, vec(q, j)] << 16, F32)
                                 for j in range(qv))

                o = lax.fori_loop(0, SC_ROWS, vrow, o)
                for j in range(qv):
                    out_v[p][vec(q, j)] = o[j]

        def token(i, p):
            tok = t0 + i
            more = i + 1 < per

            @pl.when(more)
            def _():
                for cp in inputs(tok + 1, 1 - p):
                    cp.start()

            @pl.when(i >= 2)
            def _():
                result(tok - 2, p).wait()

            for j in range(D_MODEL // lanes_n):
                out_v[p][pl.ds(j * lanes_n, lanes_n)] = zero

            @pl.loop(0, nchunk, step=2)
            def _(c):
                gather(p, c + 1, 1).start()
                gather(p, c, 0).wait()
                chunk(p, c, 0)

                @pl.when(c + 2 < nchunk)
                def _():
                    gather(p, c + 2, 0).start()

                @pl.when(jnp.logical_and(c + 2 >= nchunk, more))
                def _():
                    for cp in inputs(tok + 1, 1 - p):
                        cp.wait()
                    gather(1 - p, 0, 0).start()

                gather(p, c + 1, 1).wait()
                chunk(p, c + 1, 1)

            result(tok, p).start()

        for cp in inputs(t0, 0):
            cp.start()
        for cp in inputs(t0, 0):
            cp.wait()
        gather(0, 0, 0).start()

        @pl.loop(0, per, step=2)
        def _(i):
            token(i, 0)
            token(i + 1, 1)

        result(t0 + per - 2, 0).wait()
        result(t0 + per - 1, 1).wait()

    return sc_kernel(idx.reshape(-1, nchunk, SC_ROWS), xn, gates, uv2)


def _residual_norm_kernel(x1_ref, o_ref, gf_ref, y_ref):
    y_ref[...] = _rms(x1_ref[...] + o_ref[...], gf_ref[...])


def _residual_norm(x1, out, gf):
    n = out.shape[0]
    nt = PEER_BLOCK
    row = lambda i: (i, 0)
    return pl.pallas_call(
        _residual_norm_kernel,
        grid=(n // nt,),
        in_specs=[pl.BlockSpec((nt, D_MODEL), row), pl.BlockSpec((nt, D_MODEL), row),
                  pl.BlockSpec((1, D_MODEL), lambda i: (0, 0))],
        out_specs=pl.BlockSpec((nt, D_MODEL), row),
        out_shape=jax.ShapeDtypeStruct((n, D_MODEL), F32),
        compiler_params=pltpu.CompilerParams(dimension_semantics=("arbitrary",)),
        name="residual_norm",
    )(x1, out, gf)


def _rope_tables(seq):
    pos = jnp.arange(seq, dtype=F32)
    inv = 1.0 / (ROPE_THETA ** (jnp.arange(0, HEAD_DIM, 2, dtype=F32) / HEAD_DIM))
    ang = pos[:, None] * inv[None, :]
    cos, sin = jnp.cos(ang), jnp.sin(ang)
    cosf = jnp.tile(jnp.concatenate([cos, cos], axis=1), (1, B_HEADS))
    sins = jnp.tile(jnp.concatenate([-sin, sin], axis=1), (1, B_HEADS))
    return cosf, sins


def kernel(x, norm1_g, w_in, ln_v_g, ln_v_b, w_spatial, b_spatial, out_norm_a_g, out_norm_b_g,
           w_out, norm2_g, w_query, sub_keys, expert_u, expert_v, final_norm_g):
    batch, seq, _ = x.shape
    assert w_in.shape[0] == 1 and seq % (16 * QBLK) == 0 and seq % IN_BLOCK == 0
    row = lambda g: g.reshape(1, -1).astype(F32)

    ws = w_spatial[0].astype(BF16)
    ws_cat = jnp.concatenate([ws[0::2], ws[1::2]], axis=2)
    bs_full = jnp.repeat(b_spatial[0].T, A_GROUP_DIM, axis=1)
    cosf, sins = _rope_tables(seq)
    win, wout, wq = w_in[0].astype(BF16), w_out[0].astype(BF16), w_query[0].astype(BF16)
    keys = sub_keys[0].astype(BF16)
    gf = row(final_norm_g)
    half = lambda t: lax.bitcast_convert_type(t.astype(BF16), jnp.uint16).astype(jnp.uint32)
    uv2 = lax.bitcast_convert_type((half(expert_u[0]) << 16) | half(expert_v[0]), I32)
    uv = uv2.reshape(-1, NCH, LANES)

    nchunk = PIPE_CHUNKS if batch % PIPE_CHUNKS == 0 else 1
    cb = batch // nchunk
    pieces = []
    for ci in range(nchunk):
        x2 = x[ci * cb:(ci + 1) * cb].reshape(cb * seq, D_MODEL)
        an, q1, k1, v1, q4, k4, v4, q16, k16, v16 = _in_proj(
            x2, row(norm1_g[0]), win, row(ln_v_g[0]), row(ln_v_b[0]),
            ws_cat, bs_full, row(out_norm_a_g[0]), cosf, sins, cb, seq)
        bo = _attention(q1, k1, v1, q4, k4, v4, q16, k16, v16, cb, seq).reshape(cb * seq, D_B)
        x1, xn, idx, gates = _mid(x2, an, bo, row(out_norm_b_g[0]), wout, row(norm2_g[0]), wq, keys)
        n_sc = (cb * seq) * SC_SHARE[0] // SC_SHARE[1]
        assert n_sc % PEER_BLOCK == 0
        out_sc = _sc_peer(idx.reshape(cb * seq, N_SLOTS), xn, gates, uv2, n_sc)
        y_tc = _peer(idx, gates, xn, x1, gf, uv, n_sc)
        pieces += [_residual_norm(x1, out_sc, gf), y_tc]
    return jnp.concatenate(pieces, axis=0).reshape(batch, seq, D_MODEL)
```

```python
import functools
import math

import jax
import jax.numpy as jnp
from jax import lax
from jax.experimental import pallas as pl
from jax.experimental.pallas import tpu as pltpu
from jax.experimental.pallas import tpu_sc as plsc

F32 = jnp.float32
BF16 = jnp.bfloat16
I32 = jnp.int32

D_MODEL = 1024
D_A = 512
D_B = 512
A_GROUPS = 8
A_GROUP_DIM = 64
CHUNK = 128
B_HEADS = 8
HEAD_DIM = 64
DILATIONS = (1, 4, 16)
HALF_WINDOW = 64
ROPE_THETA = 10000.0
D_IN = 2 * D_A + 3 * D_B
N_KEYS = 128
PEER_HEADS = 8
PEER_TOPK = 16
D_KEY = 256
N_SLOTS = PEER_HEADS * PEER_TOPK
EPS = 1e-6
NEG_BIG = -1e30

LANES = 128
SUB = 8
NCH = D_MODEL // LANES
U_HALF = -65536
QBLK = 128
ATTN_UNROLL = 8
RES16_PITCH = 24
IN_BLOCK = 512
MID_BLOCK = 256
PEER_BLOCK = 128
PEER_RING = 4
SC_SHARE = (11, 16)
PIPE_CHUNKS = 8
SC_ROWS = 16
SC_PASSES = 4
VMEM_LIMIT = 48 * 1024 * 1024


def _gelu(x):
    c = math.sqrt(2.0 / math.pi)
    return 0.5 * x * (1.0 + jnp.tanh(c * (x + 0.044715 * (x * x * x))))


def _rms(x, g):
    return x * lax.rsqrt(jnp.mean(x * x, axis=-1, keepdims=True) + EPS) * g


def _in_proj_kernel(x_ref, g1_ref, win_ref, lng_ref, lnb_ref, ws_ref, bs_ref, ga_ref,
                    cos_ref, sin_ref,
                    an_ref, q1_ref, k1_ref, v1_ref, q4_ref, k4_ref, v4_ref,
                    q16_ref, k16_ref, v16_ref, slab_ref):
    nt = x_ref.shape[0]
    h = _rms(x_ref[...], g1_ref[...]).astype(BF16)
    proj = jnp.dot(h, win_ref[...], preferred_element_type=F32)

    u = _gelu(proj[:, :D_A])
    v = _gelu(proj[:, D_A:2 * D_A])
    mu = jnp.mean(v, axis=-1, keepdims=True)
    vc = v - mu
    var = jnp.mean(vc * vc, axis=-1, keepdims=True)
    vln = (vc * lax.rsqrt(var + EPS) * lng_ref[...] + lnb_ref[...]).astype(BF16)
    lane = lax.broadcasted_iota(I32, (CHUNK, LANES), 1)
    lo = lane < A_GROUP_DIM
    zero = jnp.zeros((CHUNK, LANES), BF16)
    chunks = []
    for c in range(nt // CHUNK):
        cols = []
        for j in range(A_GROUPS // 2):
            vv = vln[c * CHUNK:(c + 1) * CHUNK, j * LANES:(j + 1) * LANES]
            rhs = jnp.concatenate([jnp.where(lo, vv, zero), jnp.where(lo, zero, vv)], axis=0)
            cols.append(jnp.dot(ws_ref[j], rhs, preferred_element_type=F32))
        chunks.append(jnp.concatenate(cols, axis=1) + bs_ref[...])
    mixed = jnp.concatenate(chunks, axis=0)
    an_ref[...] = _rms(u * mixed, ga_ref[...]).astype(BF16)

    cosf = cos_ref[...]
    sins = sin_ref[...]
    lane_b = lax.broadcasted_iota(I32, (nt, D_B), 1)
    first_half = (lane_b % HEAD_DIM) < (HEAD_DIM // 2)

    def rope(t):
        partner = jnp.where(first_half, pltpu.roll(t, D_B - HEAD_DIM // 2, 1),
                            pltpu.roll(t, HEAD_DIM // 2, 1))
        return t * cosf + partner * sins

    q = rope(proj[:, 2 * D_A:2 * D_A + D_B]) * (HEAD_DIM ** -0.5)
    k = rope(proj[:, 2 * D_A + D_B:2 * D_A + 2 * D_B])
    vv = proj[:, 2 * D_A + 2 * D_B:]
    q1_ref[...] = q.astype(BF16)
    k1_ref[...] = k.astype(BF16)
    v1_ref[...] = vv.astype(BF16)

    nslab = D_B // LANES
    for a, t in enumerate((q, k, vv)):
        for s in range(nslab):
            slab_ref[a * nslab + s] = t[:, s * LANES:(s + 1) * LANES]
    for d, outs in ((4, (q4_ref, k4_ref, v4_ref)), (16, (q16_ref, k16_ref, v16_ref))):
        rows = nt // d
        for a, o_ref in enumerate(outs):
            for r in range(d):
                for s in range(nslab):
                    o_ref[r, :, s * LANES:(s + 1) * LANES] = (
                        slab_ref[a * nslab + s, pl.ds(r, rows, stride=d), :].astype(BF16))


def _in_proj(x2, g1, win, lng, lnb, ws_cat, bs_full, ga, cosf, sins, batch, seq):
    t_total = x2.shape[0]
    nt = IN_BLOCK
    nb = seq // nt
    grid = (t_total // nt,)
    row = lambda i: (i, 0)
    const2 = lambda i: (0, 0)
    tok_bf = jax.ShapeDtypeStruct((t_total, D_B), BF16)
    out_shape = (
        jax.ShapeDtypeStruct((t_total, D_A), BF16),
        tok_bf, tok_bf, tok_bf,
        *(jax.ShapeDtypeStruct((batch, 4, seq // 4, D_B), BF16),) * 3,
        *(jax.ShapeDtypeStruct((batch, 16, seq // 16, D_B), BF16),) * 3,
    )
    res4 = pl.BlockSpec((None, 4, nt // 4, D_B), lambda i: (i // nb, 0, i % nb, 0))
    res16 = pl.BlockSpec((None, 16, nt // 16, D_B), lambda i: (i // nb, 0, i % nb, 0))
    tok_spec = pl.BlockSpec((nt, D_B), row)
    return pl.pallas_call(
        _in_proj_kernel,
        grid=grid,
        in_specs=[
            pl.BlockSpec((nt, D_MODEL), row),
            pl.BlockSpec((1, D_MODEL), const2),
            pl.BlockSpec((D_MODEL, D_IN), const2),
            pl.BlockSpec((1, D_A), const2),
            pl.BlockSpec((1, D_A), const2),
            pl.BlockSpec((A_GROUPS // 2, CHUNK, 2 * CHUNK), lambda i: (0, 0, 0)),
            pl.BlockSpec((CHUNK, D_A), const2),
            pl.BlockSpec((1, D_A), const2),
            pl.BlockSpec((nt, D_B), lambda i: (i % nb, 0)),
            pl.BlockSpec((nt, D_B), lambda i: (i % nb, 0)),
        ],
        out_specs=(pl.BlockSpec((nt, D_A), row), tok_spec, tok_spec, tok_spec,
                   res4, res4, res4, res16, res16, res16),
        out_shape=out_shape,
        scratch_shapes=[pltpu.VMEM((3 * D_B // LANES, nt, LANES), F32)],
        compiler_params=pltpu.CompilerParams(
            dimension_semantics=("arbitrary",), vmem_limit_bytes=VMEM_LIMIT),
        name="in_proj",
    )(x2, g1, win, lng, lnb, ws_cat, bs_full, ga, cosf, sins)


def _attn_kernel(q1_ref, k1_ref, v1_ref, q4_ref, k4_ref, v4_ref, q16_ref, k16_ref, v16_ref,
                 o_ref, out_ref, lse_ref, out16_ref, lse16_ref):
    seq = o_ref.shape[0]
    lane = lax.broadcasted_iota(I32, (QBLK, LANES), 1)
    head0 = lane < HEAD_DIM
    refs = ((q1_ref, k1_ref, v1_ref), (q4_ref, k4_ref, v4_ref), (q16_ref, k16_ref, v16_ref))
    for bi, (d, (q_ref, k_ref, v_ref)) in enumerate(zip(DILATIONS, refs)):
        length = seq // d
        nblk = length // QBLK
        win = min(2 * QBLK, length)
        diff = (lax.broadcasted_iota(I32, (QBLK, win), 1)
                - lax.broadcasted_iota(I32, (QBLK, win), 0))

        def block(blk, carry, d=d, bi=bi, q_ref=q_ref, k_ref=k_ref, v_ref=v_ref,
                  length=length, nblk=nblk, win=win, diff=diff):
            r = blk // nblk
            i0 = pl.multiple_of((blk % nblk) * QBLK, QBLK)
            w0 = pl.multiple_of(jnp.clip(i0 - HALF_WINDOW, 0, length - win), HALF_WINDOW)
            qb = q_ref[r, pl.ds(i0, QBLK), :]
            kw = k_ref[r, pl.ds(w0, win), :]
            vw = v_ref[r, pl.ds(w0, win), :]
            rel = diff + (w0 - i0)
            valid = (rel >= -HALF_WINDOW) & (rel <= HALF_WINDOW)
            zero = jnp.zeros_like(qb)
            qq = jnp.concatenate([jnp.where(head0, qb, zero), jnp.where(head0, zero, qb)], axis=0)
            s = lax.dot_general(qq, kw, (((1,), (1,)), ((), ())), preferred_element_type=F32)
            s = jnp.where(jnp.concatenate([valid, valid], axis=0), s, NEG_BIG)
            m = jnp.max(s, axis=1, keepdims=True)
            p = jnp.exp(s - m)
            l = jnp.sum(p, axis=1, keepdims=True)
            pv = jnp.dot(p.astype(BF16), vw, preferred_element_type=F32) / l
            ml = m + jnp.log(l)
            out = jnp.where(head0, pv[:QBLK], pv[QBLK:])
            lse = jnp.where(head0, ml[:QBLK], ml[QBLK:])
            if d == 1:
                out_ref[0, pl.ds(i0, QBLK), :] = out
                lse_ref[0, pl.ds(i0, QBLK), :] = lse
            elif d == 4:
                rows = pl.ds(i0 * d + r, QBLK, stride=d)
                out_ref[1, rows, :] = out
                lse_ref[1, rows, :] = lse
            else:
                rows = pl.ds(r, QBLK, stride=RES16_PITCH)
                out16_ref[rows, :] = out
                lse16_ref[rows, :] = lse
            return carry

        lax.fori_loop(0, d * nblk, block, 0, unroll=ATTN_UNROLL)

    groups = QBLK // 16

    def merge(c, carry):
        rows = pl.ds(pl.multiple_of(c * QBLK, QBLK), QBLK)
        base = pl.multiple_of(c * (groups * RES16_PITCH), SUB)
        pieces = [pl.ds(base + g * RES16_PITCH, 16) for g in range(groups)]
        o3 = jnp.concatenate([out16_ref[pc, :] for pc in pieces], axis=0)
        e3 = jnp.concatenate([lse16_ref[pc, :] for pc in pieces], axis=0)
        e1, e2 = lse_ref[0, rows, :], lse_ref[1, rows, :]
        mx = jnp.maximum(jnp.maximum(e1, e2), e3)
        w1, w2, w3 = jnp.exp(e1 - mx), jnp.exp(e2 - mx), jnp.exp(e3 - mx)
        num = w1 * out_ref[0, rows, :] + w2 * out_ref[1, rows, :] + w3 * o3
        o_ref[rows, :] = num / (w1 + w2 + w3)
        return carry

    lax.fori_loop(0, seq // QBLK, merge, 0)


def _attention(q1, k1, v1, q4, k4, v4, q16, k16, v16, batch, seq):
    npair = D_B // LANES
    nat = pl.BlockSpec((None, 1, seq, LANES), lambda b, p: (b, 0, 0, p))
    r4 = pl.BlockSpec((None, 4, seq // 4, LANES), lambda b, p: (b, 0, 0, p))
    r16 = pl.BlockSpec((None, 16, seq // 16, LANES), lambda b, p: (b, 0, 0, p))
    q1, k1, v1 = (t.reshape(batch, 1, seq, D_B) for t in (q1, k1, v1))
    return pl.pallas_call(
        _attn_kernel,
        grid=(batch, npair),
        in_specs=[nat, nat, nat, r4, r4, r4, r16, r16, r16],
        out_specs=pl.BlockSpec((None, seq, LANES), lambda b, p: (b, 0, p)),
        out_shape=jax.ShapeDtypeStruct((batch, seq, D_B), F32),
        scratch_shapes=[pltpu.VMEM((2, seq, LANES), F32)] * 2
        + [pltpu.VMEM((seq // 16 * RES16_PITCH, LANES), F32)] * 2,
        compiler_params=pltpu.CompilerParams(
            dimension_semantics=("arbitrary", "arbitrary"), vmem_limit_bytes=VMEM_LIMIT),
        name="dilated_attn",
    )(q1, k1, v1, q4, k4, v4, q16, k16, v16)


def _topk_rows(s, k):
    n = s.shape[0]
    iota = lax.broadcasted_iota(I32, s.shape, 0).astype(F32)
    vals, idxs = [], []
    for _ in range(k):
        m = jnp.max(s, axis=0, keepdims=True)
        i = jnp.min(jnp.where(s == m, iota, float(n)), axis=0, keepdims=True)
        vals.append(m)
        idxs.append(i)
        s = jnp.where(iota == i, -jnp.inf, s)
    return jnp.concatenate(vals, axis=0), jnp.concatenate(idxs, axis=0).astype(I32)


def _take_rows(table, sel):
    out = jnp.zeros(sel.shape, table.dtype)
    for a in range(table.shape[0]):
        out = jnp.where(sel == a, table[a:a + 1, :], out)
    return out


def _mid_kernel(x_ref, an_ref, bo_ref, gb_ref, wout_ref, g2_ref, wq_ref, keys_ref,
                x1_ref, xn_ref, idx_ref, gate_ref):
    nt = x_ref.shape[0]
    bn = _rms(bo_ref[...], gb_ref[...]).astype(BF16)
    x1 = (x_ref[...]
          + jnp.dot(an_ref[...], wout_ref[:D_A, :], preferred_element_type=F32)
          + jnp.dot(bn, wout_ref[D_A:, :], preferred_element_type=F32))
    x1_ref[...] = x1
    xn = _rms(x1, g2_ref[...])
    xn_ref[...] = xn
    q = jnp.dot(xn.astype(BF16), wq_ref[...], preferred_element_type=F32).astype(BF16)
    keys = (keys_ref[0], keys_ref[1])
    half = D_KEY // 2
    for c in range(nt // LANES):
        qc = q[c * LANES:(c + 1) * LANES, :]
        experts, gates = [], []
        for h in range(PEER_HEADS):
            tops = []
            for p in range(2):
                qhp = qc[:, (2 * h + p) * half:(2 * h + p + 1) * half]
                s = lax.dot_general(keys[p], qhp, (((1,), (1,)), ((), ())),
                                    preferred_element_type=F32)
                tops.append(_topk_rows(s, PEER_TOPK))
            (s1, i1), (s2, i2) = tops
            cand = jnp.concatenate(
                [s1[0:1, :] + s2]
                + [s1[a:a + 1, :] + s2[0:SUB, :] for a in range(1, SUB)]
                + [s1[SUB:, :] + s2[0:1, :]], axis=0)
            sc, pos = _topk_rows(cand, PEER_TOPK)
            ca = jnp.where(pos < PEER_TOPK, 0,
                           jnp.where(pos < PEER_TOPK + SUB * (SUB - 1), (pos >> 3) - 1, pos - SUB * SUB))
            cb = jnp.where(pos < PEER_TOPK, pos,
                           jnp.where(pos < PEER_TOPK + SUB * (SUB - 1), pos & (SUB - 1), 0))
            e = _take_rows(i1, ca) * N_KEYS + _take_rows(i2, cb)
            ex = jnp.exp(sc - sc[0:1, :])
            gates.append(ex / jnp.sum(ex, axis=0, keepdims=True))
            experts.append(e)
        idx_ref[c] = jnp.concatenate(experts, axis=0).T
        gate_ref[c * LANES:(c + 1) * LANES, :] = jnp.concatenate(gates, axis=0).T


def _mid(x2, an, bo, gb, wout, g2, wq, keys):
    t_total = x2.shape[0]
    nt = MID_BLOCK
    row = lambda i: (i, 0)
    const2 = lambda i: (0, 0)
    return pl.pallas_call(
        _mid_kernel,
        grid=(t_total // nt,),
        in_specs=[
            pl.BlockSpec((nt, D_MODEL), row),
            pl.BlockSpec((nt, D_A), row),
            pl.BlockSpec((nt, D_B), row),
            pl.BlockSpec((1, D_B), const2),
            pl.BlockSpec((D_MODEL, D_MODEL), const2),
            pl.BlockSpec((1, D_MODEL), const2),
            pl.BlockSpec((D_MODEL, PEER_HEADS * D_KEY), const2),
            pl.BlockSpec((2, N_KEYS, D_KEY // 2), lambda i: (0, 0, 0)),
        ],
        out_specs=(
            pl.BlockSpec((nt, D_MODEL), row),
            pl.BlockSpec((nt, D_MODEL), row),
            pl.BlockSpec((nt // LANES, LANES, N_SLOTS), lambda i: (i, 0, 0)),
            pl.BlockSpec((nt, N_SLOTS), row),
        ),
        out_shape=(
            jax.ShapeDtypeStruct((t_total, D_MODEL), F32),
            jax.ShapeDtypeStruct((t_total, D_MODEL), F32),
            jax.ShapeDtypeStruct((t_total // LANES, LANES, N_SLOTS), I32),
            jax.ShapeDtypeStruct((t_total, N_SLOTS), F32),
        ),
        compiler_params=pltpu.CompilerParams(
            dimension_semantics=("arbitrary",), vmem_limit_bytes=VMEM_LIMIT),
        name="mid",
    )(x2, an, bo, gb, wout, g2, wq, keys)


def _peer_kernel(idx_ref, gate_ref, xn_ref, x1_ref, gf_ref, uv_ref, y_ref, *scratch):
    rows_refs = scratch[:PEER_RING]
    bf_ref, sem_ref = scratch[PEER_RING:]
    nt = xn_ref.shape[0]
    wide = 2 * LANES
    nw = 2 * D_MODEL // wide

    def issue(t, slot, lo=0, hi=N_SLOTS):
        for s in range(lo, hi):
            pltpu.make_async_copy(uv_ref.at[idx_ref[t, s]],
                                  rows_refs[slot].at[s // SUB, :, s % SUB, :],
                                  sem_ref.at[slot]).start(priority=s % 2)

    def wait(slot):
        pltpu.make_async_copy(uv_ref.at[pl.ds(0, N_SLOTS)],
                              rows_refs[slot].reshape(N_SLOTS, NCH, LANES), sem_ref.at[slot]).wait()

    def stage(slot, k):
        for j in range(NCH):
            w = rows_refs[slot][:, j, :, :].reshape(N_SLOTS, LANES)
            bf_ref[k, :, j * LANES:(j + 1) * LANES] = (
                pltpu.bitcast(w & U_HALF, F32).astype(BF16))
            bf_ref[k, :, D_MODEL + j * LANES:D_MODEL + (j + 1) * LANES] = (
                pltpu.bitcast(w << 16, F32).astype(BF16))

    def pair(t0, slots, prefetch):
        for k in range(2):
            wait(slots[k])
            stage(slots[k], k)
        nbatch = 2 * nw
        per = N_SLOTS // (nbatch // 2)
        batches = [(k, b * per, (b + 1) * per) for b in range(nbatch // 2) for k in range(2)]

        def next_batch():
            if prefetch and batches:
                k, lo, hi = batches.pop(0)
                issue(t0 + PEER_RING + k, slots[k], lo, hi)

        x8 = [jnp.broadcast_to(xn_ref[pl.ds(t0 + k, 1), :], (SUB, D_MODEL)).astype(BF16)
              for k in range(2)]
        act = [jnp.zeros((SUB, N_SLOTS), F32) for _ in range(2)]
        for j in range(nw // 2):
            for k in range(2):
                next_batch()
                act[k] = act[k] + lax.dot_general(
                    x8[k][:, j * wide:(j + 1) * wide], bf_ref[k, :, j * wide:(j + 1) * wide],
                    (((1,), (1,)), ((), ())), preferred_element_type=F32)
        w = [(_gelu(act[k]) * gate_ref[pl.ds(t0 + k, 1), :]).astype(BF16) for k in range(2)]
        outs = [[], []]
        for j in range(nw // 2):
            for k in range(2):
                next_batch()
                outs[k].append(jnp.dot(
                    w[k], bf_ref[k, :, D_MODEL + j * wide:D_MODEL + (j + 1) * wide],
                    preferred_element_type=F32)[0:1, :])
        for k in range(2):
            out = jnp.concatenate(outs[k], axis=1)
            y_ref[pl.ds(t0 + k, 1), :] = _rms(x1_ref[pl.ds(t0 + k, 1), :] + out, gf_ref[...])

    def group(g, prefetch):
        for p in range(PEER_RING // 2):
            pair(g * PEER_RING + 2 * p, (2 * p, 2 * p + 1), prefetch)

    for t in range(PEER_RING):
        issue(t, t)
    ngroup = nt // PEER_RING
    lax.fori_loop(0, ngroup - 1, lambda g, c: (group(g, True), c)[1], 0)
    group(ngroup - 1, False)


def _peer(idx, gates, xn, x1, gf, uv, first_token):
    t_total = xn.shape[0] - first_token
    nt = PEER_BLOCK
    b0 = first_token // nt
    row = lambda i: (i + b0, 0)
    return pl.pallas_call(
        _peer_kernel,
        grid=(t_total // nt,),
        in_specs=[
            pl.BlockSpec((None, nt, N_SLOTS), lambda i: (i + b0, 0, 0), memory_space=pltpu.SMEM),
            pl.BlockSpec((nt, N_SLOTS), row),
            pl.BlockSpec((nt, D_MODEL), row),
            pl.BlockSpec((nt, D_MODEL), row),
            pl.BlockSpec((1, D_MODEL), lambda i: (0, 0)),
            pl.BlockSpec(memory_space=pl.ANY),
        ],
        out_specs=pl.BlockSpec((nt, D_MODEL), lambda i: (i, 0)),
        out_shape=jax.ShapeDtypeStruct((t_total, D_MODEL), F32),
        scratch_shapes=[pltpu.VMEM((N_SLOTS // SUB, NCH, SUB, LANES), I32)] * PEER_RING + [
            pltpu.VMEM((2, N_SLOTS, 2 * D_MODEL), BF16),
            pltpu.SemaphoreType.DMA((PEER_RING,))],
        compiler_params=pltpu.CompilerParams(
            dimension_semantics=("arbitrary",), vmem_limit_bytes=VMEM_LIMIT),
        name="peer",
    )(idx, gates, xn, x1, gf, uv)


def _sc_peer(idx, xn, gates, uv2, n_tokens):
    info = plsc.get_sparse_core_info()
    nc, lanes_n = info.num_cores, info.num_lanes
    nw = nc * info.num_subcores
    per = n_tokens // nw
    assert n_tokens % nw == 0 and per % 2 == 0
    nchunk = N_SLOTS // SC_ROWS
    qv = D_MODEL // (SC_PASSES * lanes_n)
    c0 = math.sqrt(2.0 / math.pi)
    mesh = plsc.VectorSubcoreMesh(core_axis_name="c", subcore_axis_name="s")
    dma = pltpu.SemaphoreType.DMA

    @functools.partial(
        pl.kernel, mesh=mesh,
        out_type=jax.ShapeDtypeStruct((n_tokens, D_MODEL), F32),
        scratch_types=[
            [pltpu.VMEM((nchunk, SC_ROWS), I32)] * 2,
            [pltpu.VMEM((D_MODEL,), F32)] * 2,
            [pltpu.VMEM((N_SLOTS,), F32)] * 2,
            [pltpu.VMEM((D_MODEL,), F32)] * 2,
            [pltpu.VMEM((SC_ROWS, D_MODEL), I32)] * 2,
            [dma] * 2, [dma] * 2, [dma] * 2, [dma] * 2, [dma] * 2,
        ],
        compiler_params=pltpu.CompilerParams(needs_layout_passes=False),
        name="peer_sc",
    )
    def sc_kernel(idx_hbm, xn_hbm, gate_hbm, uv_hbm, out_hbm, idx_v, x_v, g_v, out_v, rows_v,
                  row_sem, idx_sem, x_sem, g_sem, out_sem):
        wid = lax.axis_index("s") * nc + lax.axis_index("c")
        t0 = wid * per
        lane_ids = lax.iota(I32, lanes_n)
        zero = jnp.zeros((lanes_n,), F32)

        def vec(q, j):
            return pl.ds((q * qv + j) * lanes_n, lanes_n)

        def gather(p, c, b):
            return pltpu.make_async_copy(uv_hbm.at[idx_v[p].at[c]], rows_v[b], row_sem[b])

        def inputs(tok, p):
            return (pltpu.make_async_copy(idx_hbm.at[tok], idx_v[p], idx_sem[p]),
                    pltpu.make_async_copy(xn_hbm.at[tok], x_v[p], x_sem[p]),
                    pltpu.make_async_copy(gate_hbm.at[tok], g_v[p], g_sem[p]))

        def result(tok, p):
            return pltpu.make_async_copy(out_v[p], out_hbm.at[tok], out_sem[p])

        def chunk(p, c, b):
            rows = rows_v[b]

            def ustep(j, acc):
                xj = x_v[p][pl.ds(j * lanes_n, lanes_n)]
                return tuple(
                    acc[r] + plsc.bitcast(rows[r, pl.ds(j * lanes_n, lanes_n)] & U_HALF, F32) * xj
                    for r in range(SC_ROWS))

            acc = lax.fori_loop(0, D_MODEL // lanes_n, ustep, (zero,) * SC_ROWS)
            act = zero
            for r in range(SC_ROWS):
                act = jnp.where(lane_ids == r, jnp.sum(acc[r]), act)
            z = c0 * (act + 0.044715 * (act * act * act))
            tanh_z = 1.0 - 2.0 / (jnp.exp(2.0 * z) + 1.0)
            w = 0.5 * act * (1.0 + tanh_z) * g_v[p][pl.ds(c * SC_ROWS, SC_ROWS)]
            for q in range(SC_PASSES):
                o = tuple(out_v[p][vec(q, j)] for j in range(qv))

                def vrow(r, o, q=q):
                    wr = w.at[jnp.full((lanes_n,), r, I32)].get(mode="promise_in_bounds")
                    return tuple(o[j] + wr * plsc.bitcast(rows[r, vec(q, j)] << 16, F32)
                                 for j in range(qv))

                o = lax.fori_loop(0, SC_ROWS, vrow, o)
                for j in range(qv):
                    out_v[p][vec(q, j)] = o[j]

        def token(i, p):
            tok = t0 + i
            more = i + 1 < per

            @pl.when(more)
            def _():
                for cp in inputs(tok + 1, 1 - p):
                    cp.start()

            @pl.when(i >= 2)
            def _():
                result(tok - 2, p).wait()

            for j in range(D_MODEL // lanes_n):
                out_v[p][pl.ds(j * lanes_n, lanes_n)] = zero

            @pl.loop(0, nchunk, step=2)
            def _(c):
                gather(p, c + 1, 1).start()
                gather(p, c, 0).wait()
                chunk(p, c, 0)

                @pl.when(c + 2 < nchunk)
                def _():
                    gather(p, c + 2, 0).start()

                @pl.when(jnp.logical_and(c + 2 >= nchunk, more))
                def _():
                    for cp in inputs(tok + 1, 1 - p):
                        cp.wait()
                    gather(1 - p, 0, 0).start()

                gather(p, c + 1, 1).wait()
                chunk(p, c + 1, 1)

            result(tok, p).start()

        for cp in inputs(t0, 0):
            cp.start()
        for cp in inputs(t0, 0):
            cp.wait()
        gather(0, 0, 0).start()

        @pl.loop(0, per, step=2)
        def _(i):
            token(i, 0)
            token(i + 1, 1)

        result(t0 + per - 2, 0).wait()
        result(t0 + per - 1, 1).wait()

    return sc_kernel(idx.reshape(-1, nchunk, SC_ROWS), xn, gates, uv2)


def _residual_norm_kernel(x1_ref, o_ref, gf_ref, y_ref):
    y_ref[...] = _rms(x1_ref[...] + o_ref[...], gf_ref[...])


def _residual_norm(x1, out, gf):
    n = out.shape[0]
    nt = PEER_BLOCK
    row = lambda i: (i, 0)
    return pl.pallas_call(
        _residual_norm_kernel,
        grid=(n // nt,),
        in_specs=[pl.BlockSpec((nt, D_MODEL), row), pl.BlockSpec((nt, D_MODEL), row),
                  pl.BlockSpec((1, D_MODEL), lambda i: (0, 0))],
        out_specs=pl.BlockSpec((nt, D_MODEL), row),
        out_shape=jax.ShapeDtypeStruct((n, D_MODEL), F32),
        compiler_params=pltpu.CompilerParams(dimension_semantics=("arbitrary",)),
        name="residual_norm",
    )(x1, out, gf)


def _rope_tables(seq):
    pos = jnp.arange(seq, dtype=F32)
    inv = 1.0 / (ROPE_THETA ** (jnp.arange(0, HEAD_DIM, 2, dtype=F32) / HEAD_DIM))
    ang = pos[:, None] * inv[None, :]
    cos, sin = jnp.cos(ang), jnp.sin(ang)
    cosf = jnp.tile(jnp.concatenate([cos, cos], axis=1), (1, B_HEADS))
    sins = jnp.tile(jnp.concatenate([-sin, sin], axis=1), (1, B_HEADS))
    return cosf, sins


def kernel(x, norm1_g, w_in, ln_v_g, ln_v_b, w_spatial, b_spatial, out_norm_a_g, out_norm_b_g,
           w_out, norm2_g, w_query, sub_keys, expert_u, expert_v, final_norm_g):
    batch, seq, _ = x.shape
    assert w_in.shape[0] == 1 and seq % (16 * QBLK) == 0 and seq % IN_BLOCK == 0
    row = lambda g: g.reshape(1, -1).astype(F32)

    ws = w_spatial[0].astype(BF16)
    ws_cat = jnp.concatenate([ws[0::2], ws[1::2]], axis=2)
    bs_full = jnp.repeat(b_spatial[0].T, A_GROUP_DIM, axis=1)
    cosf, sins = _rope_tables(seq)
    win, wout, wq = w_in[0].astype(BF16), w_out[0].astype(BF16), w_query[0].astype(BF16)
    keys = sub_keys[0].astype(BF16)
    gf = row(final_norm_g)
    half = lambda t: lax.bitcast_convert_type(t.astype(BF16), jnp.uint16).astype(jnp.uint32)
    uv2 = lax.bitcast_convert_type((half(expert_u[0]) << 16) | half(expert_v[0]), I32)
    uv = uv2.reshape(-1, NCH, LANES)

    nchunk = PIPE_CHUNKS if batch % PIPE_CHUNKS == 0 else 1
    cb = batch // nchunk
    pieces = []
    for ci in range(nchunk):
        x2 = x[ci * cb:(ci + 1) * cb].reshape(cb * seq, D_MODEL)
        an, q1, k1, v1, q4, k4, v4, q16, k16, v16 = _in_proj(
            x2, row(norm1_g[0]), win, row(ln_v_g[0]), row(ln_v_b[0]),
            ws_cat, bs_full, row(out_norm_a_g[0]), cosf, sins, cb, seq)
        bo = _attention(q1, k1, v1, q4, k4, v4, q16, k16, v16, cb, seq).reshape(cb * seq, D_B)
        x1, xn, idx, gates = _mid(x2, an, bo, row(out_norm_b_g[0]), wout, row(norm2_g[0]), wq, keys)
        n_sc = (cb * seq) * SC_SHARE[0] // SC_SHARE[1]
        assert n_sc % PEER_BLOCK == 0
        out_sc = _sc_peer(idx.reshape(cb * seq, N_SLOTS), xn, gates, uv2, n_sc)
        y_tc = _peer(idx, gates, xn, x1, gf, uv, n_sc)
        pieces += [_residual_norm(x1, out_sc, gf), y_tc]
    return jnp.concatenate(pieces, axis=0).reshape(batch, seq, D_MODEL)
```

```python
import functools
import math

import jax
import jax.numpy as jnp
from jax import lax
from jax.experimental import pallas as pl
from jax.experimental.pallas import tpu as pltpu
from jax.experimental.pallas import tpu_sc as plsc

F32 = jnp.float32
BF16 = jnp.bfloat16
I32 = jnp.int32

D_MODEL = 1024
D_A = 512
D_B = 512
A_GROUPS = 8
A_GROUP_DIM = 64
CHUNK = 128
B_HEADS = 8
HEAD_DIM = 64
DILATIONS = (1, 4, 16)
HALF_WINDOW = 64
ROPE_THETA = 10000.0
D_IN = 2 * D_A + 3 * D_B
N_KEYS = 128
PEER_HEADS = 8
PEER_TOPK = 16
D_KEY = 256
N_SLOTS = PEER_HEADS * PEER_TOPK
EPS = 1e-6
NEG_BIG = -1e30

LANES = 128
SUB = 8
NCH = D_MODEL // LANES
U_HALF = -65536
QBLK = 128
ATTN_UNROLL = 8
RES16_PITCH = 24
IN_BLOCK = 512
MID_BLOCK = 256
PEER_BLOCK = 128
PEER_RING = 4
SC_SHARE = (45, 64)
PIPE_CHUNKS = 8
SC_ROWS = 16
SC_PASSES = 4
VMEM_LIMIT = 48 * 1024 * 1024


def _gelu(x):
    c = math.sqrt(2.0 / math.pi)
    return 0.5 * x * (1.0 + jnp.tanh(c * (x + 0.044715 * (x * x * x))))


def _rms(x, g):
    return x * lax.rsqrt(jnp.mean(x * x, axis=-1, keepdims=True) + EPS) * g


def _in_proj_kernel(x_ref, g1_ref, win_ref, lng_ref, lnb_ref, ws_ref, bs_ref, ga_ref,
                    cos_ref, sin_ref,
                    an_ref, q1_ref, k1_ref, v1_ref, q4_ref, k4_ref, v4_ref,
                    q16_ref, k16_ref, v16_ref, slab_ref):
    nt = x_ref.shape[0]
    h = _rms(x_ref[...], g1_ref[...]).astype(BF16)
    proj = jnp.dot(h, win_ref[...], preferred_element_type=F32)

    u = _gelu(proj[:, :D_A])
    v = _gelu(proj[:, D_A:2 * D_A])
    mu = jnp.mean(v, axis=-1, keepdims=True)
    vc = v - mu
    var = jnp.mean(vc * vc, axis=-1, keepdims=True)
    vln = (vc * lax.rsqrt(var + EPS) * lng_ref[...] + lnb_ref[...]).astype(BF16)
    lane = lax.broadcasted_iota(I32, (CHUNK, LANES), 1)
    lo = lane < A_GROUP_DIM
    zero = jnp.zeros((CHUNK, LANES), BF16)
    chunks = []
    for c in range(nt // CHUNK):
        cols = []
        for j in range(A_GROUPS // 2):
            vv = vln[c * CHUNK:(c + 1) * CHUNK, j * LANES:(j + 1) * LANES]
            rhs = jnp.concatenate([jnp.where(lo, vv, zero), jnp.where(lo, zero, vv)], axis=0)
            cols.append(jnp.dot(ws_ref[j], rhs, preferred_element_type=F32))
        chunks.append(jnp.concatenate(cols, axis=1) + bs_ref[...])
    mixed = jnp.concatenate(chunks, axis=0)
    an_ref[...] = _rms(u * mixed, ga_ref[...]).astype(BF16)

    cosf = cos_ref[...]
    sins = sin_ref[...]
    lane_b = lax.broadcasted_iota(I32, (nt, D_B), 1)
    first_half = (lane_b % HEAD_DIM) < (HEAD_DIM // 2)

    def rope(t):
        partner = jnp.where(first_half, pltpu.roll(t, D_B - HEAD_DIM // 2, 1),
                            pltpu.roll(t, HEAD_DIM // 2, 1))
        return t * cosf + partner * sins

    q = rope(proj[:, 2 * D_A:2 * D_A + D_B]) * (HEAD_DIM ** -0.5)
    k = rope(proj[:, 2 * D_A + D_B:2 * D_A + 2 * D_B])
    vv = proj[:, 2 * D_A + 2 * D_B:]
    q1_ref[...] = q.astype(BF16)
    k1_ref[...] = k.astype(BF16)
    v1_ref[...] = vv.astype(BF16)

    nslab = D_B // LANES
    for a, t in enumerate((q, k, vv)):
        for s in range(nslab):
            slab_ref[a * nslab + s] = t[:, s * LANES:(s + 1) * LANES]
    for d, outs in ((4, (q4_ref, k4_ref, v4_ref)), (16, (q16_ref, k16_ref, v16_ref))):
        rows = nt // d
        for a, o_ref in enumerate(outs):
            for r in range(d):
                for s in range(nslab):
                    o_ref[r, :, s * LANES:(s + 1) * LANES] = (
                        slab_ref[a * nslab + s, pl.ds(r, rows, stride=d), :].astype(BF16))


def _in_proj(x2, g1, win, lng, lnb, ws_cat, bs_full, ga, cosf, sins, batch, seq):
    t_total = x2.shape[0]
    nt = IN_BLOCK
    nb = seq // nt
    grid = (t_total // nt,)
    row = lambda i: (i, 0)
    const2 = lambda i: (0, 0)
    tok_bf = jax.ShapeDtypeStruct((t_total, D_B), BF16)
    out_shape = (
        jax.ShapeDtypeStruct((t_total, D_A), BF16),
        tok_bf, tok_bf, tok_bf,
        *(jax.ShapeDtypeStruct((batch, 4, seq // 4, D_B), BF16),) * 3,
        *(jax.ShapeDtypeStruct((batch, 16, seq // 16, D_B), BF16),) * 3,
    )
    res4 = pl.BlockSpec((None, 4, nt // 4, D_B), lambda i: (i // nb, 0, i % nb, 0))
    res16 = pl.BlockSpec((None, 16, nt // 16, D_B), lambda i: (i // nb, 0, i % nb, 0))
    tok_spec = pl.BlockSpec((nt, D_B), row)
    return pl.pallas_call(
        _in_proj_kernel,
        grid=grid,
        in_specs=[
            pl.BlockSpec((nt, D_MODEL), row),
            pl.BlockSpec((1, D_MODEL), const2),
            pl.BlockSpec((D_MODEL, D_IN), const2),
            pl.BlockSpec((1, D_A), const2),
            pl.BlockSpec((1, D_A), const2),
            pl.BlockSpec((A_GROUPS // 2, CHUNK, 2 * CHUNK), lambda i: (0, 0, 0)),
            pl.BlockSpec((CHUNK, D_A), const2),
            pl.BlockSpec((1, D_A), const2),
            pl.BlockSpec((nt, D_B), lambda i: (i % nb, 0)),
            pl.BlockSpec((nt, D_B), lambda i: (i % nb, 0)),
        ],
        out_specs=(pl.BlockSpec((nt, D_A), row), tok_spec, tok_spec, tok_spec,
                   res4, res4, res4, res16, res16, res16),
        out_shape=out_shape,
        scratch_shapes=[pltpu.VMEM((3 * D_B // LANES, nt, LANES), F32)],
        compiler_params=pltpu.CompilerParams(
            dimension_semantics=("arbitrary",), vmem_limit_bytes=VMEM_LIMIT),
        name="in_proj",
    )(x2, g1, win, lng, lnb, ws_cat, bs_full, ga, cosf, sins)


def _attn_kernel(q1_ref, k1_ref, v1_ref, q4_ref, k4_ref, v4_ref, q16_ref, k16_ref, v16_ref,
                 o_ref, out_ref, lse_ref, out16_ref, lse16_ref):
    seq = o_ref.shape[0]
    lane = lax.broadcasted_iota(I32, (QBLK, LANES), 1)
    head0 = lane < HEAD_DIM
    refs = ((q1_ref, k1_ref, v1_ref), (q4_ref, k4_ref, v4_ref), (q16_ref, k16_ref, v16_ref))
    for bi, (d, (q_ref, k_ref, v_ref)) in enumerate(zip(DILATIONS, refs)):
        length = seq // d
        nblk = length // QBLK
        win = min(2 * QBLK, length)
        diff = (lax.broadcasted_iota(I32, (QBLK, win), 1)
                - lax.broadcasted_iota(I32, (QBLK, win), 0))

        def block(blk, carry, d=d, bi=bi, q_ref=q_ref, k_ref=k_ref, v_ref=v_ref,
                  length=length, nblk=nblk, win=win, diff=diff):
            r = blk // nblk
            i0 = pl.multiple_of((blk % nblk) * QBLK, QBLK)
            w0 = pl.multiple_of(jnp.clip(i0 - HALF_WINDOW, 0, length - win), HALF_WINDOW)
            qb = q_ref[r, pl.ds(i0, QBLK), :]
            kw = k_ref[r, pl.ds(w0, win), :]
            vw = v_ref[r, pl.ds(w0, win), :]
            rel = diff + (w0 - i0)
            valid = (rel >= -HALF_WINDOW) & (rel <= HALF_WINDOW)
            zero = jnp.zeros_like(qb)
            qq = jnp.concatenate([jnp.where(head0, qb, zero), jnp.where(head0, zero, qb)], axis=0)
            s = lax.dot_general(qq, kw, (((1,), (1,)), ((), ())), preferred_element_type=F32)
            s = jnp.where(jnp.concatenate([valid, valid], axis=0), s, NEG_BIG)
            m = jnp.max(s, axis=1, keepdims=True)
            p = jnp.exp(s - m)
            l = jnp.sum(p, axis=1, keepdims=True)
            pv = jnp.dot(p.astype(BF16), vw, preferred_element_type=F32) / l
            ml = m + jnp.log(l)
            out = jnp.where(head0, pv[:QBLK], pv[QBLK:])
            lse = jnp.where(head0, ml[:QBLK], ml[QBLK:])
            if d == 1:
                out_ref[0, pl.ds(i0, QBLK), :] = out
                lse_ref[0, pl.ds(i0, QBLK), :] = lse
            elif d == 4:
                rows = pl.ds(i0 * d + r, QBLK, stride=d)
                out_ref[1, rows, :] = out
                lse_ref[1, rows, :] = lse
            else:
                rows = pl.ds(r, QBLK, stride=RES16_PITCH)
                out16_ref[rows, :] = out
                lse16_ref[rows, :] = lse
            return carry

        lax.fori_loop(0, d * nblk, block, 0, unroll=ATTN_UNROLL)

    groups = QBLK // 16

    def merge(c, carry):
        rows = pl.ds(pl.multiple_of(c * QBLK, QBLK), QBLK)
        base = pl.multiple_of(c * (groups * RES16_PITCH), SUB)
        pieces = [pl.ds(base + g * RES16_PITCH, 16) for g in range(groups)]
        o3 = jnp.concatenate([out16_ref[pc, :] for pc in pieces], axis=0)
        e3 = jnp.concatenate([lse16_ref[pc, :] for pc in pieces], axis=0)
        e1, e2 = lse_ref[0, rows, :], lse_ref[1, rows, :]
        mx = jnp.maximum(jnp.maximum(e1, e2), e3)
        w1, w2, w3 = jnp.exp(e1 - mx), jnp.exp(e2 - mx), jnp.exp(e3 - mx)
        num = w1 * out_ref[0, rows, :] + w2 * out_ref[1, rows, :] + w3 * o3
        o_ref[rows, :] = num / (w1 + w2 + w3)
        return carry

    lax.fori_loop(0, seq // QBLK, merge, 0)


def _attention(q1, k1, v1, q4, k4, v4, q16, k16, v16, batch, seq):
    npair = D_B // LANES
    nat = pl.BlockSpec((None, 1, seq, LANES), lambda b, p: (b, 0, 0, p))
    r4 = pl.BlockSpec((None, 4, seq // 4, LANES), lambda b, p: (b, 0, 0, p))
    r16 = pl.BlockSpec((None, 16, seq // 16, LANES), lambda b, p: (b, 0, 0, p))
    q1, k1, v1 = (t.reshape(batch, 1, seq, D_B) for t in (q1, k1, v1))
    return pl.pallas_call(
        _attn_kernel,
        grid=(batch, npair),
        in_specs=[nat, nat, nat, r4, r4, r4, r16, r16, r16],
        out_specs=pl.BlockSpec((None, seq, LANES), lambda b, p: (b, 0, p)),
        out_shape=jax.ShapeDtypeStruct((batch, seq, D_B), F32),
        scratch_shapes=[pltpu.VMEM((2, seq, LANES), F32)] * 2
        + [pltpu.VMEM((seq // 16 * RES16_PITCH, LANES), F32)] * 2,
        compiler_params=pltpu.CompilerParams(
            dimension_semantics=("arbitrary", "arbitrary"), vmem_limit_bytes=VMEM_LIMIT),
        name="dilated_attn",
    )(q1, k1, v1, q4, k4, v4, q16, k16, v16)


def _topk_rows(s, k):
    n = s.shape[0]
    iota = lax.broadcasted_iota(I32, s.shape, 0).astype(F32)
    vals, idxs = [], []
    for _ in range(k):
        m = jnp.max(s, axis=0, keepdims=True)
        i = jnp.min(jnp.where(s == m, iota, float(n)), axis=0, keepdims=True)
        vals.append(m)
        idxs.append(i)
        s = jnp.where(iota == i, -jnp.inf, s)
    return jnp.concatenate(vals, axis=0), jnp.concatenate(idxs, axis=0).astype(I32)


def _take_rows(table, sel):
    out = jnp.zeros(sel.shape, table.dtype)
    for a in range(table.shape[0]):
        out = jnp.where(sel == a, table[a:a + 1, :], out)
    return out


def _mid_kernel(x_ref, an_ref, bo_ref, gb_ref, wout_ref, g2_ref, wq_ref, keys_ref,
                x1_ref, xn_ref, idx_ref, gate_ref):
    nt = x_ref.shape[0]
    bn = _rms(bo_ref[...], gb_ref[...]).astype(BF16)
    x1 = (x_ref[...]
          + jnp.dot(an_ref[...], wout_ref[:D_A, :], preferred_element_type=F32)
          + jnp.dot(bn, wout_ref[D_A:, :], preferred_element_type=F32))
    x1_ref[...] = x1
    xn = _rms(x1, g2_ref[...])
    xn_ref[...] = xn
    q = jnp.dot(xn.astype(BF16), wq_ref[...], preferred_element_type=F32).astype(BF16)
    keys = (keys_ref[0], keys_ref[1])
    half = D_KEY // 2
    for c in range(nt // LANES):
        qc = q[c * LANES:(c + 1) * LANES, :]
        experts, gates = [], []
        for h in range(PEER_HEADS):
            tops = []
            for p in range(2):
                qhp = qc[:, (2 * h + p) * half:(2 * h + p + 1) * half]
                s = lax.dot_general(keys[p], qhp, (((1,), (1,)), ((), ())),
                                    preferred_element_type=F32)
                tops.append(_topk_rows(s, PEER_TOPK))
            (s1, i1), (s2, i2) = tops
            cand = jnp.concatenate(
                [s1[0:1, :] + s2]
                + [s1[a:a + 1, :] + s2[0:SUB, :] for a in range(1, SUB)]
                + [s1[SUB:, :] + s2[0:1, :]], axis=0)
            sc, pos = _topk_rows(cand, PEER_TOPK)
            ca = jnp.where(pos < PEER_TOPK, 0,
                           jnp.where(pos < PEER_TOPK + SUB * (SUB - 1), (pos >> 3) - 1, pos - SUB * SUB))
            cb = jnp.where(pos < PEER_TOPK, pos,
                           jnp.where(pos < PEER_TOPK + SUB * (SUB - 1), pos & (SUB - 1), 0))
            e = _take_rows(i1, ca) * N_KEYS + _take_rows(i2, cb)
            ex = jnp.exp(sc - sc[0:1, :])
            gates.append(ex / jnp.sum(ex, axis=0, keepdims=True))
            experts.append(e)
        idx_ref[c] = jnp.concatenate(experts, axis=0).T
        gate_ref[c * LANES:(c + 1) * LANES, :] = jnp.concatenate(gates, axis=0).T


def _mid(x2, an, bo, gb, wout, g2, wq, keys):
    t_total = x2.shape[0]
    nt = MID_BLOCK
    row = lambda i: (i, 0)
    const2 = lambda i: (0, 0)
    return pl.pallas_call(
        _mid_kernel,
        grid=(t_total // nt,),
        in_specs=[
            pl.BlockSpec((nt, D_MODEL), row),
            pl.BlockSpec((nt, D_A), row),
            pl.BlockSpec((nt, D_B), row),
            pl.BlockSpec((1, D_B), const2),
            pl.BlockSpec((D_MODEL, D_MODEL), const2),
            pl.BlockSpec((1, D_MODEL), const2),
            pl.BlockSpec((D_MODEL, PEER_HEADS * D_KEY), const2),
            pl.BlockSpec((2, N_KEYS, D_KEY // 2), lambda i: (0, 0, 0)),
        ],
        out_specs=(
            pl.BlockSpec((nt, D_MODEL), row),
            pl.BlockSpec((nt, D_MODEL), row),
            pl.BlockSpec((nt // LANES, LANES, N_SLOTS), lambda i: (i, 0, 0)),
            pl.BlockSpec((nt, N_SLOTS), row),
        ),
        out_shape=(
            jax.ShapeDtypeStruct((t_total, D_MODEL), F32),
            jax.ShapeDtypeStruct((t_total, D_MODEL), F32),
            jax.ShapeDtypeStruct((t_total // LANES, LANES, N_SLOTS), I32),
            jax.ShapeDtypeStruct((t_total, N_SLOTS), F32),
        ),
        compiler_params=pltpu.CompilerParams(
            dimension_semantics=("arbitrary",), vmem_limit_bytes=VMEM_LIMIT),
        name="mid",
    )(x2, an, bo, gb, wout, g2, wq, keys)


def _peer_kernel(idx_ref, gate_ref, xn_ref, x1_ref, gf_ref, uv_ref, y_ref, *scratch):
    rows_refs = scratch[:PEER_RING]
    bf_ref, sem_ref = scratch[PEER_RING:]
    nt = xn_ref.shape[0]
    wide = 2 * LANES
    nw = 2 * D_MODEL // wide

    def issue(t, slot, lo=0, hi=N_SLOTS):
        for s in range(lo, hi):
            pltpu.make_async_copy(uv_ref.at[idx_ref[t, s]],
                                  rows_refs[slot].at[s // SUB, :, s % SUB, :],
                                  sem_ref.at[slot]).start(priority=s % 2)

    def wait(slot):
        pltpu.make_async_copy(uv_ref.at[pl.ds(0, N_SLOTS)],
                              rows_refs[slot].reshape(N_SLOTS, NCH, LANES), sem_ref.at[slot]).wait()

    def stage(slot, k):
        for j in range(NCH):
            w = rows_refs[slot][:, j, :, :].reshape(N_SLOTS, LANES)
            bf_ref[k, :, j * LANES:(j + 1) * LANES] = (
                pltpu.bitcast(w & U_HALF, F32).astype(BF16))
            bf_ref[k, :, D_MODEL + j * LANES:D_MODEL + (j + 1) * LANES] = (
                pltpu.bitcast(w << 16, F32).astype(BF16))

    def pair(t0, slots, prefetch):
        for k in range(2):
            wait(slots[k])
            stage(slots[k], k)
        nbatch = 2 * nw
        per = N_SLOTS // (nbatch // 2)
        batches = [(k, b * per, (b + 1) * per) for b in range(nbatch // 2) for k in range(2)]

        def next_batch():
            if prefetch and batches:
                k, lo, hi = batches.pop(0)
                issue(t0 + PEER_RING + k, slots[k], lo, hi)

        x8 = [jnp.broadcast_to(xn_ref[pl.ds(t0 + k, 1), :], (SUB, D_MODEL)).astype(BF16)
              for k in range(2)]
        act = [jnp.zeros((SUB, N_SLOTS), F32) for _ in range(2)]
        for j in range(nw // 2):
            for k in range(2):
                next_batch()
                act[k] = act[k] + lax.dot_general(
                    x8[k][:, j * wide:(j + 1) * wide], bf_ref[k, :, j * wide:(j + 1) * wide],
                    (((1,), (1,)), ((), ())), preferred_element_type=F32)
        w = [(_gelu(act[k]) * gate_ref[pl.ds(t0 + k, 1), :]).astype(BF16) for k in range(2)]
        outs = [[], []]
        for j in range(nw // 2):
            for k in range(2):
                next_batch()
                outs[k].append(jnp.dot(
                    w[k], bf_ref[k, :, D_MODEL + j * wide:D_MODEL + (j + 1) * wide],
                    preferred_element_type=F32)[0:1, :])
        for k in range(2):
            out = jnp.concatenate(outs[k], axis=1)
            y_ref[pl.ds(t0 + k, 1), :] = _rms(x1_ref[pl.ds(t0 + k, 1), :] + out, gf_ref[...])

    def group(g, prefetch):
        for p in range(PEER_RING // 2):
            pair(g * PEER_RING + 2 * p, (2 * p, 2 * p + 1), prefetch)

    for t in range(PEER_RING):
        issue(t, t)
    ngroup = nt // PEER_RING
    lax.fori_loop(0, ngroup - 1, lambda g, c: (group(g, True), c)[1], 0)
    group(ngroup - 1, False)


def _peer(idx, gates, xn, x1, gf, uv, first_token):
    t_total = xn.shape[0] - first_token
    nt = PEER_BLOCK
    b0 = first_token // nt
    row = lambda i: (i + b0, 0)
    return pl.pallas_call(
        _peer_kernel,
        grid=(t_total // nt,),
        in_specs=[
            pl.BlockSpec((None, nt, N_SLOTS), lambda i: (i + b0, 0, 0), memory_space=pltpu.SMEM),
            pl.BlockSpec((nt, N_SLOTS), row),
            pl.BlockSpec((nt, D_MODEL), row),
            pl.BlockSpec((nt, D_MODEL), row),
            pl.BlockSpec((1, D_MODEL), lambda i: (0, 0)),
            pl.BlockSpec(memory_space=pl.ANY),
        ],
        out_specs=pl.BlockSpec((nt, D_MODEL), lambda i: (i, 0)),
        out_shape=jax.ShapeDtypeStruct((t_total, D_MODEL), F32),
        scratch_shapes=[pltpu.VMEM((N_SLOTS // SUB, NCH, SUB, LANES), I32)] * PEER_RING + [
            pltpu.VMEM((2, N_SLOTS, 2 * D_MODEL), BF16),
            pltpu.SemaphoreType.DMA((PEER_RING,))],
        compiler_params=pltpu.CompilerParams(
            dimension_semantics=("arbitrary",), vmem_limit_bytes=VMEM_LIMIT),
        name="peer",
    )(idx, gates, xn, x1, gf, uv)


def _sc_peer(idx, xn, gates, uv2, n_tokens):
    info = plsc.get_sparse_core_info()
    nc, lanes_n = info.num_cores, info.num_lanes
    nw = nc * info.num_subcores
    per = n_tokens // nw
    assert n_tokens % nw == 0 and per % 2 == 0
    nchunk = N_SLOTS // SC_ROWS
    qv = D_MODEL // (SC_PASSES * lanes_n)
    c0 = math.sqrt(2.0 / math.pi)
    mesh = plsc.VectorSubcoreMesh(core_axis_name="c", subcore_axis_name="s")
    dma = pltpu.SemaphoreType.DMA

    @functools.partial(
        pl.kernel, mesh=mesh,
        out_type=jax.ShapeDtypeStruct((n_tokens, D_MODEL), F32),
        scratch_types=[
            [pltpu.VMEM((nchunk, SC_ROWS), I32)] * 2,
            [pltpu.VMEM((D_MODEL,), F32)] * 2,
            [pltpu.VMEM((N_SLOTS,), F32)] * 2,
            [pltpu.VMEM((D_MODEL,), F32)] * 2,
            [pltpu.VMEM((SC_ROWS, D_MODEL), I32)] * 2,
            [dma] * 2, [dma] * 2, [dma] * 2, [dma] * 2, [dma] * 2,
        ],
        compiler_params=pltpu.CompilerParams(needs_layout_passes=False),
        name="peer_sc",
    )
    def sc_kernel(idx_hbm, xn_hbm, gate_hbm, uv_hbm, out_hbm, idx_v, x_v, g_v, out_v, rows_v,
                  row_sem, idx_sem, x_sem, g_sem, out_sem):
        wid = lax.axis_index("s") * nc + lax.axis_index("c")
        t0 = wid * per
        lane_ids = lax.iota(I32, lanes_n)
        zero = jnp.zeros((lanes_n,), F32)

        def vec(q, j):
            return pl.ds((q * qv + j) * lanes_n, lanes_n)

        def gather(p, c, b):
            return pltpu.make_async_copy(uv_hbm.at[idx_v[p].at[c]], rows_v[b], row_sem[b])

        def inputs(tok, p):
            return (pltpu.make_async_copy(idx_hbm.at[tok], idx_v[p], idx_sem[p]),
                    pltpu.make_async_copy(xn_hbm.at[tok], x_v[p], x_sem[p]),
                    pltpu.make_async_copy(gate_hbm.at[tok], g_v[p], g_sem[p]))

        def result(tok, p):
            return pltpu.make_async_copy(out_v[p], out_hbm.at[tok], out_sem[p])

        def chunk(p, c, b):
            rows = rows_v[b]

            def ustep(j, acc):
                xj = x_v[p][pl.ds(j * lanes_n, lanes_n)]
                return tuple(
                    acc[r] + plsc.bitcast(rows[r, pl.ds(j * lanes_n, lanes_n)] & U_HALF, F32) * xj
                    for r in range(SC_ROWS))

            acc = lax.fori_loop(0, D_MODEL // lanes_n, ustep, (zero,) * SC_ROWS)
            act = zero
            for r in range(SC_ROWS):
                act = jnp.where(lane_ids == r, jnp.sum(acc[r]), act)
            z = c0 * (act + 0.044715 * (act * act * act))
            tanh_z = 1.0 - 2.0 / (jnp.exp(2.0 * z) + 1.0)
            w = 0.5 * act * (1.0 + tanh_z) * g_v[p][pl.ds(c * SC_ROWS, SC_ROWS)]
            for q in range(SC_PASSES):
                o = tuple(out_v[p][vec(q, j)] for j in range(qv))

                def vrow(r, o, q=q):
                    wr = w.at[jnp.full((lanes_n,), r, I32)].get(mode="promise_in_bounds")
                    return tuple(o[j] + wr * plsc.bitcast(rows[r, vec(q, j)] << 16, F32)
                                 for j in range(qv))

                o = lax.fori_loop(0, SC_ROWS, vrow, o)
                for j in range(qv):
                    out_v[p][vec(q, j)] = o[j]

        def token(i, p):
            tok = t0 + i
            more = i + 1 < per

            @pl.when(more)
            def _():
                for cp in inputs(tok + 1, 1 - p):
                    cp.start()

            @pl.when(i >= 2)
            def _():
                result(tok - 2, p).wait()

            for j in range(D_MODEL // lanes_n):
                out_v[p][pl.ds(j * lanes_n, lanes_n)] = zero

            @pl.loop(0, nchunk, step=2)
            def _(c):
                gather(p, c + 1, 1).start()
                gather(p, c, 0).wait()
                chunk(p, c, 0)

                @pl.when(c + 2 < nchunk)
                def _():
                    gather(p, c + 2, 0).start()

                @pl.when(jnp.logical_and(c + 2 >= nchunk, more))
                def _():
                    for cp in inputs(tok + 1, 1 - p):
                        cp.wait()
                    gather(1 - p, 0, 0).start()

                gather(p, c + 1, 1).wait()
                chunk(p, c + 1, 1)

            result(tok, p).start()

        for cp in inputs(t0, 0):
            cp.start()
        for cp in inputs(t0, 0):
            cp.wait()
        gather(0, 0, 0).start()

        @pl.loop(0, per, step=2)
        def _(i):
            token(i, 0)
            token(i + 1, 1)

        result(t0 + per - 2, 0).wait()
        result(t0 + per - 1, 1).wait()

    return sc_kernel(idx.reshape(-1, nchunk, SC_ROWS), xn, gates, uv2)


def _residual_norm_kernel(x1_ref, o_ref, gf_ref, y_ref):
    y_ref[...] = _rms(x1_ref[...] + o_ref[...], gf_ref[...])


def _residual_norm(x1, out, gf):
    n = out.shape[0]
    nt = PEER_BLOCK
    row = lambda i: (i, 0)
    return pl.pallas_call(
        _residual_norm_kernel,
        grid=(n // nt,),
        in_specs=[pl.BlockSpec((nt, D_MODEL), row), pl.BlockSpec((nt, D_MODEL), row),
                  pl.BlockSpec((1, D_MODEL), lambda i: (0, 0))],
        out_specs=pl.BlockSpec((nt, D_MODEL), row),
        out_shape=jax.ShapeDtypeStruct((n, D_MODEL), F32),
        compiler_params=pltpu.CompilerParams(dimension_semantics=("arbitrary",)),
        name="residual_norm",
    )(x1, out, gf)


def _rope_tables(seq):
    pos = jnp.arange(seq, dtype=F32)
    inv = 1.0 / (ROPE_THETA ** (jnp.arange(0, HEAD_DIM, 2, dtype=F32) / HEAD_DIM))
    ang = pos[:, None] * inv[None, :]
    cos, sin = jnp.cos(ang), jnp.sin(ang)
    cosf = jnp.tile(jnp.concatenate([cos, cos], axis=1), (1, B_HEADS))
    sins = jnp.tile(jnp.concatenate([-sin, sin], axis=1), (1, B_HEADS))
    return cosf, sins


def kernel(x, norm1_g, w_in, ln_v_g, ln_v_b, w_spatial, b_spatial, out_norm_a_g, out_norm_b_g,
           w_out, norm2_g, w_query, sub_keys, expert_u, expert_v, final_norm_g):
    batch, seq, _ = x.shape
    assert w_in.shape[0] == 1 and seq % (16 * QBLK) == 0 and seq % IN_BLOCK == 0
    row = lambda g: g.reshape(1, -1).astype(F32)

    ws = w_spatial[0].astype(BF16)
    ws_cat = jnp.concatenate([ws[0::2], ws[1::2]], axis=2)
    bs_full = jnp.repeat(b_spatial[0].T, A_GROUP_DIM, axis=1)
    cosf, sins = _rope_tables(seq)
    win, wout, wq = w_in[0].astype(BF16), w_out[0].astype(BF16), w_query[0].astype(BF16)
    keys = sub_keys[0].astype(BF16)
    gf = row(final_norm_g)
    half = lambda t: lax.bitcast_convert_type(t.astype(BF16), jnp.uint16).astype(jnp.uint32)
    uv2 = lax.bitcast_convert_type((half(expert_u[0]) << 16) | half(expert_v[0]), I32)
    uv = uv2.reshape(-1, NCH, LANES)

    nchunk = PIPE_CHUNKS if batch % PIPE_CHUNKS == 0 else 1
    cb = batch // nchunk
    pieces = []
    for ci in range(nchunk):
        x2 = x[ci * cb:(ci + 1) * cb].reshape(cb * seq, D_MODEL)
        an, q1, k1, v1, q4, k4, v4, q16, k16, v16 = _in_proj(
            x2, row(norm1_g[0]), win, row(ln_v_g[0]), row(ln_v_b[0]),
            ws_cat, bs_full, row(out_norm_a_g[0]), cosf, sins, cb, seq)
        bo = _attention(q1, k1, v1, q4, k4, v4, q16, k16, v16, cb, seq).reshape(cb * seq, D_B)
        x1, xn, idx, gates = _mid(x2, an, bo, row(out_norm_b_g[0]), wout, row(norm2_g[0]), wq, keys)
        n_sc = (cb * seq) * SC_SHARE[0] // SC_SHARE[1]
        assert n_sc % PEER_BLOCK == 0
        out_sc = _sc_peer(idx.reshape(cb * seq, N_SLOTS), xn, gates, uv2, n_sc)
        y_tc = _peer(idx, gates, xn, x1, gf, uv, n_sc)
        pieces += [_residual_norm(x1, out_sc, gf), y_tc]
    return jnp.concatenate(pieces, axis=0).reshape(batch, seq, D_MODEL)
```

```python
import functools
import math

import jax
import jax.numpy as jnp
from jax import lax
from jax.experimental import pallas as pl
from jax.experimental.pallas import tpu as pltpu
from jax.experimental.pallas import tpu_sc as plsc

F32 = jnp.float32
BF16 = jnp.bfloat16
I32 = jnp.int32

D_MODEL = 1024
D_A = 512
D_B = 512
A_GROUPS = 8
A_GROUP_DIM = 64
CHUNK = 128
B_HEADS = 8
HEAD_DIM = 64
DILATIONS = (1, 4, 16)
HALF_WINDOW = 64
ROPE_THETA = 10000.0
D_IN = 2 * D_A + 3 * D_B
N_KEYS = 128
PEER_HEADS = 8
PEER_TOPK = 16
D_KEY = 256
N_SLOTS = PEER_HEADS * PEER_TOPK
EPS = 1e-6
NEG_BIG = -1e30

LANES = 128
SUB = 8
NCH = D_MODEL // LANES
U_HALF = -65536
QBLK = 128
ATTN_UNROLL = 8
RES16_PITCH = 24
IN_BLOCK = 512
MID_BLOCK = 256
PEER_BLOCK = 128
PEER_RING = 4
SC_SHARE = (45, 64)
PIPE_CHUNKS = 8
SC_ROWS = 16
SC_PASSES = 4
VMEM_LIMIT = 48 * 1024 * 1024


def _gelu(x):
    c = math.sqrt(2.0 / math.pi)
    return 0.5 * x * (1.0 + jnp.tanh(c * (x + 0.044715 * (x * x * x))))


def _rms(x, g):
    return x * lax.rsqrt(jnp.mean(x * x, axis=-1, keepdims=True) + EPS) * g


def _in_proj_kernel(x_ref, g1_ref, win_ref, lng_ref, lnb_ref, ws_ref, bs_ref, ga_ref,
                    cos_ref, sin_ref,
                    an_ref, q1_ref, k1_ref, v1_ref, q4_ref, k4_ref, v4_ref,
                    q16_ref, k16_ref, v16_ref, slab_ref):
    nt = x_ref.shape[0]
    h = _rms(x_ref[...], g1_ref[...]).astype(BF16)
    proj = jnp.dot(h, win_ref[...], preferred_element_type=F32)

    u = _gelu(proj[:, :D_A])
    v = _gelu(proj[:, D_A:2 * D_A])
    mu = jnp.mean(v, axis=-1, keepdims=True)
    vc = v - mu
    var = jnp.mean(vc * vc, axis=-1, keepdims=True)
    vln = (vc * lax.rsqrt(var + EPS) * lng_ref[...] + lnb_ref[...]).astype(BF16)
    lane = lax.broadcasted_iota(I32, (CHUNK, LANES), 1)
    lo = lane < A_GROUP_DIM
    zero = jnp.zeros((CHUNK, LANES), BF16)
    chunks = []
    for c in range(nt // CHUNK):
        cols = []
        for j in range(A_GROUPS // 2):
            vv = vln[c * CHUNK:(c + 1) * CHUNK, j * LANES:(j + 1) * LANES]
            rhs = jnp.concatenate([jnp.where(lo, vv, zero), jnp.where(lo, zero, vv)], axis=0)
            cols.append(jnp.dot(ws_ref[j], rhs, preferred_element_type=F32))
        chunks.append(jnp.concatenate(cols, axis=1) + bs_ref[...])
    mixed = jnp.concatenate(chunks, axis=0)
    an_ref[...] = _rms(u * mixed, ga_ref[...]).astype(BF16)

    cosf = cos_ref[...]
    sins = sin_ref[...]
    lane_b = lax.broadcasted_iota(I32, (nt, D_B), 1)
    first_half = (lane_b % HEAD_DIM) < (HEAD_DIM // 2)

    def rope(t):
        partner = jnp.where(first_half, pltpu.roll(t, D_B - HEAD_DIM // 2, 1),
                            pltpu.roll(t, HEAD_DIM // 2, 1))
        return t * cosf + partner * sins

    q = rope(proj[:, 2 * D_A:2 * D_A + D_B]) * (HEAD_DIM ** -0.5)
    k = rope(proj[:, 2 * D_A + D_B:2 * D_A + 2 * D_B])
    vv = proj[:, 2 * D_A + 2 * D_B:]
    q1_ref[...] = q.astype(BF16)
    k1_ref[...] = k.astype(BF16)
    v1_ref[...] = vv.astype(BF16)

    nslab = D_B // LANES
    for a, t in enumerate((q, k, vv)):
        for s in range(nslab):
            slab_ref[a * nslab + s] = t[:, s * LANES:(s + 1) * LANES]
    for d, outs in ((4, (q4_ref, k4_ref, v4_ref)), (16, (q16_ref, k16_ref, v16_ref))):
        rows = nt // d
        for a, o_ref in enumerate(outs):
            for r in range(d):
                for s in range(nslab):
                    o_ref[r, :, s * LANES:(s + 1) * LANES] = (
                        slab_ref[a * nslab + s, pl.ds(r, rows, stride=d), :].astype(BF16))


def _in_proj(x2, g1, win, lng, lnb, ws_cat, bs_full, ga, cosf, sins, batch, seq):
    t_total = x2.shape[0]
    nt = IN_BLOCK
    nb = seq // nt
    grid = (t_total // nt,)
    row = lambda i: (i, 0)
    const2 = lambda i: (0, 0)
    tok_bf = jax.ShapeDtypeStruct((t_total, D_B), BF16)
    out_shape = (
        jax.ShapeDtypeStruct((t_total, D_A), BF16),
        tok_bf, tok_bf, tok_bf,
        *(jax.ShapeDtypeStruct((batch, 4, seq // 4, D_B), BF16),) * 3,
        *(jax.ShapeDtypeStruct((batch, 16, seq // 16, D_B), BF16),) * 3,
    )
    res4 = pl.BlockSpec((None, 4, nt // 4, D_B), lambda i: (i // nb, 0, i % nb, 0))
    res16 = pl.BlockSpec((None, 16, nt // 16, D_B), lambda i: (i // nb, 0, i % nb, 0))
    tok_spec = pl.BlockSpec((nt, D_B), row)
    return pl.pallas_call(
        _in_proj_kernel,
        grid=grid,
        in_specs=[
            pl.BlockSpec((nt, D_MODEL), row),
            pl.BlockSpec((1, D_MODEL), const2),
            pl.BlockSpec((D_MODEL, D_IN), const2),
            pl.BlockSpec((1, D_A), const2),
            pl.BlockSpec((1, D_A), const2),
            pl.BlockSpec((A_GROUPS // 2, CHUNK, 2 * CHUNK), lambda i: (0, 0, 0)),
            pl.BlockSpec((CHUNK, D_A), const2),
            pl.BlockSpec((1, D_A), const2),
            pl.BlockSpec((nt, D_B), lambda i: (i % nb, 0)),
            pl.BlockSpec((nt, D_B), lambda i: (i % nb, 0)),
        ],
        out_specs=(pl.BlockSpec((nt, D_A), row), tok_spec, tok_spec, tok_spec,
                   res4, res4, res4, res16, res16, res16),
        out_shape=out_shape,
        scratch_shapes=[pltpu.VMEM((3 * D_B // LANES, nt, LANES), F32)],
        compiler_params=pltpu.CompilerParams(
            dimension_semantics=("arbitrary",), vmem_limit_bytes=VMEM_LIMIT),
        name="in_proj",
    )(x2, g1, win, lng, lnb, ws_cat, bs_full, ga, cosf, sins)


def _attn_kernel(q1_ref, k1_ref, v1_ref, q4_ref, k4_ref, v4_ref, q16_ref, k16_ref, v16_ref,
                 o_ref, out_ref, lse_ref, out16_ref, lse16_ref):
    seq = o_ref.shape[0]
    lane = lax.broadcasted_iota(I32, (QBLK, LANES), 1)
    head0 = lane < HEAD_DIM
    refs = ((q1_ref, k1_ref, v1_ref), (q4_ref, k4_ref, v4_ref), (q16_ref, k16_ref, v16_ref))
    for bi, (d, (q_ref, k_ref, v_ref)) in enumerate(zip(DILATIONS, refs)):
        length = seq // d
        nblk = length // QBLK
        win = min(2 * QBLK, length)
        diff = (lax.broadcasted_iota(I32, (QBLK, win), 1)
                - lax.broadcasted_iota(I32, (QBLK, win), 0))

        def block(blk, carry, d=d, bi=bi, q_ref=q_ref, k_ref=k_ref, v_ref=v_ref,
                  length=length, nblk=nblk, win=win, diff=diff):
            r = blk // nblk
            i0 = pl.multiple_of((blk % nblk) * QBLK, QBLK)
            w0 = pl.multiple_of(jnp.clip(i0 - HALF_WINDOW, 0, length - win), HALF_WINDOW)
            qb = q_ref[r, pl.ds(i0, QBLK), :]
            kw = k_ref[r, pl.ds(w0, win), :]
            vw = v_ref[r, pl.ds(w0, win), :]
            rel = diff + (w0 - i0)
            valid = (rel >= -HALF_WINDOW) & (rel <= HALF_WINDOW)
            zero = jnp.zeros_like(qb)
            qq = jnp.concatenate([jnp.where(head0, qb, zero), jnp.where(head0, zero, qb)], axis=0)
            s = lax.dot_general(qq, kw, (((1,), (1,)), ((), ())), preferred_element_type=F32)
            s = jnp.where(jnp.concatenate([valid, valid], axis=0), s, NEG_BIG)
            m = jnp.max(s, axis=1, keepdims=True)
            p = jnp.exp(s - m)
            l = jnp.sum(p, axis=1, keepdims=True)
            pv = jnp.dot(p.astype(BF16), vw, preferred_element_type=F32) / l
            ml = m + jnp.log(l)
            out = jnp.where(head0, pv[:QBLK], pv[QBLK:])
            lse = jnp.where(head0, ml[:QBLK], ml[QBLK:])
            if d == 1:
                out_ref[0, pl.ds(i0, QBLK), :] = out
                lse_ref[0, pl.ds(i0, QBLK), :] = lse
            elif d == 4:
                rows = pl.ds(i0 * d + r, QBLK, stride=d)
                out_ref[1, rows, :] = out
                lse_ref[1, rows, :] = lse
            else:
                rows = pl.ds(r, QBLK, stride=RES16_PITCH)
                out16_ref[rows, :] = out
                lse16_ref[rows, :] = lse
            return carry

        lax.fori_loop(0, d * nblk, block, 0, unroll=ATTN_UNROLL)

    groups = QBLK // 16

    def merge(c, carry):
        rows = pl.ds(pl.multiple_of(c * QBLK, QBLK), QBLK)
        base = pl.multiple_of(c * (groups * RES16_PITCH), SUB)
        pieces = [pl.ds(base + g * RES16_PITCH, 16) for g in range(groups)]
        o3 = jnp.concatenate([out16_ref[pc, :] for pc in pieces], axis=0)
        e3 = jnp.concatenate([lse16_ref[pc, :] for pc in pieces], axis=0)
        e1, e2 = lse_ref[0, rows, :], lse_ref[1, rows, :]
        mx = jnp.maximum(jnp.maximum(e1, e2), e3)
        w1, w2, w3 = jnp.exp(e1 - mx), jnp.exp(e2 - mx), jnp.exp(e3 - mx)
        num = w1 * out_ref[0, rows, :] + w2 * out_ref[1, rows, :] + w3 * o3
        o_ref[rows, :] = num / (w1 + w2 + w3)
        return carry

    lax.fori_loop(0, seq // QBLK, merge, 0)


def _attention(q1, k1, v1, q4, k4, v4, q16, k16, v16, batch, seq):
    npair = D_B // LANES
    nat = pl.BlockSpec((None, 1, seq, LANES), lambda b, p: (b, 0, 0, p))
    r4 = pl.BlockSpec((None, 4, seq // 4, LANES), lambda b, p: (b, 0, 0, p))
    r16 = pl.BlockSpec((None, 16, seq // 16, LANES), lambda b, p: (b, 0, 0, p))
    q1, k1, v1 = (t.reshape(batch, 1, seq, D_B) for t in (q1, k1, v1))
    return pl.pallas_call(
        _attn_kernel,
        grid=(batch, npair),
        in_specs=[nat, nat, nat, r4, r4, r4, r16, r16, r16],
        out_specs=pl.BlockSpec((None, seq, LANES), lambda b, p: (b, 0, p)),
        out_shape=jax.ShapeDtypeStruct((batch, seq, D_B), F32),
        scratch_shapes=[pltpu.VMEM((2, seq, LANES), F32)] * 2
        + [pltpu.VMEM((seq // 16 * RES16_PITCH, LANES), F32)] * 2,
        compiler_params=pltpu.CompilerParams(
            dimension_semantics=("arbitrary", "arbitrary"), vmem_limit_bytes=VMEM_LIMIT),
        name="dilated_attn",
    )(q1, k1, v1, q4, k4, v4, q16, k16, v16)


def _topk_rows(s, k):
    n = s.shape[0]
    iota = lax.broadcasted_iota(I32, s.shape, 0).astype(F32)
    vals, idxs = [], []
    for _ in range(k):
        m = jnp.max(s, axis=0, keepdims=True)
        i = jnp.min(jnp.where(s == m, iota, float(n)), axis=0, keepdims=True)
        vals.append(m)
        idxs.append(i)
        s = jnp.where(iota == i, -jnp.inf, s)
    return jnp.concatenate(vals, axis=0), jnp.concatenate(idxs, axis=0).astype(I32)


def _take_rows(table, sel):
    out = jnp.zeros(sel.shape, table.dtype)
    for a in range(table.shape[0]):
        out = jnp.where(sel == a, table[a:a + 1, :], out)
    return out


def _mid_kernel(x_ref, an_ref, bo_ref, gb_ref, wout_ref, g2_ref, wq_ref, keys_ref,
                x1_ref, xn_ref, idx_ref, gate_ref):
    nt = x_ref.shape[0]
    bn = _rms(bo_ref[...], gb_ref[...]).astype(BF16)
    x1 = (x_ref[...]
          + jnp.dot(an_ref[...], wout_ref[:D_A, :], preferred_element_type=F32)
          + jnp.dot(bn, wout_ref[D_A:, :], preferred_element_type=F32))
    x1_ref[...] = x1
    xn = _rms(x1, g2_ref[...])
    xn_ref[...] = xn
    q = jnp.dot(xn.astype(BF16), wq_ref[...], preferred_element_type=F32).astype(BF16)
    keys = (keys_ref[0], keys_ref[1])
    half = D_KEY // 2
    for c in range(nt // LANES):
        qc = q[c * LANES:(c + 1) * LANES, :]
        experts, gates = [], []
        for h in range(PEER_HEADS):
            tops = []
            for p in range(2):
                qhp = qc[:, (2 * h + p) * half:(2 * h + p + 1) * half]
                s = lax.dot_general(keys[p], qhp, (((1,), (1,)), ((), ())),
                                    preferred_element_type=F32)
                tops.append(_topk_rows(s, PEER_TOPK))
            (s1, i1), (s2, i2) = tops
            cand = jnp.concatenate(
                [s1[0:1, :] + s2]
                + [s1[a:a + 1, :] + s2[0:SUB, :] for a in range(1, SUB)]
                + [s1[SUB:, :] + s2[0:1, :]], axis=0)
            sc, pos = _topk_rows(cand, PEER_TOPK)
            ca = jnp.where(pos < PEER_TOPK, 0,
                           jnp.where(pos < PEER_TOPK + SUB * (SUB - 1), (pos >> 3) - 1, pos - SUB * SUB))
            cb = jnp.where(pos < PEER_TOPK, pos,
                           jnp.where(pos < PEER_TOPK + SUB * (SUB - 1), pos & (SUB - 1), 0))
            e = _take_rows(i1, ca) * N_KEYS + _take_rows(i2, cb)
            ex = jnp.exp(sc - sc[0:1, :])
            gates.append(ex / jnp.sum(ex, axis=0, keepdims=True))
            experts.append(e)
        idx_ref[c] = jnp.concatenate(experts, axis=0).T
        gate_ref[c * LANES:(c + 1) * LANES, :] = jnp.concatenate(gates, axis=0).T


def _mid(x2, an, bo, gb, wout, g2, wq, keys):
    t_total = x2.shape[0]
    nt = MID_BLOCK
    row = lambda i: (i, 0)
    const2 = lambda i: (0, 0)
    return pl.pallas_call(
        _mid_kernel,
        grid=(t_total // nt,),
        in_specs=[
            pl.BlockSpec((nt, D_MODEL), row),
            pl.BlockSpec((nt, D_A), row),
            pl.BlockSpec((nt, D_B), row),
            pl.BlockSpec((1, D_B), const2),
            pl.BlockSpec((D_MODEL, D_MODEL), const2),
            pl.BlockSpec((1, D_MODEL), const2),
            pl.BlockSpec((D_MODEL, PEER_HEADS * D_KEY), const2),
            pl.BlockSpec((2, N_KEYS, D_KEY // 2), lambda i: (0, 0, 0)),
        ],
        out_specs=(
            pl.BlockSpec((nt, D_MODEL), row),
            pl.BlockSpec((nt, D_MODEL), row),
            pl.BlockSpec((nt // LANES, LANES, N_SLOTS), lambda i: (i, 0, 0)),
            pl.BlockSpec((nt, N_SLOTS), row),
        ),
        out_shape=(
            jax.ShapeDtypeStruct((t_total, D_MODEL), F32),
            jax.ShapeDtypeStruct((t_total, D_MODEL), F32),
            jax.ShapeDtypeStruct((t_total // LANES, LANES, N_SLOTS), I32),
            jax.ShapeDtypeStruct((t_total, N_SLOTS), F32),
        ),
        compiler_params=pltpu.CompilerParams(
            dimension_semantics=("arbitrary",), vmem_limit_bytes=VMEM_LIMIT),
        name="mid",
    )(x2, an, bo, gb, wout, g2, wq, keys)


def _peer_kernel(idx_ref, gate_ref, xn_ref, x1_ref, gf_ref, uv_ref, y_ref, *scratch):
    rows_refs = scratch[:PEER_RING]
    bf_ref, sem_ref = scratch[PEER_RING:]
    nt = xn_ref.shape[0]
    wide = 2 * LANES
    nw = 2 * D_MODEL // wide

    def issue(t, slot, lo=0, hi=N_SLOTS):
        for s in range(lo, hi):
            pltpu.make_async_copy(uv_ref.at[idx_ref[t, s]],
                                  rows_refs[slot].at[s // SUB, :, s % SUB, :],
                                  sem_ref.at[slot]).start(priority=s % 2)

    def wait(slot):
        pltpu.make_async_copy(uv_ref.at[pl.ds(0, N_SLOTS)],
                              rows_refs[slot].reshape(N_SLOTS, NCH, LANES), sem_ref.at[slot]).wait()

    def stage(slot, k):
        for j in range(NCH):
            w = rows_refs[slot][:, j, :, :].reshape(N_SLOTS, LANES)
            bf_ref[k, :, j * LANES:(j + 1) * LANES] = (
                pltpu.bitcast(w & U_HALF, F32).astype(BF16))
            bf_ref[k, :, D_MODEL + j * LANES:D_MODEL + (j + 1) * LANES] = (
                pltpu.bitcast(w << 16, F32).astype(BF16))

    def pair(t0, slots, prefetch):
        for k in range(2):
            wait(slots[k])
            stage(slots[k], k)
        nbatch = 2 * nw
        per = N_SLOTS // (nbatch // 2)
        batches = [(k, b * per, (b + 1) * per) for b in range(nbatch // 2) for k in range(2)]

        def next_batch():
            if prefetch and batches:
                k, lo, hi = batches.pop(0)
                issue(t0 + PEER_RING + k, slots[k], lo, hi)

        x8 = [jnp.broadcast_to(xn_ref[pl.ds(t0 + k, 1), :], (SUB, D_MODEL)).astype(BF16)
              for k in range(2)]
        act = [jnp.zeros((SUB, N_SLOTS), F32) for _ in range(2)]
        for j in range(nw // 2):
            for k in range(2):
                next_batch()
                act[k] = act[k] + lax.dot_general(
                    x8[k][:, j * wide:(j + 1) * wide], bf_ref[k, :, j * wide:(j + 1) * wide],
                    (((1,), (1,)), ((), ())), preferred_element_type=F32)
        w = [(_gelu(act[k]) * gate_ref[pl.ds(t0 + k, 1), :]).astype(BF16) for k in range(2)]
        outs = [[], []]
        for j in range(nw // 2):
            for k in range(2):
                next_batch()
                outs[k].append(jnp.dot(
                    w[k], bf_ref[k, :, D_MODEL + j * wide:D_MODEL + (j + 1) * wide],
                    preferred_element_type=F32)[0:1, :])
        for k in range(2):
            out = jnp.concatenate(outs[k], axis=1)
            y_ref[pl.ds(t0 + k, 1), :] = _rms(x1_ref[pl.ds(t0 + k, 1), :] + out, gf_ref[...])

    def group(g, prefetch):
        for p in range(PEER_RING // 2):
            pair(g * PEER_RING + 2 * p, (2 * p, 2 * p + 1), prefetch)

    for t in range(PEER_RING):
        issue(t, t)
    ngroup = nt // PEER_RING
    lax.fori_loop(0, ngroup - 1, lambda g, c: (group(g, True), c)[1], 0)
    group(ngroup - 1, False)


def _peer(idx, gates, xn, x1, gf, uv, first_token):
    t_total = xn.shape[0] - first_token
    nt = PEER_BLOCK
    b0 = first_token // nt
    row = lambda i: (i + b0, 0)
    return pl.pallas_call(
        _peer_kernel,
        grid=(t_total // nt,),
        in_specs=[
            pl.BlockSpec((None, nt, N_SLOTS), lambda i: (i + b0, 0, 0), memory_space=pltpu.SMEM),
            pl.BlockSpec((nt, N_SLOTS), row),
            pl.BlockSpec((nt, D_MODEL), row),
            pl.BlockSpec((nt, D_MODEL), row),
            pl.BlockSpec((1, D_MODEL), lambda i: (0, 0)),
            pl.BlockSpec(memory_space=pl.ANY),
        ],
        out_specs=pl.BlockSpec((nt, D_MODEL), lambda i: (i, 0)),
        out_shape=jax.ShapeDtypeStruct((t_total, D_MODEL), F32),
        scratch_shapes=[pltpu.VMEM((N_SLOTS // SUB, NCH, SUB, LANES), I32)] * PEER_RING + [
            pltpu.VMEM((2, N_SLOTS, 2 * D_MODEL), BF16),
            pltpu.SemaphoreType.DMA((PEER_RING,))],
        compiler_params=pltpu.CompilerParams(
            dimension_semantics=("arbitrary",), vmem_limit_bytes=VMEM_LIMIT),
        name="peer",
    )(idx, gates, xn, x1, gf, uv)


def _sc_peer(idx, xn, gates, uv2, n_tokens):
    info = plsc.get_sparse_core_info()
    nc, lanes_n = info.num_cores, info.num_lanes
    nw = nc * info.num_subcores
    per = n_tokens // nw
    assert n_tokens % nw == 0 and per % 2 == 0
    nchunk = N_SLOTS // SC_ROWS
    qv = D_MODEL // (SC_PASSES * lanes_n)
    c0 = math.sqrt(2.0 / math.pi)
    mesh = plsc.VectorSubcoreMesh(core_axis_name="c", subcore_axis_name="s")
    dma = pltpu.SemaphoreType.DMA

    @functools.partial(
        pl.kernel, mesh=mesh,
        out_type=jax.ShapeDtypeStruct((n_tokens, D_MODEL), F32),
        scratch_types=[
            [pltpu.VMEM((nchunk, SC_ROWS), I32)] * 2,
            [pltpu.VMEM((D_MODEL,), F32)] * 2,
            [pltpu.VMEM((N_SLOTS,), F32)] * 2,
            [pltpu.VMEM((D_MODEL,), F32)] * 2,
            [pltpu.VMEM((SC_ROWS, D_MODEL), I32)] * 2,
            [dma] * 2, [dma] * 2, [dma] * 2, [dma] * 2, [dma] * 2,
        ],
        compiler_params=pltpu.CompilerParams(needs_layout_passes=False),
        name="peer_sc",
    )
    def sc_kernel(idx_hbm, xn_hbm, gate_hbm, uv_hbm, out_hbm, idx_v, x_v, g_v, out_v, rows_v,
                  row_sem, idx_sem, x_sem, g_sem, out_sem):
        wid = lax.axis_index("s") * nc + lax.axis_index("c")
        t0 = wid * per
        lane_ids = lax.iota(I32, lanes_n)
        zero = jnp.zeros((lanes_n,), F32)

        def vec(q, j):
            return pl.ds((q * qv + j) * lanes_n, lanes_n)

        def gather(p, c, b):
            return pltpu.make_async_copy(uv_hbm.at[idx_v[p].at[c]], rows_v[b], row_sem[b])

        def inputs(tok, p):
            return (pltpu.make_async_copy(idx_hbm.at[tok], idx_v[p], idx_sem[p]),
                    pltpu.make_async_copy(xn_hbm.at[tok], x_v[p], x_sem[p]),
                    pltpu.make_async_copy(gate_hbm.at[tok], g_v[p], g_sem[p]))

        def result(tok, p):
            return pltpu.make_async_copy(out_v[p], out_hbm.at[tok], out_sem[p])

        def chunk(p, c, b):
            rows = rows_v[b]

            def ustep(j, acc):
                xj = x_v[p][pl.ds(j * lanes_n, lanes_n)]
                return tuple(
                    acc[r] + plsc.bitcast(rows[r, pl.ds(j * lanes_n, lanes_n)] & U_HALF, F32) * xj
                    for r in range(SC_ROWS))

            acc = lax.fori_loop(0, D_MODEL // lanes_n, ustep, (zero,) * SC_ROWS)
            act = zero
            for r in range(SC_ROWS):
                act = jnp.where(lane_ids == r, jnp.sum(acc[r]), act)
            z = c0 * (act + 0.044715 * (act * act * act))
            tanh_z = 1.0 - 2.0 / (jnp.exp(2.0 * z) + 1.0)
            w = 0.5 * act * (1.0 + tanh_z) * g_v[p][pl.ds(c * SC_ROWS, SC_ROWS)]
            for q in range(SC_PASSES):
                o = tuple(out_v[p][vec(q, j)] for j in range(qv))

                def vrow(r, o, q=q):
                    wr = w.at[jnp.full((lanes_n,), r, I32)].get(mode="promise_in_bounds")
                    return tuple(o[j] + wr * plsc.bitcast(rows[r, vec(q, j)] << 16, F32)
                                 for j in range(qv))

                o = lax.fori_loop(0, SC_ROWS, vrow, o)
                for j in range(qv):
                    out_v[p][vec(q, j)] = o[j]

        def token(i, p):
            tok = t0 + i
            more = i + 1 < per

            @pl.when(more)
            def _():
                for cp in inputs(tok + 1, 1 - p):
                    cp.start()

            @pl.when(i >= 2)
            def _():
                result(tok - 2, p).wait()

            for j in range(D_MODEL // lanes_n):
                out_v[p][pl.ds(j * lanes_n, lanes_n)] = zero

            @pl.loop(0, nchunk, step=2)
            def _(c):
                gather(p, c + 1, 1).start()
                gather(p, c, 0).wait()
                chunk(p, c, 0)

                @pl.when(c + 2 < nchunk)
                def _():
                    gather(p, c + 2, 0).start()

                @pl.when(jnp.logical_and(c + 2 >= nchunk, more))
                def _():
                    for cp in inputs(tok + 1, 1 - p):
                        cp.wait()
                    gather(1 - p, 0, 0).start()

                gather(p, c + 1, 1).wait()
                chunk(p, c + 1, 1)

            result(tok, p).start()

        for cp in inputs(t0, 0):
            cp.start()
        for cp in inputs(t0, 0):
            cp.wait()
        gather(0, 0, 0).start()

        @pl.loop(0, per, step=2)
        def _(i):
            token(i, 0)
            token(i + 1, 1)

        result(t0 + per - 2, 0).wait()
        result(t0 + per - 1, 1).wait()

    return sc_kernel(idx.reshape(-1, nchunk, SC_ROWS), xn, gates, uv2)


def _residual_norm_kernel(x1_ref, o_ref, gf_ref, y_ref):
    y_ref[...] = _rms(x1_ref[...] + o_ref[...], gf_ref[...])


def _residual_norm(x1, out, gf):
    n = out.shape[0]
    nt = PEER_BLOCK
    row = lambda i: (i, 0)
    return pl.pallas_call(
        _residual_norm_kernel,
        grid=(n // nt,),
        in_specs=[pl.BlockSpec((nt, D_MODEL), row), pl.BlockSpec((nt, D_MODEL), row),
                  pl.BlockSpec((1, D_MODEL), lambda i: (0, 0))],
        out_specs=pl.BlockSpec((nt, D_MODEL), row),
        out_shape=jax.ShapeDtypeStruct((n, D_MODEL), F32),
        compiler_params=pltpu.CompilerParams(dimension_semantics=("arbitrary",)),
        name="residual_norm",
    )(x1, out, gf)


def _rope_tables(seq):
    pos = jnp.arange(seq, dtype=F32)
    inv = 1.0 / (ROPE_THETA ** (jnp.arange(0, HEAD_DIM, 2, dtype=F32) / HEAD_DIM))
    ang = pos[:, None] * inv[None, :]
    cos, sin = jnp.cos(ang), jnp.sin(ang)
    cosf = jnp.tile(jnp.concatenate([cos, cos], axis=1), (1, B_HEADS))
    sins = jnp.tile(jnp.concatenate([-sin, sin], axis=1), (1, B_HEADS))
    return cosf, sins


def kernel(x, norm1_g, w_in, ln_v_g, ln_v_b, w_spatial, b_spatial, out_norm_a_g, out_norm_b_g,
           w_out, norm2_g, w_query, sub_keys, expert_u, expert_v, final_norm_g):
    batch, seq, _ = x.shape
    assert w_in.shape[0] == 1 and seq % (16 * QBLK) == 0 and seq % IN_BLOCK == 0
    row = lambda g: g.reshape(1, -1).astype(F32)

    ws = w_spatial[0].astype(BF16)
    ws_cat = jnp.concatenate([ws[0::2], ws[1::2]], axis=2)
    bs_full = jnp.repeat(b_spatial[0].T, A_GROUP_DIM, axis=1)
    cosf, sins = _rope_tables(seq)
    win, wout, wq = w_in[0].astype(BF16), w_out[0].astype(BF16), w_query[0].astype(BF16)
    keys = sub_keys[0].astype(BF16)
    gf = row(final_norm_g)
    half = lambda t: lax.bitcast_convert_type(t.astype(BF16), jnp.uint16).astype(jnp.uint32)
    uv2 = lax.bitcast_convert_type((half(expert_u[0]) << 16) | half(expert_v[0]), I32)
    uv = uv2.reshape(-1, NCH, LANES)

    nchunk = PIPE_CHUNKS if batch % PIPE_CHUNKS == 0 else 1
    sizes = [batch // nchunk] * nchunk
    if sizes[0] % 2 == 0:
        sizes = [sizes[0] // 2, sizes[0] // 2] + sizes[1:]
    granule = PEER_BLOCK
    pieces = []
    b0 = 0
    for cb in sizes:
        x2 = x[b0:b0 + cb].reshape(cb * seq, D_MODEL)
        b0 += cb
        an, q1, k1, v1, q4, k4, v4, q16, k16, v16 = _in_proj(
            x2, row(norm1_g[0]), win, row(ln_v_g[0]), row(ln_v_b[0]),
            ws_cat, bs_full, row(out_norm_a_g[0]), cosf, sins, cb, seq)
        bo = _attention(q1, k1, v1, q4, k4, v4, q16, k16, v16, cb, seq).reshape(cb * seq, D_B)
        x1, xn, idx, gates = _mid(x2, an, bo, row(out_norm_b_g[0]), wout, row(norm2_g[0]), wq, keys)
        n_sc = (cb * seq) * SC_SHARE[0] // SC_SHARE[1] // granule * granule
        out_sc = _sc_peer(idx.reshape(cb * seq, N_SLOTS), xn, gates, uv2, n_sc)
        y_tc = _peer(idx, gates, xn, x1, gf, uv, n_sc)
        pieces += [_residual_norm(x1, out_sc, gf), y_tc]
    return jnp.concatenate(pieces, axis=0).reshape(batch, seq, D_MODEL)
```

```python
import functools
import math

import jax
import jax.numpy as jnp
from jax import lax
from jax.experimental import pallas as pl
from jax.experimental.pallas import tpu as pltpu
from jax.experimental.pallas import tpu_sc as plsc

F32 = jnp.float32
BF16 = jnp.bfloat16
I32 = jnp.int32

D_MODEL = 1024
D_A = 512
D_B = 512
A_GROUPS = 8
A_GROUP_DIM = 64
CHUNK = 128
B_HEADS = 8
HEAD_DIM = 64
DILATIONS = (1, 4, 16)
HALF_WINDOW = 64
ROPE_THETA = 10000.0
D_IN = 2 * D_A + 3 * D_B
N_KEYS = 128
PEER_HEADS = 8
PEER_TOPK = 16
D_KEY = 256
N_SLOTS = PEER_HEADS * PEER_TOPK
EPS = 1e-6
NEG_BIG = -1e30

LANES = 128
SUB = 8
NCH = D_MODEL // LANES
U_HALF = -65536
QBLK = 128
ATTN_UNROLL = 8
RES16_PITCH = 24
IN_BLOCK = 512
MID_BLOCK = 256
PEER_BLOCK = 128
PEER_RING = 4
SC_SHARE = (45, 64)
PIPE_CHUNKS = 8
SC_ROWS = 16
SC_PASSES = 4
VMEM_LIMIT = 48 * 1024 * 1024


def _gelu(x):
    c = math.sqrt(2.0 / math.pi)
    return 0.5 * x * (1.0 + jnp.tanh(c * (x + 0.044715 * (x * x * x))))


def _rms(x, g):
    return x * lax.rsqrt(jnp.mean(x * x, axis=-1, keepdims=True) + EPS) * g


def _in_proj_kernel(x_ref, g1_ref, win_ref, lng_ref, lnb_ref, ws_ref, bs_ref, ga_ref,
                    cos_ref, sin_ref,
                    an_ref, q1_ref, k1_ref, v1_ref, q4_ref, k4_ref, v4_ref,
                    q16_ref, k16_ref, v16_ref, slab_ref):
    nt = x_ref.shape[0]
    h = _rms(x_ref[...], g1_ref[...]).astype(BF16)
    proj = jnp.dot(h, win_ref[...], preferred_element_type=F32)

    u = _gelu(proj[:, :D_A])
    v = _gelu(proj[:, D_A:2 * D_A])
    mu = jnp.mean(v, axis=-1, keepdims=True)
    vc = v - mu
    var = jnp.mean(vc * vc, axis=-1, keepdims=True)
    vln = (vc * lax.rsqrt(var + EPS) * lng_ref[...] + lnb_ref[...]).astype(BF16)
    lane = lax.broadcasted_iota(I32, (CHUNK, LANES), 1)
    lo = lane < A_GROUP_DIM
    zero = jnp.zeros((CHUNK, LANES), BF16)
    chunks = []
    for c in range(nt // CHUNK):
        cols = []
        for j in range(A_GROUPS // 2):
            vv = vln[c * CHUNK:(c + 1) * CHUNK, j * LANES:(j + 1) * LANES]
            rhs = jnp.concatenate([jnp.where(lo, vv, zero), jnp.where(lo, zero, vv)], axis=0)
            cols.append(jnp.dot(ws_ref[j], rhs, preferred_element_type=F32))
        chunks.append(jnp.concatenate(cols, axis=1) + bs_ref[...])
    mixed = jnp.concatenate(chunks, axis=0)
    an_ref[...] = _rms(u * mixed, ga_ref[...]).astype(BF16)

    cosf = cos_ref[...]
    sins = sin_ref[...]
    lane_b = lax.broadcasted_iota(I32, (nt, D_B), 1)
    first_half = (lane_b % HEAD_DIM) < (HEAD_DIM // 2)

    def rope(t):
        partner = jnp.where(first_half, pltpu.roll(t, D_B - HEAD_DIM // 2, 1),
                            pltpu.roll(t, HEAD_DIM // 2, 1))
        return t * cosf + partner * sins

    q = rope(proj[:, 2 * D_A:2 * D_A + D_B]) * (HEAD_DIM ** -0.5)
    k = rope(proj[:, 2 * D_A + D_B:2 * D_A + 2 * D_B])
    vv = proj[:, 2 * D_A + 2 * D_B:]
    q1_ref[...] = q.astype(BF16)
    k1_ref[...] = k.astype(BF16)
    v1_ref[...] = vv.astype(BF16)

    nslab = D_B // LANES
    for a, t in enumerate((q, k, vv)):
        for s in range(nslab):
            slab_ref[a * nslab + s] = t[:, s * LANES:(s + 1) * LANES]
    for d, outs in ((4, (q4_ref, k4_ref, v4_ref)), (16, (q16_ref, k16_ref, v16_ref))):
        rows = nt // d
        for a, o_ref in enumerate(outs):
            for r in range(d):
                for s in range(nslab):
                    o_ref[r, :, s * LANES:(s + 1) * LANES] = (
                        slab_ref[a * nslab + s, pl.ds(r, rows, stride=d), :].astype(BF16))


def _in_proj(x2, g1, win, lng, lnb, ws_cat, bs_full, ga, cosf, sins, batch, seq):
    t_total = x2.shape[0]
    nt = IN_BLOCK
    nb = seq // nt
    grid = (t_total // nt,)
    row = lambda i: (i, 0)
    const2 = lambda i: (0, 0)
    tok_bf = jax.ShapeDtypeStruct((t_total, D_B), BF16)
    out_shape = (
        jax.ShapeDtypeStruct((t_total, D_A), BF16),
        tok_bf, tok_bf, tok_bf,
        *(jax.ShapeDtypeStruct((batch, 4, seq // 4, D_B), BF16),) * 3,
        *(jax.ShapeDtypeStruct((batch, 16, seq // 16, D_B), BF16),) * 3,
    )
    res4 = pl.BlockSpec((None, 4, nt // 4, D_B), lambda i: (i // nb, 0, i % nb, 0))
    res16 = pl.BlockSpec((None, 16, nt // 16, D_B), lambda i: (i // nb, 0, i % nb, 0))
    tok_spec = pl.BlockSpec((nt, D_B), row)
    return pl.pallas_call(
        _in_proj_kernel,
        grid=grid,
        in_specs=[
            pl.BlockSpec((nt, D_MODEL), row),
            pl.BlockSpec((1, D_MODEL), const2),
            pl.BlockSpec((D_MODEL, D_IN), const2),
            pl.BlockSpec((1, D_A), const2),
            pl.BlockSpec((1, D_A), const2),
            pl.BlockSpec((A_GROUPS // 2, CHUNK, 2 * CHUNK), lambda i: (0, 0, 0)),
            pl.BlockSpec((CHUNK, D_A), const2),
            pl.BlockSpec((1, D_A), const2),
            pl.BlockSpec((nt, D_B), lambda i: (i % nb, 0)),
            pl.BlockSpec((nt, D_B), lambda i: (i % nb, 0)),
        ],
        out_specs=(pl.BlockSpec((nt, D_A), row), tok_spec, tok_spec, tok_spec,
                   res4, res4, res4, res16, res16, res16),
        out_shape=out_shape,
        scratch_shapes=[pltpu.VMEM((3 * D_B // LANES, nt, LANES), F32)],
        compiler_params=pltpu.CompilerParams(
            dimension_semantics=("arbitrary",), vmem_limit_bytes=VMEM_LIMIT),
        name="in_proj",
    )(x2, g1, win, lng, lnb, ws_cat, bs_full, ga, cosf, sins)


def _attn_kernel(q1_ref, k1_ref, v1_ref, q4_ref, k4_ref, v4_ref, q16_ref, k16_ref, v16_ref,
                 o_ref, out_ref, lse_ref, out16_ref, lse16_ref):
    seq = o_ref.shape[0]
    lane = lax.broadcasted_iota(I32, (QBLK, LANES), 1)
    head0 = lane < HEAD_DIM
    refs = ((q1_ref, k1_ref, v1_ref), (q4_ref, k4_ref, v4_ref), (q16_ref, k16_ref, v16_ref))
    for bi, (d, (q_ref, k_ref, v_ref)) in enumerate(zip(DILATIONS, refs)):
        length = seq // d
        nblk = length // QBLK
        win = min(2 * QBLK, length)
        diff = (lax.broadcasted_iota(I32, (QBLK, win), 1)
                - lax.broadcasted_iota(I32, (QBLK, win), 0))

        def block(blk, carry, d=d, bi=bi, q_ref=q_ref, k_ref=k_ref, v_ref=v_ref,
                  length=length, nblk=nblk, win=win, diff=diff):
            r = blk // nblk
            i0 = pl.multiple_of((blk % nblk) * QBLK, QBLK)
            w0 = pl.multiple_of(jnp.clip(i0 - HALF_WINDOW, 0, length - win), HALF_WINDOW)
            qb = q_ref[r, pl.ds(i0, QBLK), :]
            kw = k_ref[r, pl.ds(w0, win), :]
            vw = v_ref[r, pl.ds(w0, win), :]
            rel = diff + (w0 - i0)
            valid = (rel >= -HALF_WINDOW) & (rel <= HALF_WINDOW)
            zero = jnp.zeros_like(qb)
            qq = jnp.concatenate([jnp.where(head0, qb, zero), jnp.where(head0, zero, qb)], axis=0)
            s = lax.dot_general(qq, kw, (((1,), (1,)), ((), ())), preferred_element_type=F32)
            s = jnp.where(jnp.concatenate([valid, valid], axis=0), s, NEG_BIG)
            m = jnp.max(s, axis=1, keepdims=True)
            p = jnp.exp(s - m)
            l = jnp.sum(p, axis=1, keepdims=True)
            pv = jnp.dot(p.astype(BF16), vw, preferred_element_type=F32) / l
            ml = m + jnp.log(l)
            out = jnp.where(head0, pv[:QBLK], pv[QBLK:])
            lse = jnp.where(head0, ml[:QBLK], ml[QBLK:])
            if d == 1:
                out_ref[0, pl.ds(i0, QBLK), :] = out
                lse_ref[0, pl.ds(i0, QBLK), :] = lse
            elif d == 4:
                rows = pl.ds(i0 * d + r, QBLK, stride=d)
                out_ref[1, rows, :] = out
                lse_ref[1, rows, :] = lse
            else:
                rows = pl.ds(r, QBLK, stride=RES16_PITCH)
                out16_ref[rows, :] = out
                lse16_ref[rows, :] = lse
            return carry

        lax.fori_loop(0, d * nblk, block, 0, unroll=ATTN_UNROLL)

    groups = QBLK // 16

    def merge(c, carry):
        rows = pl.ds(pl.multiple_of(c * QBLK, QBLK), QBLK)
        base = pl.multiple_of(c * (groups * RES16_PITCH), SUB)
        pieces = [pl.ds(base + g * RES16_PITCH, 16) for g in range(groups)]
        o3 = jnp.concatenate([out16_ref[pc, :] for pc in pieces], axis=0)
        e3 = jnp.concatenate([lse16_ref[pc, :] for pc in pieces], axis=0)
        e1, e2 = lse_ref[0, rows, :], lse_ref[1, rows, :]
        mx = jnp.maximum(jnp.maximum(e1, e2), e3)
        w1, w2, w3 = jnp.exp(e1 - mx), jnp.exp(e2 - mx), jnp.exp(e3 - mx)
        num = w1 * out_ref[0, rows, :] + w2 * out_ref[1, rows, :] + w3 * o3
        o_ref[rows, :] = num / (w1 + w2 + w3)
        return carry

    lax.fori_loop(0, seq // QBLK, merge, 0)


def _attention(q1, k1, v1, q4, k4, v4, q16, k16, v16, batch, seq):
    npair = D_B // LANES
    nat = pl.BlockSpec((None, 1, seq, LANES), lambda b, p: (b, 0, 0, p))
    r4 = pl.BlockSpec((None, 4, seq // 4, LANES), lambda b, p: (b, 0, 0, p))
    r16 = pl.BlockSpec((None, 16, seq // 16, LANES), lambda b, p: (b, 0, 0, p))
    q1, k1, v1 = (t.reshape(batch, 1, seq, D_B) for t in (q1, k1, v1))
    return pl.pallas_call(
        _attn_kernel,
        grid=(batch, npair),
        in_specs=[nat, nat, nat, r4, r4, r4, r16, r16, r16],
        out_specs=pl.BlockSpec((None, seq, LANES), lambda b, p: (b, 0, p)),
        out_shape=jax.ShapeDtypeStruct((batch, seq, D_B), F32),
        scratch_shapes=[pltpu.VMEM((2, seq, LANES), F32)] * 2
        + [pltpu.VMEM((seq // 16 * RES16_PITCH, LANES), F32)] * 2,
        compiler_params=pltpu.CompilerParams(
            dimension_semantics=("arbitrary", "arbitrary"), vmem_limit_bytes=VMEM_LIMIT),
        name="dilated_attn",
    )(q1, k1, v1, q4, k4, v4, q16, k16, v16)


def _topk_rows(s, k):
    n = s.shape[0]
    iota = lax.broadcasted_iota(I32, s.shape, 0).astype(F32)
    vals, idxs = [], []
    for _ in range(k):
        m = jnp.max(s, axis=0, keepdims=True)
        i = jnp.min(jnp.where(s == m, iota, float(n)), axis=0, keepdims=True)
        vals.append(m)
        idxs.append(i)
        s = jnp.where(iota == i, -jnp.inf, s)
    return jnp.concatenate(vals, axis=0), jnp.concatenate(idxs, axis=0).astype(I32)


def _take_rows(table, sel):
    out = jnp.zeros(sel.shape, table.dtype)
    for a in range(table.shape[0]):
        out = jnp.where(sel == a, table[a:a + 1, :], out)
    return out


def _mid_kernel(x_ref, an_ref, bo_ref, gb_ref, wout_ref, g2_ref, wq_ref, keys_ref,
                x1_ref, xn_ref, idx_ref, gate_ref):
    nt = x_ref.shape[0]
    bn = _rms(bo_ref[...], gb_ref[...]).astype(BF16)
    x1 = (x_ref[...]
          + jnp.dot(an_ref[...], wout_ref[:D_A, :], preferred_element_type=F32)
          + jnp.dot(bn, wout_ref[D_A:, :], preferred_element_type=F32))
    x1_ref[...] = x1
    xn = _rms(x1, g2_ref[...])
    xn_ref[...] = xn
    q = jnp.dot(xn.astype(BF16), wq_ref[...], preferred_element_type=F32).astype(BF16)
    keys = (keys_ref[0], keys_ref[1])
    half = D_KEY // 2
    for c in range(nt // LANES):
        qc = q[c * LANES:(c + 1) * LANES, :]
        experts, gates = [], []
        for h in range(PEER_HEADS):
            tops = []
            for p in range(2):
                qhp = qc[:, (2 * h + p) * half:(2 * h + p + 1) * half]
                s = lax.dot_general(keys[p], qhp, (((1,), (1,)), ((), ())),
                                    preferred_element_type=F32)
                tops.append(_topk_rows(s, PEER_TOPK))
            (s1, i1), (s2, i2) = tops
            cand = jnp.concatenate(
                [s1[0:1, :] + s2]
                + [s1[a:a + 1, :] + s2[0:SUB, :] for a in range(1, SUB)]
                + [s1[SUB:, :] + s2[0:1, :]], axis=0)
            sc, pos = _topk_rows(cand, PEER_TOPK)
            ca = jnp.where(pos < PEER_TOPK, 0,
                           jnp.where(pos < PEER_TOPK + SUB * (SUB - 1), (pos >> 3) - 1, pos - SUB * SUB))
            cb = jnp.where(pos < PEER_TOPK, pos,
                           jnp.where(pos < PEER_TOPK + SUB * (SUB - 1), pos & (SUB - 1), 0))
            e = _take_rows(i1, ca) * N_KEYS + _take_rows(i2, cb)
            ex = jnp.exp(sc - sc[0:1, :])
            gates.append(ex / jnp.sum(ex, axis=0, keepdims=True))
            experts.append(e)
        idx_ref[c] = jnp.concatenate(experts, axis=0).T
        gate_ref[c * LANES:(c + 1) * LANES, :] = jnp.concatenate(gates, axis=0).T


def _mid(x2, an, bo, gb, wout, g2, wq, keys):
    t_total = x2.shape[0]
    nt = MID_BLOCK
    row = lambda i: (i, 0)
    const2 = lambda i: (0, 0)
    return pl.pallas_call(
        _mid_kernel,
        grid=(t_total // nt,),
        in_specs=[
            pl.BlockSpec((nt, D_MODEL), row),
            pl.BlockSpec((nt, D_A), row),
            pl.BlockSpec((nt, D_B), row),
            pl.BlockSpec((1, D_B), const2),
            pl.BlockSpec((D_MODEL, D_MODEL), const2),
            pl.BlockSpec((1, D_MODEL), const2),
            pl.BlockSpec((D_MODEL, PEER_HEADS * D_KEY), const2),
            pl.BlockSpec((2, N_KEYS, D_KEY // 2), lambda i: (0, 0, 0)),
        ],
        out_specs=(
            pl.BlockSpec((nt, D_MODEL), row),
            pl.BlockSpec((nt, D_MODEL), row),
            pl.BlockSpec((nt // LANES, LANES, N_SLOTS), lambda i: (i, 0, 0)),
            pl.BlockSpec((nt, N_SLOTS), row),
        ),
        out_shape=(
            jax.ShapeDtypeStruct((t_total, D_MODEL), F32),
            jax.ShapeDtypeStruct((t_total, D_MODEL), F32),
            jax.ShapeDtypeStruct((t_total // LANES, LANES, N_SLOTS), I32),
            jax.ShapeDtypeStruct((t_total, N_SLOTS), F32),
        ),
        compiler_params=pltpu.CompilerParams(
            dimension_semantics=("arbitrary",), vmem_limit_bytes=VMEM_LIMIT),
        name="mid",
    )(x2, an, bo, gb, wout, g2, wq, keys)


def _peer_kernel(idx_ref, nidx_ref, gate_ref, xn_ref, x1_ref, gf_ref, uv_ref, y_ref, *scratch):
    rows_refs = scratch[:PEER_RING]
    bf_ref, sem_ref = scratch[PEER_RING:]
    nt = xn_ref.shape[0]
    wide = 2 * LANES
    nw = 2 * D_MODEL // wide

    step = pl.program_id(0)
    last_step = step == pl.num_programs(0) - 1

    def issue(t, slot, lo=0, hi=N_SLOTS, ids=idx_ref):
        for s in range(lo, hi):
            pltpu.make_async_copy(uv_ref.at[ids[t, s]],
                                  rows_refs[slot].at[s // SUB, :, s % SUB, :],
                                  sem_ref.at[slot]).start(priority=s % 2)

    def wait(slot):
        pltpu.make_async_copy(uv_ref.at[pl.ds(0, N_SLOTS)],
                              rows_refs[slot].reshape(N_SLOTS, NCH, LANES), sem_ref.at[slot]).wait()

    def stage(slot, k):
        for j in range(NCH):
            w = rows_refs[slot][:, j, :, :].reshape(N_SLOTS, LANES)
            bf_ref[k, :, j * LANES:(j + 1) * LANES] = (
                pltpu.bitcast(w & U_HALF, F32).astype(BF16))
            bf_ref[k, :, D_MODEL + j * LANES:D_MODEL + (j + 1) * LANES] = (
                pltpu.bitcast(w << 16, F32).astype(BF16))

    def pair(t0, slots, prefetch):
        for k in range(2):
            wait(slots[k])
            stage(slots[k], k)
        if prefetch == "next":
            @pl.when(jnp.logical_not(last_step))
            def _():
                for k in range(2):
                    issue(t0 + PEER_RING + k - nt, slots[k], ids=nidx_ref)
        nbatch = 2 * nw
        per = N_SLOTS // (nbatch // 2)
        batches = [(k, b * per, (b + 1) * per) for b in range(nbatch // 2) for k in range(2)]

        def next_batch():
            if prefetch == "same" and batches:
                k, lo, hi = batches.pop(0)
                issue(t0 + PEER_RING + k, slots[k], lo, hi)

        x8 = [jnp.broadcast_to(xn_ref[pl.ds(t0 + k, 1), :], (SUB, D_MODEL)).astype(BF16)
              for k in range(2)]
        act = [jnp.zeros((SUB, N_SLOTS), F32) for _ in range(2)]
        for j in range(nw // 2):
            for k in range(2):
                next_batch()
                act[k] = act[k] + lax.dot_general(
                    x8[k][:, j * wide:(j + 1) * wide], bf_ref[k, :, j * wide:(j + 1) * wide],
                    (((1,), (1,)), ((), ())), preferred_element_type=F32)
        w = [(_gelu(act[k]) * gate_ref[pl.ds(t0 + k, 1), :]).astype(BF16) for k in range(2)]
        outs = [[], []]
        for j in range(nw // 2):
            for k in range(2):
                next_batch()
                outs[k].append(jnp.dot(
                    w[k], bf_ref[k, :, D_MODEL + j * wide:D_MODEL + (j + 1) * wide],
                    preferred_element_type=F32)[0:1, :])
        for k in range(2):
            out = jnp.concatenate(outs[k], axis=1)
            y_ref[pl.ds(t0 + k, 1), :] = _rms(x1_ref[pl.ds(t0 + k, 1), :] + out, gf_ref[...])

    def group(g, prefetch):
        for p in range(PEER_RING // 2):
            pair(g * PEER_RING + 2 * p, (2 * p, 2 * p + 1), prefetch)

    @pl.when(step == 0)
    def _():
        for t in range(PEER_RING):
            issue(t, t)

    ngroup = nt // PEER_RING
    lax.fori_loop(0, ngroup - 1, lambda g, c: (group(g, "same"), c)[1], 0)
    group(ngroup - 1, "next")


def _peer(idx, gates, xn, x1, gf, uv, first_token):
    t_total = xn.shape[0] - first_token
    nt = PEER_BLOCK
    b0 = first_token // nt
    last_block = xn.shape[0] // nt - 1
    row = lambda i: (i + b0, 0)
    return pl.pallas_call(
        _peer_kernel,
        grid=(t_total // nt,),
        in_specs=[
            pl.BlockSpec((None, nt, N_SLOTS), lambda i: (i + b0, 0, 0), memory_space=pltpu.SMEM),
            pl.BlockSpec((None, nt, N_SLOTS), lambda i: (jnp.minimum(i + b0 + 1, last_block), 0, 0),
                         memory_space=pltpu.SMEM),
            pl.BlockSpec((nt, N_SLOTS), row),
            pl.BlockSpec((nt, D_MODEL), row),
            pl.BlockSpec((nt, D_MODEL), row),
            pl.BlockSpec((1, D_MODEL), lambda i: (0, 0)),
            pl.BlockSpec(memory_space=pl.ANY),
        ],
        out_specs=pl.BlockSpec((nt, D_MODEL), lambda i: (i, 0)),
        out_shape=jax.ShapeDtypeStruct((t_total, D_MODEL), F32),
        scratch_shapes=[pltpu.VMEM((N_SLOTS // SUB, NCH, SUB, LANES), I32)] * PEER_RING + [
            pltpu.VMEM((2, N_SLOTS, 2 * D_MODEL), BF16),
            pltpu.SemaphoreType.DMA((PEER_RING,))],
        compiler_params=pltpu.CompilerParams(
            dimension_semantics=("arbitrary",), vmem_limit_bytes=VMEM_LIMIT),
        name="peer",
    )(idx, idx, gates, xn, x1, gf, uv)


def _sc_peer(idx, xn, gates, uv2, n_tokens):
    info = plsc.get_sparse_core_info()
    nc, lanes_n = info.num_cores, info.num_lanes
    nw = nc * info.num_subcores
    per = n_tokens // nw
    assert n_tokens % nw == 0 and per % 2 == 0
    nchunk = N_SLOTS // SC_ROWS
    qv = D_MODEL // (SC_PASSES * lanes_n)
    c0 = math.sqrt(2.0 / math.pi)
    mesh = plsc.VectorSubcoreMesh(core_axis_name="c", subcore_axis_name="s")
    dma = pltpu.SemaphoreType.DMA

    @functools.partial(
        pl.kernel, mesh=mesh,
        out_type=jax.ShapeDtypeStruct((n_tokens, D_MODEL), F32),
        scratch_types=[
            [pltpu.VMEM((nchunk, SC_ROWS), I32)] * 2,
            [pltpu.VMEM((D_MODEL,), F32)] * 2,
            [pltpu.VMEM((N_SLOTS,), F32)] * 2,
            [pltpu.VMEM((D_MODEL,), F32)] * 2,
            [pltpu.VMEM((SC_ROWS, D_MODEL), I32)] * 2,
            [dma] * 2, [dma] * 2, [dma] * 2, [dma] * 2, [dma] * 2,
        ],
        compiler_params=pltpu.CompilerParams(needs_layout_passes=False),
        name="peer_sc",
    )
    def sc_kernel(idx_hbm, xn_hbm, gate_hbm, uv_hbm, out_hbm, idx_v, x_v, g_v, out_v, rows_v,
                  row_sem, idx_sem, x_sem, g_sem, out_sem):
        wid = lax.axis_index("s") * nc + lax.axis_index("c")
        t0 = wid * per
        lane_ids = lax.iota(I32, lanes_n)
        zero = jnp.zeros((lanes_n,), F32)

        def vec(q, j):
            return pl.ds((q * qv + j) * lanes_n, lanes_n)

        def gather(p, c, b):
            return pltpu.make_async_copy(uv_hbm.at[idx_v[p].at[c]], rows_v[b], row_sem[b])

        def inputs(tok, p):
            return (pltpu.make_async_copy(idx_hbm.at[tok], idx_v[p], idx_sem[p]),
                    pltpu.make_async_copy(xn_hbm.at[tok], x_v[p], x_sem[p]),
                    pltpu.make_async_copy(gate_hbm.at[tok], g_v[p], g_sem[p]))

        def result(tok, p):
            return pltpu.make_async_copy(out_v[p], out_hbm.at[tok], out_sem[p])

        def chunk(p, c, b):
            rows = rows_v[b]

            def ustep(j, acc):
                xj = x_v[p][pl.ds(j * lanes_n, lanes_n)]
                return tuple(
                    acc[r] + plsc.bitcast(rows[r, pl.ds(j * lanes_n, lanes_n)] & U_HALF, F32) * xj
                    for r in range(SC_ROWS))

            acc = lax.fori_loop(0, D_MODEL // lanes_n, ustep, (zero,) * SC_ROWS)
            act = zero
            for r in range(SC_ROWS):
                act = jnp.where(lane_ids == r, jnp.sum(acc[r]), act)
            z = c0 * (act + 0.044715 * (act * act * act))
            tanh_z = 1.0 - 2.0 / (jnp.exp(2.0 * z) + 1.0)
            w = 0.5 * act * (1.0 + tanh_z) * g_v[p][pl.ds(c * SC_ROWS, SC_ROWS)]
            for q in range(SC_PASSES):
                o = tuple(out_v[p][vec(q, j)] for j in range(qv))

                def vrow(r, o, q=q):
                    wr = w.at[jnp.full((lanes_n,), r, I32)].get(mode="promise_in_bounds")
                    return tuple(o[j] + wr * plsc.bitcast(rows[r, vec(q, j)] << 16, F32)
                                 for j in range(qv))

                o = lax.fori_loop(0, SC_ROWS, vrow, o)
                for j in range(qv):
                    out_v[p][vec(q, j)] = o[j]

        def token(i, p):
            tok = t0 + i
            more = i + 1 < per

            @pl.when(more)
            def _():
                for cp in inputs(tok + 1, 1 - p):
                    cp.start()

            @pl.when(i >= 2)
            def _():
                result(tok - 2, p).wait()

            for j in range(D_MODEL // lanes_n):
                out_v[p][pl.ds(j * lanes_n, lanes_n)] = zero

            @pl.loop(0, nchunk, step=2)
            def _(c):
                gather(p, c + 1, 1).start()
                gather(p, c, 0).wait()
                chunk(p, c, 0)

                @pl.when(c + 2 < nchunk)
                def _():
                    gather(p, c + 2, 0).start()

                @pl.when(jnp.logical_and(c + 2 >= nchunk, more))
                def _():
                    for cp in inputs(tok + 1, 1 - p):
                        cp.wait()
                    gather(1 - p, 0, 0).start()

                gather(p, c + 1, 1).wait()
                chunk(p, c + 1, 1)

            result(tok, p).start()

        for cp in inputs(t0, 0):
            cp.start()
        for cp in inputs(t0, 0):
            cp.wait()
        gather(0, 0, 0).start()

        @pl.loop(0, per, step=2)
        def _(i):
            token(i, 0)
            token(i + 1, 1)

        result(t0 + per - 2, 0).wait()
        result(t0 + per - 1, 1).wait()

    return sc_kernel(idx.reshape(-1, nchunk, SC_ROWS), xn, gates, uv2)


def _residual_norm_kernel(x1_ref, o_ref, gf_ref, y_ref):
    y_ref[...] = _rms(x1_ref[...] + o_ref[...], gf_ref[...])


def _residual_norm(x1, out, gf):
    n = out.shape[0]
    nt = PEER_BLOCK
    row = lambda i: (i, 0)
    return pl.pallas_call(
        _residual_norm_kernel,
        grid=(n // nt,),
        in_specs=[pl.BlockSpec((nt, D_MODEL), row), pl.BlockSpec((nt, D_MODEL), row),
                  pl.BlockSpec((1, D_MODEL), lambda i: (0, 0))],
        out_specs=pl.BlockSpec((nt, D_MODEL), row),
        out_shape=jax.ShapeDtypeStruct((n, D_MODEL), F32),
        compiler_params=pltpu.CompilerParams(dimension_semantics=("arbitrary",)),
        name="residual_norm",
    )(x1, out, gf)


def _rope_tables(seq):
    pos = jnp.arange(seq, dtype=F32)
    inv = 1.0 / (ROPE_THETA ** (jnp.arange(0, HEAD_DIM, 2, dtype=F32) / HEAD_DIM))
    ang = pos[:, None] * inv[None, :]
    cos, sin = jnp.cos(ang), jnp.sin(ang)
    cosf = jnp.tile(jnp.concatenate([cos, cos], axis=1), (1, B_HEADS))
    sins = jnp.tile(jnp.concatenate([-sin, sin], axis=1), (1, B_HEADS))
    return cosf, sins


def kernel(x, norm1_g, w_in, ln_v_g, ln_v_b, w_spatial, b_spatial, out_norm_a_g, out_norm_b_g,
           w_out, norm2_g, w_query, sub_keys, expert_u, expert_v, final_norm_g):
    batch, seq, _ = x.shape
    assert w_in.shape[0] == 1 and seq % (16 * QBLK) == 0 and seq % IN_BLOCK == 0
    row = lambda g: g.reshape(1, -1).astype(F32)

    ws = w_spatial[0].astype(BF16)
    ws_cat = jnp.concatenate([ws[0::2], ws[1::2]], axis=2)
    bs_full = jnp.repeat(b_spatial[0].T, A_GROUP_DIM, axis=1)
    cosf, sins = _rope_tables(seq)
    win, wout, wq = w_in[0].astype(BF16), w_out[0].astype(BF16), w_query[0].astype(BF16)
    keys = sub_keys[0].astype(BF16)
    gf = row(final_norm_g)
    half = lambda t: lax.bitcast_convert_type(t.astype(BF16), jnp.uint16).astype(jnp.uint32)
    uv2 = lax.bitcast_convert_type((half(expert_u[0]) << 16) | half(expert_v[0]), I32)
    uv = uv2.reshape(-1, NCH, LANES)

    nchunk = PIPE_CHUNKS if batch % PIPE_CHUNKS == 0 else 1
    cb = batch // nchunk
    pieces = []
    for ci in range(nchunk):
        x2 = x[ci * cb:(ci + 1) * cb].reshape(cb * seq, D_MODEL)
        an, q1, k1, v1, q4, k4, v4, q16, k16, v16 = _in_proj(
            x2, row(norm1_g[0]), win, row(ln_v_g[0]), row(ln_v_b[0]),
            ws_cat, bs_full, row(out_norm_a_g[0]), cosf, sins, cb, seq)
        bo = _attention(q1, k1, v1, q4, k4, v4, q16, k16, v16, cb, seq).reshape(cb * seq, D_B)
        x1, xn, idx, gates = _mid(x2, an, bo, row(out_norm_b_g[0]), wout, row(norm2_g[0]), wq, keys)
        n_sc = (cb * seq) * SC_SHARE[0] // SC_SHARE[1]
        assert n_sc % PEER_BLOCK == 0
        out_sc = _sc_peer(idx.reshape(cb * seq, N_SLOTS), xn, gates, uv2, n_sc)
        y_tc = _peer(idx, gates, xn, x1, gf, uv, n_sc)
        pieces += [_residual_norm(x1, out_sc, gf), y_tc]
    return jnp.concatenate(pieces, axis=0).reshape(batch, seq, D_MODEL)
```
